```python
import jax, jax.numpy as jnp
from jax import lax
import numpy as np

D_MODEL = 2048
BATCH = 4
SEQ = 2048
DEPTH = 1

PLE_DIM = 256
POOL_WINDOWS = (2, 4, 8, 16)
N_POOL_GROUPS = 4
POOL_WIDTH = D_MODEL // 2
POOL_GROUP = POOL_WIDTH // N_POOL_GROUPS
CONV_WIDTH = D_MODEL // 2
CONV_K = 3
N_IN = POOL_WIDTH + 3 * CONV_WIDTH + 2 * D_MODEL
N_EXPERTS = 32
TOP_K = 4
D_FF = D_MODEL
SWIGLU_LIMIT = 7.0
SWIGLU_ALPHA = 1.702
MOE_BLOCK = 128
LN_EPS = 1e-5
DEEPNORM_ALPHA = (2.0 * DEPTH) ** 0.25
DEEPNORM_BETA = (8.0 * DEPTH) ** -0.25

kernel_name = "hybrid_pool_shortconv_moe_deepnorm"


def layer_norm(x, g, b):
    xf = x.astype(jnp.float32)
    mu = jnp.mean(xf, axis=-1, keepdims=True)
    var = jnp.mean(jnp.square(xf - mu), axis=-1, keepdims=True)
    y = (xf - mu) * lax.rsqrt(var + LN_EPS)
    return (y * g.astype(jnp.float32) + b.astype(jnp.float32)).astype(x.dtype)


def causal_multiscale_pool(u):
    B, T, _ = u.shape
    ug = u.reshape(B, T, N_POOL_GROUPS, POOL_GROUP).astype(jnp.float32)
    cs = jnp.cumsum(ug, axis=1)
    t1 = jnp.arange(1, T + 1, dtype=jnp.float32)
    outs = []
    for g, w in enumerate(POOL_WINDOWS):
        cp = jnp.pad(cs[:, :, g], ((0, 0), (w, 0), (0, 0)))
        win_sum = cp[:, w:] - cp[:, :T]
        cnt = jnp.minimum(t1, float(w))[None, :, None]
        outs.append(win_sum / cnt)
    pooled = jnp.stack(outs, axis=2)
    return (pooled - ug).astype(u.dtype)


def causal_short_conv(s, conv_w):
    T = s.shape[1]
    sp = jnp.pad(s, ((0, 0), (CONV_K - 1, 0), (0, 0)))
    acc = sp[:, 0:T] * conv_w[0]
    for k in range(1, CONV_K):
        acc = acc + sp[:, k:k + T] * conv_w[k]
    return acc


def token_mixer(x, w_in, w_pool, pool_scale, conv_w, w_br_a, w_br_b, w_o):
    B, T, _ = x.shape
    proj = x @ w_in
    o0 = POOL_WIDTH
    o1 = o0 + CONV_WIDTH
    o2 = o1 + CONV_WIDTH
    o3 = o2 + CONV_WIDTH
    o4 = o3 + D_MODEL
    u_a = proj[..., :o0]
    c_b = proj[..., o0:o1]
    c_c = proj[..., o1:o2]
    c_v = proj[..., o2:o3]
    g_a = proj[..., o3:o4]
    g_b = proj[..., o4:]
    pooled = causal_multiscale_pool(u_a)
    a = jnp.einsum('btgc,gcd->btgd', pooled, w_pool).reshape(B, T, POOL_WIDTH) * pool_scale
    b = c_b * causal_short_conv(c_c * c_v, conv_w)
    m = jax.nn.sigmoid(g_a) * (a @ w_br_a) + jax.nn.sigmoid(g_b) * (b @ w_br_b)
    return m @ w_o


def moe(h, w_router, b_router, w_gu, b_gu, w_down, b_down):
    B, T, D = h.shape
    N = B * T
    NK = N * TOP_K
    hf = h.reshape(N, D)
    logits = (hf @ w_router + b_router).astype(jnp.float32)
    top_v, top_e = lax.top_k(logits, TOP_K)
    gates = jax.nn.softmax(top_v, axis=-1).astype(h.dtype)
    e_flat = top_e.reshape(NK).astype(jnp.int32)
    g_flat = gates.reshape(NK)
    tok_flat = jnp.arange(NK, dtype=jnp.int32) // TOP_K
    onehot = jax.nn.one_hot(e_flat, N_EXPERTS, dtype=jnp.int32)
    rank = jnp.sum(jnp.cumsum(onehot, axis=0) * onehot, axis=-1) - 1
    counts = jnp.sum(onehot, axis=0)
    padded = (counts + MOE_BLOCK - 1) // MOE_BLOCK * MOE_BLOCK
    pad_ends = jnp.cumsum(padded)
    pad_starts = pad_ends - padded
    dest = pad_starts[e_flat] + rank
    n_blocks = (NK + N_EXPERTS * (MOE_BLOCK - 1) + MOE_BLOCK - 1) // MOE_BLOCK
    cap = n_blocks * MOE_BLOCK
    slot_tok = jnp.full((cap,), N, jnp.int32).at[dest].set(tok_flat)
    slot_gate = jnp.zeros((cap,), h.dtype).at[dest].set(g_flat)
    block_starts = jnp.arange(n_blocks, dtype=jnp.int32) * MOE_BLOCK
    block_e = jnp.minimum(jnp.searchsorted(pad_ends, block_starts, side='right'), N_EXPERTS - 1)
    h_pad = jnp.concatenate([hf, jnp.zeros((1, D), h.dtype)], axis=0)
    xs = h_pad[slot_tok].reshape(n_blocks, MOE_BLOCK, D)

    def expert_block(args):
        xb, e = args
        gu = xb @ w_gu[e] + b_gu[e]
        gate = jnp.minimum(gu[:, :D_FF], SWIGLU_LIMIT)
        up = jnp.clip(gu[:, D_FF:], -SWIGLU_LIMIT, SWIGLU_LIMIT)
        act = (up + 1.0) * (gate * jax.nn.sigmoid(SWIGLU_ALPHA * gate))
        return act @ w_down[e] + b_down[e]

    ys = lax.map(expert_block, (xs, block_e)).reshape(cap, D)
    out = jax.ops.segment_sum(ys * slot_gate[:, None], slot_tok, num_segments=N + 1)[:N]
    return out.reshape(B, T, D)


def setup_inputs(seed: int = 0) -> dict:
    key = jax.random.key(seed)
    ks = jax.random.split(key, 26)
    f32 = jnp.float32
    L = DEPTH

    def nrm(k, shape, scale):
        return jax.random.normal(k, shape, f32) * scale

    return {
        "x": nrm(ks[0], (BATCH, SEQ, D_MODEL), 1.0),
        "p": nrm(ks[1], (DEPTH, BATCH, SEQ, PLE_DIM), 1.0),
        "w_in": nrm(ks[2], (L, D_MODEL, N_IN), D_MODEL ** -0.5),
        "w_pool": nrm(ks[3], (L, N_POOL_GROUPS, POOL_GROUP, POOL_GROUP), POOL_GROUP ** -0.5),
        "pool_scale": 1.0 + nrm(ks[4], (L, POOL_WIDTH), 0.02),
        "conv_w": nrm(ks[5], (L, CONV_K, CONV_WIDTH), CONV_K ** -0.5),
        "w_br_a": nrm(ks[6], (L, POOL_WIDTH, D_MODEL), POOL_WIDTH ** -0.5 * DEEPNORM_BETA),
        "w_br_b": nrm(ks[7], (L, CONV_WIDTH, D_MODEL), CONV_WIDTH ** -0.5 * DEEPNORM_BETA),
        "w_o": nrm(ks[8], (L, D_MODEL, D_MODEL), D_MODEL ** -0.5 * DEEPNORM_BETA),
        "ln1_g": 1.0 + nrm(ks[9], (L, D_MODEL), 0.02),
        "ln1_b": nrm(ks[10], (L, D_MODEL), 0.02),
        "w_router": nrm(ks[11], (L, D_MODEL, N_EXPERTS), D_MODEL ** -0.5),
        "b_router": nrm(ks[12], (L, N_EXPERTS), 0.01),
        "w_gu": nrm(ks[13], (L, N_EXPERTS, D_MODEL, 2 * D_FF), D_MODEL ** -0.5),
        "b_gu": nrm(ks[14], (L, N_EXPERTS, 2 * D_FF), 0.01),
        "w_down": nrm(ks[15], (L, N_EXPERTS, D_FF, D_MODEL), D_FF ** -0.5 * DEEPNORM_BETA),
        "b_down": nrm(ks[16], (L, N_EXPERTS, D_MODEL), 0.01),
        "ln2_g": 1.0 + nrm(ks[17], (L, D_MODEL), 0.02),
        "ln2_b": nrm(ks[18], (L, D_MODEL), 0.02),
        "w_pg": nrm(ks[19], (L, D_MODEL, D_MODEL), D_MODEL ** -0.5),
        "b_pg": nrm(ks[20], (L, D_MODEL), 0.01),
        "w_ple": nrm(ks[21], (L, PLE_DIM, D_MODEL), PLE_DIM ** -0.5 * DEEPNORM_BETA),
        "ln3_g": 1.0 + nrm(ks[22], (L, D_MODEL), 0.02),
        "ln3_b": nrm(ks[23], (L, D_MODEL), 0.02),
    }


def reference(x, p, w_in, w_pool, pool_scale, conv_w, w_br_a, w_br_b, w_o, ln1_g, ln1_b,
              w_router, b_router, w_gu, b_gu, w_down, b_down, ln2_g, ln2_b,
              w_pg, b_pg, w_ple, ln3_g, ln3_b):
    for i in range(DEPTH):
        mix = token_mixer(x, w_in[i], w_pool[i], pool_scale[i], conv_w[i],
                          w_br_a[i], w_br_b[i], w_o[i])
        x1 = layer_norm(DEEPNORM_ALPHA * x + mix, ln1_g[i], ln1_b[i])
        ffn = moe(x1, w_router[i], b_router[i], w_gu[i], b_gu[i], w_down[i], b_down[i])
        h2 = layer_norm(DEEPNORM_ALPHA * x1 + ffn, ln2_g[i], ln2_b[i])
        ple = jax.nn.sigmoid(h2 @ w_pg[i] + b_pg[i]) * (p[i] @ w_ple[i])
        x = layer_norm(DEEPNORM_ALPHA * h2 + ple, ln3_g[i], ln3_b[i])
    return x
```

```python
import functools

import jax
import jax.numpy as jnp
from jax import lax
from jax.experimental import pallas as pl
from jax.experimental.pallas import tpu as pltpu

F32 = jnp.float32
BF16 = jnp.bfloat16
U32 = jnp.uint32

POOL_WINDOWS = (2, 4, 8, 16)
N_POOL_GROUPS = 4
CONV_K = 3
N_EXPERTS = 32
TOP_K = 4
SWIGLU_LIMIT = 7.0
SWIGLU_ALPHA = 1.702
LN_EPS = 1e-5

LANES = 128
VMEM_LIMIT_BYTES = 56 * 1024 * 1024

HALO = 16
PROJ_BM = 1024
PROJ_BN = 1024
MIX_TM = 256
DISP_TB = 512
ROW_BLK = 128
ITEM_BLKS = 10
FFN_TF = 256


def _layer_norm(v, g, b):
    mu = jnp.mean(v, axis=-1, keepdims=True)
    c = v - mu
    var = jnp.mean(c * c, axis=-1, keepdims=True)
    return c * lax.rsqrt(var + LN_EPS) * g + b


def _dot(a, b):
    return jnp.dot(a, b, preferred_element_type=F32)


def _pack_pairs(v):
    half = v.shape[1] // 2
    bits = pltpu.bitcast(v.astype(BF16).astype(F32), U32)
    return (bits[:, 0:half] & U32(0xFFFF0000)) | (bits[:, half:] >> 16)


def _store_slabs(ref, row0, packed):
    rows, width = packed.shape
    s = width // LANES
    for c in range(s):
        ref[pl.ds(row0 * s + c, rows, stride=s), :] = packed[:, c * LANES:(c + 1) * LANES]


def _load_slab_column(ref, row0, rows, s, c):
    p = ref[pl.ds(row0 * s + c, rows, stride=s), :]
    return pltpu.bitcast(p & U32(0xFFFF0000), F32), pltpu.bitcast(p << 16, F32)


def _proj_kernel(x_ref, w_ref, o_ref):
    o_ref[...] = _dot(x_ref[...], w_ref[...]).astype(o_ref.dtype)


def _proj(xb, wb):
    n, d = xb.shape
    n_in = wb.shape[1]
    return pl.pallas_call(
        _proj_kernel,
        grid=(n // PROJ_BM, n_in // PROJ_BN),
        in_specs=[pl.BlockSpec((PROJ_BM, d), lambda i, j: (i, 0)),
                  pl.BlockSpec((d, PROJ_BN), lambda i, j: (0, j))],
        out_specs=pl.BlockSpec((PROJ_BM, PROJ_BN), lambda i, j: (i, j)),
        out_shape=jax.ShapeDtypeStruct((n, n_in), BF16),
        compiler_params=pltpu.CompilerParams(
            dimension_semantics=("arbitrary", "arbitrary"),
            vmem_limit_bytes=VMEM_LIMIT_BYTES),
        name="proj",
    )(xb, wb)


def _mixer_kernel(proj_ref, halo_ref, x_ref, wpool_ref, pscale_ref, convw_ref,
                  wbra_ref, wbrb_ref, wo_ref, g1_ref, b1_ref, wrt_ref, br_ref,
                  x1_ref, x1p_ref, tope_ref, gate_ref, rank_ref, cnt_ref,
                  ext_ref, carry_ref, *, seq, alpha, pw, cw, d):
    tm = x_ref.shape[0]
    i = pl.program_id(0)
    blocks_per_seq = seq // tm
    j = i % blocks_per_seq
    keep_halo = (j > 0).astype(F32)
    row = lax.broadcasted_iota(jnp.int32, (tm, 1), 0)
    pos1 = (j * tm + row + 1).astype(F32)

    @pl.when(i == 0)
    def _():
        carry_ref[...] = jnp.zeros_like(carry_ref)

    u = proj_ref[:, 0:pw].astype(F32)
    ext_ref[0:HALO, :] = halo_ref[:, 0:pw].astype(F32) * keep_halo
    ext_ref[HALO:HALO + tm, :] = u
    gw = pw // N_POOL_GROUPS
    a_parts = []
    for g, w in enumerate(POOL_WINDOWS):
        cols = slice(g * gw, (g + 1) * gw)
        s = u[:, cols]
        for sh in range(1, w):
            s = s + ext_ref[HALO - sh:HALO - sh + tm, cols]
        cnt = jnp.minimum(pos1, float(w))
        pooled = s / cnt - u[:, cols]
        a_g = _dot(pooled.astype(BF16), wpool_ref[g]) * pscale_ref[:, cols]
        a_parts.append(a_g.astype(BF16))
    a = jnp.concatenate(a_parts, axis=1)
    br_a = _dot(a, wbra_ref[...])

    o1, o2, o3 = pw + cw, pw + 2 * cw, pw + 3 * cw
    s = proj_ref[:, o1:o2].astype(F32) * proj_ref[:, o2:o3].astype(F32)
    ext_ref[0:HALO, :] = (halo_ref[:, o1:o2].astype(F32) * halo_ref[:, o2:o3].astype(F32)) * keep_halo
    ext_ref[HALO:HALO + tm, :] = s
    conv = s * convw_ref[CONV_K - 1:CONV_K, :]
    for k in range(CONV_K - 1):
        sh = CONV_K - 1 - k
        conv = conv + ext_ref[HALO - sh:HALO - sh + tm, :] * convw_ref[k:k + 1, :]
    b = proj_ref[:, pw:o1].astype(F32) * conv
    br_b = _dot(b.astype(BF16), wbrb_ref[...])

    g_a = proj_ref[:, o3:o3 + d].astype(F32)
    g_b = proj_ref[:, o3 + d:o3 + 2 * d].astype(F32)
    m = jax.nn.sigmoid(g_a) * br_a + jax.nn.sigmoid(g_b) * br_b
    mix = _dot(m.astype(BF16), wo_ref[...])
    x1 = _layer_norm(alpha * x_ref[...] + mix, g1_ref[...], b1_ref[...])
    x1_ref[...] = x1
    _store_slabs(x1p_ref, 0, _pack_pairs(x1))

    logits = lax.dot_general(wrt_ref[...], x1.astype(BF16), (((1,), (1,)), ((), ())),
                             preferred_element_type=F32) + br_ref[...]
    e_iota = lax.broadcasted_iota(jnp.int32, logits.shape, 0)
    vals, idxs = [], []
    l = logits
    for _ in range(TOP_K):
        mx = jnp.max(l, axis=0, keepdims=True)
        ix = jnp.min(jnp.where(l == mx, e_iota, N_EXPERTS), axis=0, keepdims=True)
        vals.append(mx)
        idxs.append(ix)
        l = jnp.where(e_iota == ix, -jnp.inf, l)
    exps = [jnp.exp(v - vals[0]) for v in vals]
    denom = exps[0] + exps[1] + exps[2] + exps[3]
    onehot = jnp.zeros(logits.shape, F32)
    for k in range(TOP_K):
        tope_ref[k:k + 1, :] = idxs[k]
        gate_ref[k:k + 1, :] = exps[k] / denom
        onehot = onehot + (e_iota == idxs[k]).astype(F32)

    r_i = lax.broadcasted_iota(jnp.int32, (tm, tm), 0)
    c_i = lax.broadcasted_iota(jnp.int32, (tm, tm), 1)
    before = (r_i < c_i).astype(BF16)
    seen = _dot(onehot.astype(BF16), before) + carry_ref[:, 0:1]
    for k in range(TOP_K):
        rk = jnp.sum(jnp.where(e_iota == idxs[k], seen, 0.0), axis=0, keepdims=True)
        rank_ref[k:k + 1, :] = rk.astype(jnp.int32)
    carry_ref[...] = carry_ref[...] + jnp.sum(onehot, axis=1, keepdims=True)
    cnt_ref[...] = carry_ref[...].astype(jnp.int32)


def _mixer(proj, x2, wpool, pscale, convw, wbra, wbrb, wo, g1, b1, wrt, br, *, seq, alpha):
    n, d = x2.shape
    n_in = proj.shape[1]
    pw = wbra.shape[0]
    cw = wbrb.shape[0]
    tm = MIX_TM
    slab = d // 2 // LANES
    hb = tm // HALO
    const = lambda i: (0, 0)
    kern = functools.partial(_mixer_kernel, seq=seq, alpha=alpha, pw=pw, cw=cw, d=d)
    return pl.pallas_call(
        kern,
        grid=(n // tm,),
        in_specs=[
            pl.BlockSpec((tm, n_in), lambda i: (i, 0)),
            pl.BlockSpec((HALO, n_in // 2), lambda i: (jnp.maximum(i * hb - 1, 0), 0)),
            pl.BlockSpec((tm, d), lambda i: (i, 0)),
            pl.BlockSpec(wpool.shape, lambda i: (0, 0, 0)),
            pl.BlockSpec(pscale.shape, const),
            pl.BlockSpec(convw.shape, const),
            pl.BlockSpec(wbra.shape, const),
            pl.BlockSpec(wbrb.shape, const),
            pl.BlockSpec(wo.shape, const),
            pl.BlockSpec(g1.shape, const),
            pl.BlockSpec(b1.shape, const),
            pl.BlockSpec(wrt.shape, const),
            pl.BlockSpec(br.shape, const),
        ],
        out_specs=[
            pl.BlockSpec((tm, d), lambda i: (i, 0)),
            pl.BlockSpec((tm * slab, LANES), lambda i: (i, 0)),
            pl.BlockSpec((TOP_K, tm), lambda i: (0, i)),
            pl.BlockSpec((TOP_K, tm), lambda i: (0, i)),
            pl.BlockSpec((TOP_K, tm), lambda i: (0, i)),
            pl.BlockSpec((N_EXPERTS, LANES), const),
        ],
        out_shape=[
            jax.ShapeDtypeStruct((n, d), F32),
            jax.ShapeDtypeStruct((n * slab, LANES), U32),
            jax.ShapeDtypeStruct((TOP_K, n), jnp.int32),
            jax.ShapeDtypeStruct((TOP_K, n), F32),
            jax.ShapeDtypeStruct((TOP_K, n), jnp.int32),
            jax.ShapeDtypeStruct((N_EXPERTS, LANES), jnp.int32),
        ],
        scratch_shapes=[pltpu.VMEM((HALO + tm, pw), F32),
                        pltpu.VMEM((N_EXPERTS, LANES), F32)],
        compiler_params=pltpu.CompilerParams(
            dimension_semantics=("arbitrary",),
            vmem_limit_bytes=VMEM_LIMIT_BYTES),
        name="mixer",
    )(proj, proj, x2, wpool, pscale, convw, wbra, wbrb, wo, g1, b1, wrt, br)


def _dispatch_kernel(pst_ref, pend_ref, tope_ref, rank_ref, x1p_hbm, xs_hbm, zero_ref, sem, zsem,
                     *, slab):
    tb = tope_ref.shape[1]
    i = pl.program_id(0)
    blk = ROW_BLK * slab

    def zero_copy(b):
        return pltpu.make_async_copy(
            zero_ref, xs_hbm.at[pl.ds(pl.multiple_of(b * blk, blk), blk)], zsem)

    @pl.when(i == 0)
    def _():
        zero_ref[...] = jnp.zeros_like(zero_ref)

        def start(e, c):
            @pl.when(pend_ref[e] > pst_ref[e])
            def _():
                zero_copy(pend_ref[e] // ROW_BLK - 1).start()
            return c

        def wait(e, c):
            @pl.when(pend_ref[e] > pst_ref[e])
            def _():
                zero_copy(pend_ref[e] // ROW_BLK - 1).wait()
            return c

        def start_tail(b, c):
            zero_copy(b).start()
            return c

        def wait_tail(b, c):
            zero_copy(b).wait()
            return c

        used = pend_ref[N_EXPERTS - 1] // ROW_BLK
        total = xs_hbm.shape[0] // blk
        lax.fori_loop(0, N_EXPERTS, start, 0)
        lax.fori_loop(used, total, start_tail, 0)
        lax.fori_loop(0, N_EXPERTS, wait, 0)
        lax.fori_loop(used, total, wait_tail, 0)

    def row_copy(t, k):
        dst = pst_ref[tope_ref[k, t]] + rank_ref[k, t]
        return pltpu.make_async_copy(
            x1p_hbm.at[pl.ds(pl.multiple_of((i * tb + t) * slab, slab), slab)],
            xs_hbm.at[pl.ds(pl.multiple_of(dst * slab, slab), slab)], sem)

    def start(t, c):
        for k in range(TOP_K):
            row_copy(t, k).start()
        return c

    def wait(t, c):
        for k in range(TOP_K):
            row_copy(t, k).wait()
        return c

    lax.fori_loop(0, tb, start, 0)
    lax.fori_loop(0, tb, wait, 0)


def _dispatch(pad_starts, pad_ends, tope, rank, x1p, n, cap):
    slab = x1p.shape[0] // n
    tb = DISP_TB
    smem_blk = pl.BlockSpec((TOP_K, tb), lambda i, *_: (0, i), memory_space=pltpu.SMEM)
    return pl.pallas_call(
        functools.partial(_dispatch_kernel, slab=slab),
        grid_spec=pltpu.PrefetchScalarGridSpec(
            num_scalar_prefetch=2,
            grid=(n // tb,),
            in_specs=[smem_blk, smem_blk, pl.BlockSpec(memory_space=pl.ANY)],
            out_specs=pl.BlockSpec(memory_space=pl.ANY),
            scratch_shapes=[pltpu.VMEM((ROW_BLK * slab, LANES), U32),
                            pltpu.SemaphoreType.DMA(()),
                            pltpu.SemaphoreType.DMA(())],
        ),
        out_shape=jax.ShapeDtypeStruct((cap * slab, LANES), U32),
        compiler_params=pltpu.CompilerParams(
            dimension_semantics=("arbitrary",),
            has_side_effects=True),
        name="dispatch",
    )(pad_starts, pad_ends, tope, rank, x1p)


def _ffn_kernel(ie_ref, ist_ref, inb_ref, used_ref,
                xs_hbm, wg_ref, wu_ref, bg_ref, bu_ref, wd_ref, bd_ref,
                y_hbm,
                slabs, xbuf, yacc, wgb, wub, wdb, sem_in, sem_out, *, slab):
    w = pl.program_id(0)
    f = pl.program_id(1)
    nf = pl.num_programs(1)
    nb = inb_ref[w]
    st = ist_ref[w]
    blk = ROW_BLK * slab
    half = slab * LANES

    def rows(r):
        return pl.ds(pl.multiple_of(r * ROW_BLK, ROW_BLK), ROW_BLK)

    def hbm_blk(ref, r):
        return ref.at[pl.ds(pl.multiple_of((st + r) * blk, blk), blk)]

    def vmem_blk(r):
        return slabs.at[pl.ds(pl.multiple_of(r * blk, blk), blk)]

    @pl.when((w == 0) & (f == 0))
    def _():
        slabs[0:blk, :] = jnp.zeros((blk, LANES), U32)

        def tail_copy(b):
            return pltpu.make_async_copy(
                vmem_blk(0), y_hbm.at[pl.ds(pl.multiple_of(b * blk, blk), blk)], sem_out)

        def start(b, c):
            tail_copy(b).start()
            return c

        def wait(b, c):
            tail_copy(b).wait()
            return c

        total = y_hbm.shape[0] // blk
        lax.fori_loop(used_ref[0], total, start, 0)
        lax.fori_loop(used_ref[0], total, wait, 0)

    @pl.when(nb > 0)
    def _():
        @pl.when(f == 0)
        def _():
            def start(r, c):
                pltpu.make_async_copy(hbm_blk(xs_hbm, r), vmem_blk(r), sem_in).start()
                return c

            def wait(r, c):
                pltpu.make_async_copy(hbm_blk(xs_hbm, r), vmem_blk(r), sem_in).wait()
                return c

            def unpack(r, c):
                for col in range(slab):
                    hi, lo = _load_slab_column(slabs, r * ROW_BLK, ROW_BLK, slab, col)
                    xbuf[rows(r), col * LANES:(col + 1) * LANES] = hi.astype(BF16)
                    xbuf[rows(r), half + col * LANES:half + (col + 1) * LANES] = lo.astype(BF16)
                yacc[rows(r), :] = jnp.broadcast_to(bd_ref[...], (ROW_BLK, yacc.shape[1]))
                return c

            lax.fori_loop(0, nb, start, 0)
            lax.fori_loop(0, nb, wait, 0)
            lax.fori_loop(0, nb, unpack, 0)

        wgb[...] = wg_ref[...].astype(BF16)
        wub[...] = wu_ref[...].astype(BF16)
        wdb[...] = wd_ref[...].astype(BF16)

        def sub(r, c):
            xt = xbuf[rows(r), :]
            gate = jnp.minimum(_dot(xt, wgb[...]) + bg_ref[...], SWIGLU_LIMIT)
            up = jnp.clip(_dot(xt, wub[...]) + bu_ref[...], -SWIGLU_LIMIT, SWIGLU_LIMIT)
            act = (up + 1.0) * (gate * jax.nn.sigmoid(SWIGLU_ALPHA * gate))
            yacc[rows(r), :] += _dot(act.astype(BF16), wdb[...])
            return c

        lax.fori_loop(0, nb, sub, 0)

        @pl.when(f == nf - 1)
        def _():
            def start(r, c):
                _store_slabs(slabs, r * ROW_BLK, _pack_pairs(yacc[rows(r), :]))
                pltpu.make_async_copy(vmem_blk(r), hbm_blk(y_hbm, r), sem_out).start()
                return c

            def wait(r, c):
                pltpu.make_async_copy(vmem_blk(r), hbm_blk(y_hbm, r), sem_out).wait()
                return c

            lax.fori_loop(0, nb, start, 0)
            lax.fori_loop(0, nb, wait, 0)


def _ffn(item_e, item_st, item_nb, used_blocks, xs, w_gu, b_gu, w_down, b_down, cap):
    slab = xs.shape[0] // cap
    n_e, d, f2 = w_gu.shape
    dff = f2 // 2
    tf = FFN_TF
    nf = dff // tf
    n_items = item_e.shape[0]
    ts = ITEM_BLKS * ROW_BLK

    def fsel(w, f, inb):
        return jnp.where(inb[w] > 0, f, nf - 1)

    wg_spec = pl.BlockSpec((None, d, tf), lambda w, f, ie, ist, inb, u: (ie[w], 0, fsel(w, f, inb)))
    wu_spec = pl.BlockSpec((None, d, tf), lambda w, f, ie, ist, inb, u: (ie[w], 0, nf + fsel(w, f, inb)))
    bg_spec = pl.BlockSpec((None, 1, tf), lambda w, f, ie, ist, inb, u: (ie[w], 0, fsel(w, f, inb)))
    bu_spec = pl.BlockSpec((None, 1, tf), lambda w, f, ie, ist, inb, u: (ie[w], 0, nf + fsel(w, f, inb)))
    wd_spec = pl.BlockSpec((None, tf, d), lambda w, f, ie, ist, inb, u: (ie[w], fsel(w, f, inb), 0))
    bd_spec = pl.BlockSpec((None, 1, d), lambda w, f, ie, ist, inb, u: (ie[w], 0, 0))
    return pl.pallas_call(
        functools.partial(_ffn_kernel, slab=slab),
        grid_spec=pltpu.PrefetchScalarGridSpec(
            num_scalar_prefetch=4,
            grid=(n_items, nf),
            in_specs=[pl.BlockSpec(memory_space=pl.ANY),
                      wg_spec, wu_spec, bg_spec, bu_spec, wd_spec, bd_spec],
            out_specs=pl.BlockSpec(memory_space=pl.ANY),
            scratch_shapes=[pltpu.VMEM((ts * slab, LANES), U32),
                            pltpu.VMEM((ts, d), BF16),
                            pltpu.VMEM((ts, d), F32),
                            pltpu.VMEM((d, tf), BF16),
                            pltpu.VMEM((d, tf), BF16),
                            pltpu.VMEM((tf, d), BF16),
                            pltpu.SemaphoreType.DMA(()),
                            pltpu.SemaphoreType.DMA(())],
        ),
        out_shape=jax.ShapeDtypeStruct((cap * slab, LANES), U32),
        compiler_params=pltpu.CompilerParams(
            dimension_semantics=("arbitrary", "arbitrary"),
            vmem_limit_bytes=VMEM_LIMIT_BYTES,
            has_side_effects=True),
        name="ffn",
    )(item_e, item_st, item_nb, used_blocks, xs, w_gu, w_gu, b_gu.reshape(n_e, 1, f2), b_gu.reshape(n_e, 1, f2),
      w_down, b_down.reshape(n_e, 1, d))


def _final_kernel(pst_ref, tope_ref, rank_ref, gate_ref, x1_ref, p_ref, y_hbm,
                  g2_ref, b2_ref, wpg_ref, bpg_ref, wple_ref, g3_ref, b3_ref,
                  o_ref, ybuf, ffn_ref, sem, *, alpha, slab):
    tm = x1_ref.shape[0]
    half = slab * LANES

    def row_copy(t, k):
        src = pst_ref[tope_ref[k, t]] + rank_ref[k, t]
        return pltpu.make_async_copy(
            y_hbm.at[pl.ds(pl.multiple_of(src * slab, slab), slab)],
            ybuf.at[pl.ds(pl.multiple_of((k * tm + t) * slab, slab), slab)], sem)

    def start(t, c):
        for k in range(TOP_K):
            row_copy(t, k).start()
        return c

    def wait(t, c):
        for k in range(TOP_K):
            row_copy(t, k).wait()
        return c

    lax.fori_loop(0, tm, start, 0)
    lax.fori_loop(0, tm, wait, 0)

    for col in range(slab):
        acc_hi = acc_lo = None
        for k in range(TOP_K):
            hi, lo = _load_slab_column(ybuf, k * tm, tm, slab, col)
            g = gate_ref[:, k:k + 1]
            acc_hi = hi * g if acc_hi is None else acc_hi + hi * g
            acc_lo = lo * g if acc_lo is None else acc_lo + lo * g
        ffn_ref[:, col * LANES:(col + 1) * LANES] = acc_hi
        ffn_ref[:, half + col * LANES:half + (col + 1) * LANES] = acc_lo

    h2 = _layer_norm(alpha * x1_ref[...] + ffn_ref[...], g2_ref[...], b2_ref[...])
    pg = jax.nn.sigmoid(_dot(h2.astype(BF16), wpg_ref[...]) + bpg_ref[...])
    pe = _dot(p_ref[...].astype(BF16), wple_ref[...])
    o_ref[...] = _layer_norm(alpha * h2 + pg * pe, g3_ref[...], b3_ref[...])


def _final(pad_starts, tope, rank, gates_t, x1, p2, y, g2, b2, wpg, bpg, wple, g3, b3, cap, *, alpha):
    n, d = x1.shape
    slab = y.shape[0] // cap
    tm = MIX_TM
    smem_blk = pl.BlockSpec((TOP_K, tm), lambda i, *_: (0, i), memory_space=pltpu.SMEM)
    const = lambda i, *_: (0, 0)
    row_blk = lambda width: pl.BlockSpec((tm, width), lambda i, *_: (i, 0))
    return pl.pallas_call(
        functools.partial(_final_kernel, alpha=alpha, slab=slab),
        grid_spec=pltpu.PrefetchScalarGridSpec(
            num_scalar_prefetch=1,
            grid=(n // tm,),
            in_specs=[smem_blk, smem_blk, row_blk(TOP_K), row_blk(d), row_blk(p2.shape[1]),
                      pl.BlockSpec(memory_space=pl.ANY),
                      pl.BlockSpec(g2.shape, const), pl.BlockSpec(b2.shape, const),
                      pl.BlockSpec(wpg.shape, const), pl.BlockSpec(bpg.shape, const),
                      pl.BlockSpec(wple.shape, const),
                      pl.BlockSpec(g3.shape, const), pl.BlockSpec(b3.shape, const)],
            out_specs=row_blk(d),
            scratch_shapes=[pltpu.VMEM((TOP_K * tm * slab, LANES), U32),
                            pltpu.VMEM((tm, d), F32),
                            pltpu.SemaphoreType.DMA(())],
        ),
        out_shape=jax.ShapeDtypeStruct((n, d), F32),
        compiler_params=pltpu.CompilerParams(
            dimension_semantics=("arbitrary",),
            vmem_limit_bytes=VMEM_LIMIT_BYTES),
        name="final",
    )(pad_starts, tope, rank, gates_t, x1, p2, y, g2, b2, wpg, bpg, wple, g3, b3)


def _plan(counts, n_items):
    nblk = (counts + ROW_BLK - 1) // ROW_BLK
    blk_end = jnp.cumsum(nblk)
    blk_start = blk_end - nblk
    items_per = (nblk + ITEM_BLKS - 1) // ITEM_BLKS
    item_end = jnp.cumsum(items_per)
    item_start = item_end - items_per
    total = item_end[-1]
    w = jnp.arange(n_items, dtype=jnp.int32)
    w_eff = jnp.minimum(w, total - 1)
    e = jnp.minimum(jnp.searchsorted(item_end, w_eff, side="right"), N_EXPERTS - 1).astype(jnp.int32)
    j = w_eff - item_start[e]
    st = blk_start[e] + j * ITEM_BLKS
    nb = jnp.where(w < total, jnp.clip(nblk[e] - j * ITEM_BLKS, 0, ITEM_BLKS), 0)
    return ((blk_start * ROW_BLK).astype(jnp.int32), (blk_end * ROW_BLK).astype(jnp.int32),
            e, st.astype(jnp.int32), nb.astype(jnp.int32))


def kernel(x, p, w_in, w_pool, pool_scale, conv_w, w_br_a, w_br_b, w_o, ln1_g, ln1_b,
           w_router, b_router, w_gu, b_gu, w_down, b_down, ln2_g, ln2_b,
           w_pg, b_pg, w_ple, ln3_g, ln3_b):
    depth = w_in.shape[0]
    bsz, seq, d = x.shape
    n = bsz * seq
    alpha = (2.0 * depth) ** 0.25
    nk = n * TOP_K
    n_row_blocks = (nk + N_EXPERTS * (ROW_BLK - 1) + ROW_BLK - 1) // ROW_BLK
    cap = n_row_blocks * ROW_BLK
    n_items = N_EXPERTS + n_row_blocks // ITEM_BLKS
    row = lambda v: v.reshape(1, -1)

    h = x.reshape(n, d)
    for i in range(depth):
        proj = _proj(h.astype(BF16), w_in[i].astype(BF16))
        x1, x1p, tope, gates, rank, cnt = _mixer(
            proj, h, w_pool[i].astype(BF16), row(pool_scale[i]), conv_w[i],
            w_br_a[i].astype(BF16), w_br_b[i].astype(BF16), w_o[i].astype(BF16),
            row(ln1_g[i]), row(ln1_b[i]), w_router[i].T.astype(BF16),
            b_router[i].reshape(N_EXPERTS, 1), seq=seq, alpha=alpha)
        pad_starts, pad_ends, item_e, item_st, item_nb = _plan(cnt[:, 0], n_items)
        xs = _dispatch(pad_starts, pad_ends, tope, rank, x1p, n, cap)
        used_blocks = pad_ends[N_EXPERTS - 1:] // ROW_BLK
        y = _ffn(item_e, item_st, item_nb, used_blocks, xs, w_gu[i], b_gu[i], w_down[i], b_down[i], cap)
        h = _final(pad_starts, tope, rank, gates.T, x1, p[i].reshape(n, -1), y,
                   row(ln2_g[i]), row(ln2_b[i]), w_pg[i].astype(BF16), row(b_pg[i]),
                   w_ple[i].astype(BF16), row(ln3_g[i]), row(ln3_b[i]), cap, alpha=alpha)
    return h.reshape(bsz, seq, d)
```

```python
import functools

import jax
import jax.numpy as jnp
from jax import lax
from jax.experimental import pallas as pl
from jax.experimental.pallas import tpu as pltpu

F32 = jnp.float32
BF16 = jnp.bfloat16
U32 = jnp.uint32

POOL_WINDOWS = (2, 4, 8, 16)
N_POOL_GROUPS = 4
CONV_K = 3
N_EXPERTS = 32
TOP_K = 4
SWIGLU_LIMIT = 7.0
SWIGLU_ALPHA = 1.702
LN_EPS = 1e-5

LANES = 128
VMEM_LIMIT_BYTES = 56 * 1024 * 1024

HALO = 16
PROJ_BM = 1024
PROJ_BN = 1024
MIX_TM = 256
DISP_TB = 512
ROW_BLK = 128
ITEM_BLKS = 10
FFN_TF = 512


def _layer_norm(v, g, b):
    mu = jnp.mean(v, axis=-1, keepdims=True)
    c = v - mu
    var = jnp.mean(c * c, axis=-1, keepdims=True)
    return c * lax.rsqrt(var + LN_EPS) * g + b


def _dot(a, b):
    return jnp.dot(a, b, preferred_element_type=F32)


def _pack_pairs(v):
    half = v.shape[1] // 2
    bits = pltpu.bitcast(v.astype(BF16).astype(F32), U32)
    return (bits[:, 0:half] & U32(0xFFFF0000)) | (bits[:, half:] >> 16)


def _store_slabs(ref, row0, packed):
    rows, width = packed.shape
    s = width // LANES
    for c in range(s):
        ref[pl.ds(row0 * s + c, rows, stride=s), :] = packed[:, c * LANES:(c + 1) * LANES]


def _load_slab_column(ref, row0, rows, s, c):
    p = ref[pl.ds(row0 * s + c, rows, stride=s), :]
    return pltpu.bitcast(p & U32(0xFFFF0000), F32), pltpu.bitcast(p << 16, F32)


def _proj_kernel(x_ref, w_ref, o_ref):
    o_ref[...] = _dot(x_ref[...], w_ref[...]).astype(o_ref.dtype)


def _proj(xb, wb):
    n, d = xb.shape
    n_in = wb.shape[1]
    return pl.pallas_call(
        _proj_kernel,
        grid=(n // PROJ_BM, n_in // PROJ_BN),
        in_specs=[pl.BlockSpec((PROJ_BM, d), lambda i, j: (i, 0)),
                  pl.BlockSpec((d, PROJ_BN), lambda i, j: (0, j))],
        out_specs=pl.BlockSpec((PROJ_BM, PROJ_BN), lambda i, j: (i, j)),
        out_shape=jax.ShapeDtypeStruct((n, n_in), BF16),
        compiler_params=pltpu.CompilerParams(
            dimension_semantics=("arbitrary", "arbitrary"),
            vmem_limit_bytes=VMEM_LIMIT_BYTES),
        name="proj",
    )(xb, wb)


def _mixer_kernel(proj_ref, halo_ref, x_ref, wpool_ref, pscale_ref, convw_ref,
                  wbra_ref, wbrb_ref, wo_ref, g1_ref, b1_ref, wrt_ref, br_ref,
                  x1_ref, x1p_ref, tope_ref, gate_ref, rank_ref, cnt_ref,
                  ext_ref, carry_ref, *, seq, alpha, pw, cw, d):
    tm = x_ref.shape[0]
    i = pl.program_id(0)
    blocks_per_seq = seq // tm
    j = i % blocks_per_seq
    keep_halo = (j > 0).astype(F32)
    row = lax.broadcasted_iota(jnp.int32, (tm, 1), 0)
    pos1 = (j * tm + row + 1).astype(F32)

    @pl.when(i == 0)
    def _():
        carry_ref[...] = jnp.zeros_like(carry_ref)

    u = proj_ref[:, 0:pw].astype(F32)
    ext_ref[0:HALO, :] = halo_ref[:, 0:pw].astype(F32) * keep_halo
    ext_ref[HALO:HALO + tm, :] = u
    gw = pw // N_POOL_GROUPS
    a_parts = []
    for g, w in enumerate(POOL_WINDOWS):
        cols = slice(g * gw, (g + 1) * gw)
        s = u[:, cols]
        for sh in range(1, w):
            s = s + ext_ref[HALO - sh:HALO - sh + tm, cols]
        cnt = jnp.minimum(pos1, float(w))
        pooled = s / cnt - u[:, cols]
        a_g = _dot(pooled.astype(BF16), wpool_ref[g]) * pscale_ref[:, cols]
        a_parts.append(a_g.astype(BF16))
    a = jnp.concatenate(a_parts, axis=1)
    br_a = _dot(a, wbra_ref[...])

    o1, o2, o3 = pw + cw, pw + 2 * cw, pw + 3 * cw
    s = proj_ref[:, o1:o2].astype(F32) * proj_ref[:, o2:o3].astype(F32)
    ext_ref[0:HALO, :] = (halo_ref[:, o1:o2].astype(F32) * halo_ref[:, o2:o3].astype(F32)) * keep_halo
    ext_ref[HALO:HALO + tm, :] = s
    conv = s * convw_ref[CONV_K - 1:CONV_K, :]
    for k in range(CONV_K - 1):
        sh = CONV_K - 1 - k
        conv = conv + ext_ref[HALO - sh:HALO - sh + tm, :] * convw_ref[k:k + 1, :]
    b = proj_ref[:, pw:o1].astype(F32) * conv
    br_b = _dot(b.astype(BF16), wbrb_ref[...])

    g_a = proj_ref[:, o3:o3 + d].astype(F32)
    g_b = proj_ref[:, o3 + d:o3 + 2 * d].astype(F32)
    m = jax.nn.sigmoid(g_a) * br_a + jax.nn.sigmoid(g_b) * br_b
    mix = _dot(m.astype(BF16), wo_ref[...])
    x1 = _layer_norm(alpha * x_ref[...] + mix, g1_ref[...], b1_ref[...])
    x1_ref[...] = x1
    _store_slabs(x1p_ref, 0, _pack_pairs(x1))

    logits = lax.dot_general(wrt_ref[...], x1.astype(BF16), (((1,), (1,)), ((), ())),
                             preferred_element_type=F32) + br_ref[...]
    e_iota = lax.broadcasted_iota(jnp.int32, logits.shape, 0)
    vals, idxs = [], []
    l = logits
    for _ in range(TOP_K):
        mx = jnp.max(l, axis=0, keepdims=True)
        ix = jnp.min(jnp.where(l == mx, e_iota, N_EXPERTS), axis=0, keepdims=True)
        vals.append(mx)
        idxs.append(ix)
        l = jnp.where(e_iota == ix, -jnp.inf, l)
    exps = [jnp.exp(v - vals[0]) for v in vals]
    denom = exps[0] + exps[1] + exps[2] + exps[3]
    onehot = jnp.zeros(logits.shape, F32)
    for k in range(TOP_K):
        tope_ref[k:k + 1, :] = idxs[k]
        gate_ref[k:k + 1, :] = exps[k] / denom
        onehot = onehot + (e_iota == idxs[k]).astype(F32)

    r_i = lax.broadcasted_iota(jnp.int32, (tm, tm), 0)
    c_i = lax.broadcasted_iota(jnp.int32, (tm, tm), 1)
    before = (r_i < c_i).astype(BF16)
    seen = _dot(onehot.astype(BF16), before) + carry_ref[:, 0:1]
    for k in range(TOP_K):
        rk = jnp.sum(jnp.where(e_iota == idxs[k], seen, 0.0), axis=0, keepdims=True)
        rank_ref[k:k + 1, :] = rk.astype(jnp.int32)
    carry_ref[...] = carry_ref[...] + jnp.sum(onehot, axis=1, keepdims=True)
    cnt_ref[...] = carry_ref[...].astype(jnp.int32)


def _mixer(proj, x2, wpool, pscale, convw, wbra, wbrb, wo, g1, b1, wrt, br, *, seq, alpha):
    n, d = x2.shape
    n_in = proj.shape[1]
    pw = wbra.shape[0]
    cw = wbrb.shape[0]
    tm = MIX_TM
    slab = d // 2 // LANES
    hb = tm // HALO
    const = lambda i: (0, 0)
    kern = functools.partial(_mixer_kernel, seq=seq, alpha=alpha, pw=pw, cw=cw, d=d)
    return pl.pallas_call(
        kern,
        grid=(n // tm,),
        in_specs=[
            pl.BlockSpec((tm, n_in), lambda i: (i, 0)),
            pl.BlockSpec((HALO, n_in // 2), lambda i: (jnp.maximum(i * hb - 1, 0), 0)),
            pl.BlockSpec((tm, d), lambda i: (i, 0)),
            pl.BlockSpec(wpool.shape, lambda i: (0, 0, 0)),
            pl.BlockSpec(pscale.shape, const),
            pl.BlockSpec(convw.shape, const),
            pl.BlockSpec(wbra.shape, const),
            pl.BlockSpec(wbrb.shape, const),
            pl.BlockSpec(wo.shape, const),
            pl.BlockSpec(g1.shape, const),
            pl.BlockSpec(b1.shape, const),
            pl.BlockSpec(wrt.shape, const),
            pl.BlockSpec(br.shape, const),
        ],
        out_specs=[
            pl.BlockSpec((tm, d), lambda i: (i, 0)),
            pl.BlockSpec((tm * slab, LANES), lambda i: (i, 0)),
            pl.BlockSpec((TOP_K, tm), lambda i: (0, i)),
            pl.BlockSpec((TOP_K, tm), lambda i: (0, i)),
            pl.BlockSpec((TOP_K, tm), lambda i: (0, i)),
            pl.BlockSpec((N_EXPERTS, LANES), const),
        ],
        out_shape=[
            jax.ShapeDtypeStruct((n, d), F32),
            jax.ShapeDtypeStruct((n * slab, LANES), U32),
            jax.ShapeDtypeStruct((TOP_K, n), jnp.int32),
            jax.ShapeDtypeStruct((TOP_K, n), F32),
            jax.ShapeDtypeStruct((TOP_K, n), jnp.int32),
            jax.ShapeDtypeStruct((N_EXPERTS, LANES), jnp.int32),
        ],
        scratch_shapes=[pltpu.VMEM((HALO + tm, pw), F32),
                        pltpu.VMEM((N_EXPERTS, LANES), F32)],
        compiler_params=pltpu.CompilerParams(
            dimension_semantics=("arbitrary",),
            vmem_limit_bytes=VMEM_LIMIT_BYTES),
        name="mixer",
    )(proj, proj, x2, wpool, pscale, convw, wbra, wbrb, wo, g1, b1, wrt, br)


def _dispatch_kernel(pst_ref, pend_ref, tope_ref, rank_ref, x1p_ref, xs_hbm, zero_ref, sem, zsem,
                     *, slab):
    tb = tope_ref.shape[1]
    i = pl.program_id(0)
    blk = ROW_BLK * slab

    def zero_copy(b):
        return pltpu.make_async_copy(
            zero_ref, xs_hbm.at[pl.ds(pl.multiple_of(b * blk, blk), blk)], zsem)

    @pl.when(i == 0)
    def _():
        zero_ref[...] = jnp.zeros_like(zero_ref)

        def start(e, c):
            @pl.when(pend_ref[e] > pst_ref[e])
            def _():
                zero_copy(pend_ref[e] // ROW_BLK - 1).start()
            return c

        def wait(e, c):
            @pl.when(pend_ref[e] > pst_ref[e])
            def _():
                zero_copy(pend_ref[e] // ROW_BLK - 1).wait()
            return c

        def start_tail(b, c):
            zero_copy(b).start()
            return c

        def wait_tail(b, c):
            zero_copy(b).wait()
            return c

        used = pend_ref[N_EXPERTS - 1] // ROW_BLK
        total = xs_hbm.shape[0] // blk
        lax.fori_loop(0, N_EXPERTS, start, 0)
        lax.fori_loop(used, total, start_tail, 0)
        lax.fori_loop(0, N_EXPERTS, wait, 0)
        lax.fori_loop(used, total, wait_tail, 0)

    def start(t, c):
        src = x1p_ref.at[pl.ds(pl.multiple_of(t * slab, slab), slab)]
        for k in range(TOP_K):
            dst = pst_ref[tope_ref[k, t]] + rank_ref[k, t]
            pltpu.make_async_copy(
                src, xs_hbm.at[pl.ds(pl.multiple_of(dst * slab, slab), slab)], sem).start()
        return c

    lax.fori_loop(0, tb, start, 0)
    for k in range(TOP_K):
        pltpu.make_async_copy(x1p_ref, xs_hbm.at[pl.ds(0, tb * slab)], sem).wait()


def _dispatch(pad_starts, pad_ends, tope, rank, x1p, n, cap):
    slab = x1p.shape[0] // n
    tb = DISP_TB
    smem_blk = pl.BlockSpec((TOP_K, tb), lambda i, *_: (0, i), memory_space=pltpu.SMEM)
    return pl.pallas_call(
        functools.partial(_dispatch_kernel, slab=slab),
        grid_spec=pltpu.PrefetchScalarGridSpec(
            num_scalar_prefetch=2,
            grid=(n // tb,),
            in_specs=[smem_blk, smem_blk,
                      pl.BlockSpec((tb * slab, LANES), lambda i, *_: (i, 0))],
            out_specs=pl.BlockSpec(memory_space=pl.ANY),
            scratch_shapes=[pltpu.VMEM((ROW_BLK * slab, LANES), U32),
                            pltpu.SemaphoreType.DMA(()),
                            pltpu.SemaphoreType.DMA(())],
        ),
        out_shape=jax.ShapeDtypeStruct((cap * slab, LANES), U32),
        compiler_params=pltpu.CompilerParams(
            dimension_semantics=("arbitrary",),
            has_side_effects=True),
        name="dispatch",
    )(pad_starts, pad_ends, tope, rank, x1p)


def _ffn_kernel(ie_ref, ist_ref, inb_ref, used_ref,
                xs_hbm, wg_ref, wu_ref, bg_ref, bu_ref, wd_ref, bd_ref,
                y_hbm,
                slabs, xbuf, yacc, sem_in, sem_out, *, slab):
    w = pl.program_id(0)
    f = pl.program_id(1)
    nf = pl.num_programs(1)
    nb = inb_ref[w]
    st = ist_ref[w]
    blk = ROW_BLK * slab
    half = slab * LANES

    def rows(r):
        return pl.ds(pl.multiple_of(r * ROW_BLK, ROW_BLK), ROW_BLK)

    def hbm_blk(ref, r):
        return ref.at[pl.ds(pl.multiple_of((st + r) * blk, blk), blk)]

    def vmem_blk(r):
        return slabs.at[pl.ds(pl.multiple_of(r * blk, blk), blk)]

    @pl.when((w == 0) & (f == 0))
    def _():
        slabs[0:blk, :] = jnp.zeros((blk, LANES), U32)

        def tail_copy(b):
            return pltpu.make_async_copy(
                vmem_blk(0), y_hbm.at[pl.ds(pl.multiple_of(b * blk, blk), blk)], sem_out)

        def start(b, c):
            tail_copy(b).start()
            return c

        def wait(b, c):
            tail_copy(b).wait()
            return c

        total = y_hbm.shape[0] // blk
        lax.fori_loop(used_ref[0], total, start, 0)
        lax.fori_loop(used_ref[0], total, wait, 0)

    @pl.when(nb > 0)
    def _():
        @pl.when(f == 0)
        def _():
            def start(r, c):
                pltpu.make_async_copy(hbm_blk(xs_hbm, r), vmem_blk(r), sem_in).start()
                return c

            def wait(r, c):
                pltpu.make_async_copy(hbm_blk(xs_hbm, r), vmem_blk(r), sem_in).wait()
                return c

            def unpack(r, c):
                for col in range(slab):
                    hi, lo = _load_slab_column(slabs, r * ROW_BLK, ROW_BLK, slab, col)
                    xbuf[rows(r), col * LANES:(col + 1) * LANES] = hi.astype(BF16)
                    xbuf[rows(r), half + col * LANES:half + (col + 1) * LANES] = lo.astype(BF16)
                yacc[rows(r), :] = jnp.broadcast_to(bd_ref[...], (ROW_BLK, yacc.shape[1]))
                return c

            lax.fori_loop(0, nb, start, 0)
            lax.fori_loop(0, nb, wait, 0)
            lax.fori_loop(0, nb, unpack, 0)

        def sub(row0, n_rows):
            rs = pl.ds(pl.multiple_of(row0, ROW_BLK), n_rows)
            xt = xbuf[rs, :]
            gate = jnp.minimum(_dot(xt, wg_ref[...]) + bg_ref[...], SWIGLU_LIMIT)
            up = jnp.clip(_dot(xt, wu_ref[...]) + bu_ref[...], -SWIGLU_LIMIT, SWIGLU_LIMIT)
            act = (up + 1.0) * (gate * jax.nn.sigmoid(SWIGLU_ALPHA * gate))
            yacc[rs, :] += _dot(act.astype(BF16), wd_ref[...])

        def pair(q, c):
            sub(q * (2 * ROW_BLK), 2 * ROW_BLK)
            return c

        lax.fori_loop(0, nb // 2, pair, 0)

        @pl.when(nb % 2 == 1)
        def _():
            sub((nb - 1) * ROW_BLK, ROW_BLK)

        @pl.when(f == nf - 1)
        def _():
            def start(r, c):
                _store_slabs(slabs, r * ROW_BLK, _pack_pairs(yacc[rows(r), :]))
                pltpu.make_async_copy(vmem_blk(r), hbm_blk(y_hbm, r), sem_out).start()
                return c

            def wait(r, c):
                pltpu.make_async_copy(vmem_blk(r), hbm_blk(y_hbm, r), sem_out).wait()
                return c

            lax.fori_loop(0, nb, start, 0)
            lax.fori_loop(0, nb, wait, 0)


def _ffn(item_e, item_st, item_nb, used_blocks, xs, w_gu, b_gu, w_down, b_down, cap):
    slab = xs.shape[0] // cap
    n_e, d, f2 = w_gu.shape
    dff = f2 // 2
    tf = FFN_TF
    nf = dff // tf
    n_items = item_e.shape[0]
    ts = ITEM_BLKS * ROW_BLK

    def fsel(w, f, inb):
        return jnp.where(inb[w] > 0, f, nf - 1)

    wg_spec = pl.BlockSpec((None, d, tf), lambda w, f, ie, ist, inb, u: (ie[w], 0, fsel(w, f, inb)))
    wu_spec = pl.BlockSpec((None, d, tf), lambda w, f, ie, ist, inb, u: (ie[w], 0, nf + fsel(w, f, inb)))
    bg_spec = pl.BlockSpec((None, 1, tf), lambda w, f, ie, ist, inb, u: (ie[w], 0, fsel(w, f, inb)))
    bu_spec = pl.BlockSpec((None, 1, tf), lambda w, f, ie, ist, inb, u: (ie[w], 0, nf + fsel(w, f, inb)))
    wd_spec = pl.BlockSpec((None, tf, d), lambda w, f, ie, ist, inb, u: (ie[w], fsel(w, f, inb), 0))
    bd_spec = pl.BlockSpec((None, 1, d), lambda w, f, ie, ist, inb, u: (ie[w], 0, 0))
    return pl.pallas_call(
        functools.partial(_ffn_kernel, slab=slab),
        grid_spec=pltpu.PrefetchScalarGridSpec(
            num_scalar_prefetch=4,
            grid=(n_items, nf),
            in_specs=[pl.BlockSpec(memory_space=pl.ANY),
                      wg_spec, wu_spec, bg_spec, bu_spec, wd_spec, bd_spec],
            out_specs=pl.BlockSpec(memory_space=pl.ANY),
            scratch_shapes=[pltpu.VMEM((ts * slab, LANES), U32),
                            pltpu.VMEM((ts, d), BF16),
                            pltpu.VMEM((ts, d), F32),
                            pltpu.SemaphoreType.DMA(()),
                            pltpu.SemaphoreType.DMA(())],
        ),
        out_shape=jax.ShapeDtypeStruct((cap * slab, LANES), U32),
        compiler_params=pltpu.CompilerParams(
            dimension_semantics=("arbitrary", "arbitrary"),
            vmem_limit_bytes=VMEM_LIMIT_BYTES,
            has_side_effects=True),
        name="ffn",
    )(item_e, item_st, item_nb, used_blocks, xs, w_gu, w_gu, b_gu.reshape(n_e, 1, f2), b_gu.reshape(n_e, 1, f2),
      w_down, b_down.reshape(n_e, 1, d))


def _final_kernel(pst_ref, tope_ref, rank_ref, gate_ref, x1_ref, p_ref, y_hbm,
                  g2_ref, b2_ref, wpg_ref, bpg_ref, wple_ref, g3_ref, b3_ref,
                  o_ref, ybuf, ffn_ref, sem, *, alpha, slab):
    tm = x1_ref.shape[0]
    half = slab * LANES

    def start(t, c):
        for k in range(TOP_K):
            src = pst_ref[tope_ref[k, t]] + rank_ref[k, t]
            pltpu.make_async_copy(
                y_hbm.at[pl.ds(pl.multiple_of(src * slab, slab), slab)],
                ybuf.at[pl.ds(pl.multiple_of((k * tm + t) * slab, slab), slab)], sem).start()
        return c

    lax.fori_loop(0, tm, start, 0)
    pltpu.make_async_copy(y_hbm.at[pl.ds(0, ybuf.shape[0])], ybuf, sem).wait()

    for col in range(slab):
        acc_hi = acc_lo = None
        for k in range(TOP_K):
            hi, lo = _load_slab_column(ybuf, k * tm, tm, slab, col)
            g = gate_ref[:, k:k + 1]
            acc_hi = hi * g if acc_hi is None else acc_hi + hi * g
            acc_lo = lo * g if acc_lo is None else acc_lo + lo * g
        ffn_ref[:, col * LANES:(col + 1) * LANES] = acc_hi
        ffn_ref[:, half + col * LANES:half + (col + 1) * LANES] = acc_lo

    h2 = _layer_norm(alpha * x1_ref[...] + ffn_ref[...], g2_ref[...], b2_ref[...])
    pg = jax.nn.sigmoid(_dot(h2.astype(BF16), wpg_ref[...]) + bpg_ref[...])
    pe = _dot(p_ref[...].astype(BF16), wple_ref[...])
    o_ref[...] = _layer_norm(alpha * h2 + pg * pe, g3_ref[...], b3_ref[...])


def _final(pad_starts, tope, rank, gates_t, x1, p2, y, g2, b2, wpg, bpg, wple, g3, b3, cap, *, alpha):
    n, d = x1.shape
    slab = y.shape[0] // cap
    tm = MIX_TM
    smem_blk = pl.BlockSpec((TOP_K, tm), lambda i, *_: (0, i), memory_space=pltpu.SMEM)
    const = lambda i, *_: (0, 0)
    row_blk = lambda width: pl.BlockSpec((tm, width), lambda i, *_: (i, 0))
    return pl.pallas_call(
        functools.partial(_final_kernel, alpha=alpha, slab=slab),
        grid_spec=pltpu.PrefetchScalarGridSpec(
            num_scalar_prefetch=1,
            grid=(n // tm,),
            in_specs=[smem_blk, smem_blk, row_blk(TOP_K), row_blk(d), row_blk(p2.shape[1]),
                      pl.BlockSpec(memory_space=pl.ANY),
                      pl.BlockSpec(g2.shape, const), pl.BlockSpec(b2.shape, const),
                      pl.BlockSpec(wpg.shape, const), pl.BlockSpec(bpg.shape, const),
                      pl.BlockSpec(wple.shape, const),
                      pl.BlockSpec(g3.shape, const), pl.BlockSpec(b3.shape, const)],
            out_specs=row_blk(d),
            scratch_shapes=[pltpu.VMEM((TOP_K * tm * slab, LANES), U32),
                            pltpu.VMEM((tm, d), F32),
                            pltpu.SemaphoreType.DMA(())],
        ),
        out_shape=jax.ShapeDtypeStruct((n, d), F32),
        compiler_params=pltpu.CompilerParams(
            dimension_semantics=("arbitrary",),
            vmem_limit_bytes=VMEM_LIMIT_BYTES),
        name="final",
    )(pad_starts, tope, rank, gates_t, x1, p2, y, g2, b2, wpg, bpg, wple, g3, b3)


def _plan(counts, n_items):
    nblk = (counts + ROW_BLK - 1) // ROW_BLK
    blk_end = jnp.cumsum(nblk)
    blk_start = blk_end - nblk
    items_per = (nblk + ITEM_BLKS - 1) // ITEM_BLKS
    item_end = jnp.cumsum(items_per)
    item_start = item_end - items_per
    total = item_end[-1]
    w = jnp.arange(n_items, dtype=jnp.int32)
    w_eff = jnp.minimum(w, total - 1)
    e = jnp.minimum(jnp.searchsorted(item_end, w_eff, side="right"), N_EXPERTS - 1).astype(jnp.int32)
    j = w_eff - item_start[e]
    st = blk_start[e] + j * ITEM_BLKS
    nb = jnp.where(w < total, jnp.clip(nblk[e] - j * ITEM_BLKS, 0, ITEM_BLKS), 0)
    return ((blk_start * ROW_BLK).astype(jnp.int32), (blk_end * ROW_BLK).astype(jnp.int32),
            e, st.astype(jnp.int32), nb.astype(jnp.int32))


def kernel(x, p, w_in, w_pool, pool_scale, conv_w, w_br_a, w_br_b, w_o, ln1_g, ln1_b,
           w_router, b_router, w_gu, b_gu, w_down, b_down, ln2_g, ln2_b,
           w_pg, b_pg, w_ple, ln3_g, ln3_b):
    depth = w_in.shape[0]
    bsz, seq, d = x.shape
    n = bsz * seq
    alpha = (2.0 * depth) ** 0.25
    nk = n * TOP_K
    n_row_blocks = (nk + N_EXPERTS * (ROW_BLK - 1) + ROW_BLK - 1) // ROW_BLK
    cap = n_row_blocks * ROW_BLK
    n_items = N_EXPERTS + n_row_blocks // ITEM_BLKS
    row = lambda v: v.reshape(1, -1)

    h = x.reshape(n, d)
    for i in range(depth):
        proj = _proj(h.astype(BF16), w_in[i].astype(BF16))
        x1, x1p, tope, gates, rank, cnt = _mixer(
            proj, h, w_pool[i].astype(BF16), row(pool_scale[i]), conv_w[i],
            w_br_a[i].astype(BF16), w_br_b[i].astype(BF16), w_o[i].astype(BF16),
            row(ln1_g[i]), row(ln1_b[i]), w_router[i].T.astype(BF16),
            b_router[i].reshape(N_EXPERTS, 1), seq=seq, alpha=alpha)
        pad_starts, pad_ends, item_e, item_st, item_nb = _plan(cnt[:, 0], n_items)
        xs = _dispatch(pad_starts, pad_ends, tope, rank, x1p, n, cap)
        used_blocks = pad_ends[N_EXPERTS - 1:] // ROW_BLK
        y = _ffn(item_e, item_st, item_nb, used_blocks, xs, w_gu[i], b_gu[i], w_down[i], b_down[i], cap)
        h = _final(pad_starts, tope, rank, gates.T, x1, p[i].reshape(n, -1), y,
                   row(ln2_g[i]), row(ln2_b[i]), w_pg[i].astype(BF16), row(b_pg[i]),
                   w_ple[i].astype(BF16), row(ln3_g[i]), row(ln3_b[i]), cap, alpha=alpha)
    return h.reshape(bsz, seq, d)
```

```python
import functools

import jax
import jax.numpy as jnp
from jax import lax
from jax.experimental import pallas as pl
from jax.experimental.pallas import tpu as pltpu

F32 = jnp.float32
BF16 = jnp.bfloat16
U32 = jnp.uint32

POOL_WINDOWS = (2, 4, 8, 16)
N_POOL_GROUPS = 4
CONV_K = 3
N_EXPERTS = 32
TOP_K = 4
SWIGLU_LIMIT = 7.0
SWIGLU_ALPHA = 1.702
LN_EPS = 1e-5

LANES = 128
VMEM_LIMIT_BYTES = 56 * 1024 * 1024

HALO = 16
PROJ_BM = 1024
PROJ_BN = 1024
MIX_TM = 256
DISP_TB = 512
ROW_BLK = 128
ITEM_BLKS = 10
FFN_TF = 512


def _layer_norm(v, g, b):
    mu = jnp.mean(v, axis=-1, keepdims=True)
    c = v - mu
    var = jnp.mean(c * c, axis=-1, keepdims=True)
    return c * lax.rsqrt(var + LN_EPS) * g + b


def _dot(a, b):
    return jnp.dot(a, b, preferred_element_type=F32)


def _pack_pairs(v):
    half = v.shape[1] // 2
    bits = pltpu.bitcast(v.astype(BF16).astype(F32), U32)
    return (bits[:, 0:half] & U32(0xFFFF0000)) | (bits[:, half:] >> 16)


def _store_slabs(ref, row0, packed):
    rows, width = packed.shape
    s = width // LANES
    for c in range(s):
        ref[pl.ds(row0 * s + c, rows, stride=s), :] = packed[:, c * LANES:(c + 1) * LANES]


def _load_slab_column(ref, row0, rows, s, c):
    p = ref[pl.ds(row0 * s + c, rows, stride=s), :]
    return pltpu.bitcast(p & U32(0xFFFF0000), F32), pltpu.bitcast(p << 16, F32)


def _proj_kernel(x_ref, w_ref, o_ref):
    o_ref[...] = _dot(x_ref[...].astype(BF16), w_ref[...].astype(BF16)).astype(o_ref.dtype)


def _proj(xb, wb):
    n, d = xb.shape
    n_in = wb.shape[1]
    return pl.pallas_call(
        _proj_kernel,
        grid=(n // PROJ_BM, n_in // PROJ_BN),
        in_specs=[pl.BlockSpec((PROJ_BM, d), lambda i, j: (i, 0)),
                  pl.BlockSpec((d, PROJ_BN), lambda i, j: (0, j))],
        out_specs=pl.BlockSpec((PROJ_BM, PROJ_BN), lambda i, j: (i, j)),
        out_shape=jax.ShapeDtypeStruct((n, n_in), BF16),
        compiler_params=pltpu.CompilerParams(
            dimension_semantics=("arbitrary", "arbitrary"),
            vmem_limit_bytes=VMEM_LIMIT_BYTES),
        name="proj",
    )(xb, wb)


def _mixer_kernel(proj_ref, halo_ref, x_ref, wpool_ref, pscale_ref, convw_ref,
                  wbra_ref, wbrb_ref, wo_ref, g1_ref, b1_ref, wrt_ref, br_ref,
                  x1_ref, x1p_ref, tope_ref, gate_ref, rank_ref, cnt_ref,
                  ext_ref, carry_ref, *, seq, alpha, pw, cw, d):
    tm = x_ref.shape[0]
    i = pl.program_id(0)
    blocks_per_seq = seq // tm
    j = i % blocks_per_seq
    keep_halo = (j > 0).astype(F32)
    row = lax.broadcasted_iota(jnp.int32, (tm, 1), 0)
    pos1 = (j * tm + row + 1).astype(F32)

    @pl.when(i == 0)
    def _():
        carry_ref[...] = jnp.zeros_like(carry_ref)

    u = proj_ref[:, 0:pw].astype(F32)
    ext_ref[0:HALO, :] = halo_ref[:, 0:pw].astype(F32) * keep_halo
    ext_ref[HALO:HALO + tm, :] = u
    gw = pw // N_POOL_GROUPS
    a_parts = []
    for g, w in enumerate(POOL_WINDOWS):
        cols = slice(g * gw, (g + 1) * gw)
        s = u[:, cols]
        for sh in range(1, w):
            s = s + ext_ref[HALO - sh:HALO - sh + tm, cols]
        cnt = jnp.minimum(pos1, float(w))
        pooled = s / cnt - u[:, cols]
        a_g = _dot(pooled.astype(BF16), wpool_ref[g]) * pscale_ref[:, cols]
        a_parts.append(a_g.astype(BF16))
    a = jnp.concatenate(a_parts, axis=1)
    br_a = _dot(a, wbra_ref[...])

    o1, o2, o3 = pw + cw, pw + 2 * cw, pw + 3 * cw
    s = proj_ref[:, o1:o2].astype(F32) * proj_ref[:, o2:o3].astype(F32)
    ext_ref[0:HALO, :] = (halo_ref[:, o1:o2].astype(F32) * halo_ref[:, o2:o3].astype(F32)) * keep_halo
    ext_ref[HALO:HALO + tm, :] = s
    conv = s * convw_ref[CONV_K - 1:CONV_K, :]
    for k in range(CONV_K - 1):
        sh = CONV_K - 1 - k
        conv = conv + ext_ref[HALO - sh:HALO - sh + tm, :] * convw_ref[k:k + 1, :]
    b = proj_ref[:, pw:o1].astype(F32) * conv
    br_b = _dot(b.astype(BF16), wbrb_ref[...])

    g_a = proj_ref[:, o3:o3 + d].astype(F32)
    g_b = proj_ref[:, o3 + d:o3 + 2 * d].astype(F32)
    m = jax.nn.sigmoid(g_a) * br_a + jax.nn.sigmoid(g_b) * br_b
    mix = _dot(m.astype(BF16), wo_ref[...])
    x1 = _layer_norm(alpha * x_ref[...] + mix, g1_ref[...], b1_ref[...])
    x1_ref[...] = x1
    _store_slabs(x1p_ref, 0, _pack_pairs(x1))

    logits = lax.dot_general(wrt_ref[...], x1.astype(BF16), (((1,), (1,)), ((), ())),
                             preferred_element_type=F32) + br_ref[...]
    e_iota = lax.broadcasted_iota(jnp.int32, logits.shape, 0)
    vals, idxs = [], []
    l = logits
    for _ in range(TOP_K):
        mx = jnp.max(l, axis=0, keepdims=True)
        ix = jnp.min(jnp.where(l == mx, e_iota, N_EXPERTS), axis=0, keepdims=True)
        vals.append(mx)
        idxs.append(ix)
        l = jnp.where(e_iota == ix, -jnp.inf, l)
    exps = [jnp.exp(v - vals[0]) for v in vals]
    denom = exps[0] + exps[1] + exps[2] + exps[3]
    onehot = jnp.zeros(logits.shape, F32)
    for k in range(TOP_K):
        tope_ref[k:k + 1, :] = idxs[k]
        gate_ref[k:k + 1, :] = exps[k] / denom
        onehot = onehot + (e_iota == idxs[k]).astype(F32)

    r_i = lax.broadcasted_iota(jnp.int32, (tm, tm), 0)
    c_i = lax.broadcasted_iota(jnp.int32, (tm, tm), 1)
    before = (r_i < c_i).astype(BF16)
    seen = _dot(onehot.astype(BF16), before) + carry_ref[:, 0:1]
    for k in range(TOP_K):
        rk = jnp.sum(jnp.where(e_iota == idxs[k], seen, 0.0), axis=0, keepdims=True)
        rank_ref[k:k + 1, :] = rk.astype(jnp.int32)
    carry_ref[...] = carry_ref[...] + jnp.sum(onehot, axis=1, keepdims=True)
    cnt_ref[...] = carry_ref[...].astype(jnp.int32)


def _mixer(proj, x2, wpool, pscale, convw, wbra, wbrb, wo, g1, b1, wrt, br, *, seq, alpha):
    n, d = x2.shape
    n_in = proj.shape[1]
    pw = wbra.shape[0]
    cw = wbrb.shape[0]
    tm = MIX_TM
    slab = d // 2 // LANES
    hb = tm // HALO
    const = lambda i: (0, 0)
    kern = functools.partial(_mixer_kernel, seq=seq, alpha=alpha, pw=pw, cw=cw, d=d)
    return pl.pallas_call(
        kern,
        grid=(n // tm,),
        in_specs=[
            pl.BlockSpec((tm, n_in), lambda i: (i, 0)),
            pl.BlockSpec((HALO, n_in // 2), lambda i: (jnp.maximum(i * hb - 1, 0), 0)),
            pl.BlockSpec((tm, d), lambda i: (i, 0)),
            pl.BlockSpec(wpool.shape, lambda i: (0, 0, 0)),
            pl.BlockSpec(pscale.shape, const),
            pl.BlockSpec(convw.shape, const),
            pl.BlockSpec(wbra.shape, const),
            pl.BlockSpec(wbrb.shape, const),
            pl.BlockSpec(wo.shape, const),
            pl.BlockSpec(g1.shape, const),
            pl.BlockSpec(b1.shape, const),
            pl.BlockSpec(wrt.shape, const),
            pl.BlockSpec(br.shape, const),
        ],
        out_specs=[
            pl.BlockSpec((tm, d), lambda i: (i, 0)),
            pl.BlockSpec((tm * slab, LANES), lambda i: (i, 0)),
            pl.BlockSpec((TOP_K, tm), lambda i: (0, i)),
            pl.BlockSpec((TOP_K, tm), lambda i: (0, i)),
            pl.BlockSpec((TOP_K, tm), lambda i: (0, i)),
            pl.BlockSpec((N_EXPERTS, LANES), const),
        ],
        out_shape=[
            jax.ShapeDtypeStruct((n, d), F32),
            jax.ShapeDtypeStruct((n * slab, LANES), U32),
            jax.ShapeDtypeStruct((TOP_K, n), jnp.int32),
            jax.ShapeDtypeStruct((TOP_K, n), F32),
            jax.ShapeDtypeStruct((TOP_K, n), jnp.int32),
            jax.ShapeDtypeStruct((N_EXPERTS, LANES), jnp.int32),
        ],
        scratch_shapes=[pltpu.VMEM((HALO + tm, pw), F32),
                        pltpu.VMEM((N_EXPERTS, LANES), F32)],
        compiler_params=pltpu.CompilerParams(
            dimension_semantics=("arbitrary",),
            vmem_limit_bytes=VMEM_LIMIT_BYTES),
        name="mixer",
    )(proj, proj, x2, wpool, pscale, convw, wbra, wbrb, wo, g1, b1, wrt, br)


def _dispatch_kernel(pst_ref, pend_ref, dest_ref, x1p_ref, xs_hbm, zero_ref, sem, zsem,
                     *, slab):
    tb = dest_ref.shape[1]
    i = pl.program_id(0)
    blk = ROW_BLK * slab

    def zero_copy(b):
        return pltpu.make_async_copy(
            zero_ref, xs_hbm.at[pl.ds(pl.multiple_of(b * blk, blk), blk)], zsem)

    @pl.when(i == 0)
    def _():
        zero_ref[...] = jnp.zeros_like(zero_ref)

        def start(e, c):
            @pl.when(pend_ref[e] > pst_ref[e])
            def _():
                zero_copy(pend_ref[e] // ROW_BLK - 1).start()
            return c

        def wait(e, c):
            @pl.when(pend_ref[e] > pst_ref[e])
            def _():
                zero_copy(pend_ref[e] // ROW_BLK - 1).wait()
            return c

        def start_tail(b, c):
            zero_copy(b).start()
            return c

        def wait_tail(b, c):
            zero_copy(b).wait()
            return c

        used = pend_ref[N_EXPERTS - 1] // ROW_BLK
        total = xs_hbm.shape[0] // blk
        lax.fori_loop(0, N_EXPERTS, start, 0)
        lax.fori_loop(used, total, start_tail, 0)
        lax.fori_loop(0, N_EXPERTS, wait, 0)
        lax.fori_loop(used, total, wait_tail, 0)

    def start(t, c):
        src = x1p_ref.at[pl.ds(pl.multiple_of(t * slab, slab), slab)]
        for k in range(TOP_K):
            dst = dest_ref[k, t]
            pltpu.make_async_copy(
                src, xs_hbm.at[pl.ds(pl.multiple_of(dst * slab, slab), slab)], sem).start()
        return c

    lax.fori_loop(0, tb, start, 0)
    for k in range(TOP_K):
        pltpu.make_async_copy(x1p_ref, xs_hbm.at[pl.ds(0, tb * slab)], sem).wait()


def _dispatch(pad_starts, pad_ends, dest, x1p, n, cap):
    slab = x1p.shape[0] // n
    tb = DISP_TB
    smem_blk = pl.BlockSpec((TOP_K, tb), lambda i, *_: (0, i), memory_space=pltpu.SMEM)
    return pl.pallas_call(
        functools.partial(_dispatch_kernel, slab=slab),
        grid_spec=pltpu.PrefetchScalarGridSpec(
            num_scalar_prefetch=2,
            grid=(n // tb,),
            in_specs=[smem_blk,
                      pl.BlockSpec((tb * slab, LANES), lambda i, *_: (i, 0))],
            out_specs=pl.BlockSpec(memory_space=pl.ANY),
            scratch_shapes=[pltpu.VMEM((ROW_BLK * slab, LANES), U32),
                            pltpu.SemaphoreType.DMA(()),
                            pltpu.SemaphoreType.DMA(())],
        ),
        out_shape=jax.ShapeDtypeStruct((cap * slab, LANES), U32),
        compiler_params=pltpu.CompilerParams(
            dimension_semantics=("arbitrary",),
            has_side_effects=True),
        name="dispatch",
    )(pad_starts, pad_ends, dest, x1p)


def _ffn_kernel(ie_ref, ist_ref, inb_ref, used_ref,
                xs_hbm, wg_ref, wu_ref, bg_ref, bu_ref, wd_ref, bd_ref,
                y_hbm,
                slabs, xbuf, yacc, sem_in, sem_out, *, slab):
    w = pl.program_id(0)
    f = pl.program_id(1)
    n_items = pl.num_programs(0)
    nf = pl.num_programs(1)
    nb = inb_ref[w]
    blk = ROW_BLK * slab
    half = slab * LANES
    slot_rows = slabs.shape[0] // 2
    first_step = f == 0
    last_step = f == nf - 1
    prev_item = jnp.maximum(w - 1, 0)
    next_item = jnp.minimum(w + 1, n_items - 1)

    def rows(r):
        return pl.ds(pl.multiple_of(r * ROW_BLK, ROW_BLK), ROW_BLK)

    def slot_row0(item):
        return (item % 2) * (slot_rows // slab)

    def hbm_blk(ref, item, r):
        return ref.at[pl.ds(pl.multiple_of((ist_ref[item] + r) * blk, blk), blk)]

    def vmem_blk(item, r):
        return slabs.at[pl.ds(pl.multiple_of((item % 2) * slot_rows + r * blk, blk), blk)]

    def for_blocks(item, fn):
        def body(r, c):
            fn(r)
            return c
        lax.fori_loop(0, inb_ref[item], body, 0)

    def load(item):
        return lambda r: pltpu.make_async_copy(hbm_blk(xs_hbm, item, r), vmem_blk(item, r), sem_in)

    def store(item):
        return lambda r: pltpu.make_async_copy(
            vmem_blk(item, r), hbm_blk(y_hbm, item, r), sem_out.at[item % 2])

    @pl.when((w == 0) & first_step)
    def _():
        slabs[0:blk, :] = jnp.zeros((blk, LANES), U32)

        def tail_copy(b):
            return pltpu.make_async_copy(
                slabs.at[pl.ds(0, blk)], y_hbm.at[pl.ds(pl.multiple_of(b * blk, blk), blk)],
                sem_out.at[0])

        def start(b, c):
            tail_copy(b).start()
            return c

        def wait(b, c):
            tail_copy(b).wait()
            return c

        total = y_hbm.shape[0] // blk
        lax.fori_loop(used_ref[0], total, start, 0)
        lax.fori_loop(used_ref[0], total, wait, 0)
        for_blocks(w, lambda r: load(w)(r).start())

    @pl.when(last_step)
    def _():
        @pl.when(w >= 1)
        def _():
            for_blocks(prev_item, lambda r: store(prev_item)(r).wait())

        @pl.when(w + 1 < n_items)
        def _():
            for_blocks(next_item, lambda r: load(next_item)(r).start())

    @pl.when(nb > 0)
    def _():
        @pl.when(first_step)
        def _():
            for_blocks(w, lambda r: load(w)(r).wait())

            def unpack(r):
                for col in range(slab):
                    hi, lo = _load_slab_column(slabs, slot_row0(w) + r * ROW_BLK, ROW_BLK, slab, col)
                    xbuf[rows(r), col * LANES:(col + 1) * LANES] = hi.astype(BF16)
                    xbuf[rows(r), half + col * LANES:half + (col + 1) * LANES] = lo.astype(BF16)
                yacc[rows(r), :] = jnp.broadcast_to(bd_ref[...], (ROW_BLK, yacc.shape[1]))

            for_blocks(w, unpack)

        def sub(row0, n_rows):
            rs = pl.ds(pl.multiple_of(row0, ROW_BLK), n_rows)
            xt = xbuf[rs, :]
            gate = jnp.minimum(_dot(xt, wg_ref[...]) + bg_ref[...], SWIGLU_LIMIT)
            up = jnp.clip(_dot(xt, wu_ref[...]) + bu_ref[...], -SWIGLU_LIMIT, SWIGLU_LIMIT)
            act = (up + 1.0) * (gate * jax.nn.sigmoid(SWIGLU_ALPHA * gate))
            yacc[rs, :] += _dot(act.astype(BF16), wd_ref[...])

        def pair(q, c):
            sub(q * (2 * ROW_BLK), 2 * ROW_BLK)
            return c

        lax.fori_loop(0, nb // 2, pair, 0)

        @pl.when(nb % 2 == 1)
        def _():
            sub((nb - 1) * ROW_BLK, ROW_BLK)

        @pl.when(last_step)
        def _():
            def pack_and_store(r):
                _store_slabs(slabs, slot_row0(w) + r * ROW_BLK, _pack_pairs(yacc[rows(r), :]))
                store(w)(r).start()

            for_blocks(w, pack_and_store)

    @pl.when(last_step & (w == n_items - 1))
    def _():
        for_blocks(w, lambda r: store(w)(r).wait())


def _ffn(item_e, item_st, item_nb, used_blocks, xs, w_gu, b_gu, w_down, b_down, cap):
    slab = xs.shape[0] // cap
    n_e, d, f2 = w_gu.shape
    dff = f2 // 2
    tf = FFN_TF
    nf = dff // tf
    assert nf >= 2, "the ffn kernel prefetches the next item during a step that is not the first"
    n_items = item_e.shape[0]
    ts = ITEM_BLKS * ROW_BLK

    def fsel(w, f, inb):
        return jnp.where(inb[w] > 0, f, nf - 1)

    wg_spec = pl.BlockSpec((None, d, tf), lambda w, f, ie, ist, inb, u: (ie[w], 0, fsel(w, f, inb)))
    wu_spec = pl.BlockSpec((None, d, tf), lambda w, f, ie, ist, inb, u: (ie[w], 0, nf + fsel(w, f, inb)))
    bg_spec = pl.BlockSpec((None, 1, tf), lambda w, f, ie, ist, inb, u: (ie[w], 0, fsel(w, f, inb)))
    bu_spec = pl.BlockSpec((None, 1, tf), lambda w, f, ie, ist, inb, u: (ie[w], 0, nf + fsel(w, f, inb)))
    wd_spec = pl.BlockSpec((None, tf, d), lambda w, f, ie, ist, inb, u: (ie[w], fsel(w, f, inb), 0))
    bd_spec = pl.BlockSpec((None, 1, d), lambda w, f, ie, ist, inb, u: (ie[w], 0, 0))
    return pl.pallas_call(
        functools.partial(_ffn_kernel, slab=slab),
        grid_spec=pltpu.PrefetchScalarGridSpec(
            num_scalar_prefetch=4,
            grid=(n_items, nf),
            in_specs=[pl.BlockSpec(memory_space=pl.ANY),
                      wg_spec, wu_spec, bg_spec, bu_spec, wd_spec, bd_spec],
            out_specs=pl.BlockSpec(memory_space=pl.ANY),
            scratch_shapes=[pltpu.VMEM((2 * ts * slab, LANES), U32),
                            pltpu.VMEM((ts, d), BF16),
                            pltpu.VMEM((ts, d), F32),
                            pltpu.SemaphoreType.DMA(()),
                            pltpu.SemaphoreType.DMA((2,))],
        ),
        out_shape=jax.ShapeDtypeStruct((cap * slab, LANES), U32),
        compiler_params=pltpu.CompilerParams(
            dimension_semantics=("arbitrary", "arbitrary"),
            vmem_limit_bytes=VMEM_LIMIT_BYTES,
            has_side_effects=True),
        name="ffn",
    )(item_e, item_st, item_nb, used_blocks, xs, w_gu, w_gu, b_gu.reshape(n_e, 1, f2), b_gu.reshape(n_e, 1, f2),
      w_down, b_down.reshape(n_e, 1, d))


def _final_kernel(dest_ref, gate_ref, x1_ref, p_ref, y_hbm,
                  g2_ref, b2_ref, wpg_ref, bpg_ref, wple_ref, g3_ref, b3_ref,
                  o_ref, ybuf, ffn_ref, sem, *, alpha, slab):
    tm = x1_ref.shape[0]
    half = slab * LANES

    def start(t, c):
        for k in range(TOP_K):
            src = dest_ref[k, t]
            pltpu.make_async_copy(
                y_hbm.at[pl.ds(pl.multiple_of(src * slab, slab), slab)],
                ybuf.at[pl.ds(pl.multiple_of((k * tm + t) * slab, slab), slab)], sem).start()
        return c

    lax.fori_loop(0, tm, start, 0)
    pltpu.make_async_copy(y_hbm.at[pl.ds(0, ybuf.shape[0])], ybuf, sem).wait()

    for col in range(slab):
        acc_hi = acc_lo = None
        for k in range(TOP_K):
            hi, lo = _load_slab_column(ybuf, k * tm, tm, slab, col)
            g = gate_ref[:, k:k + 1]
            acc_hi = hi * g if acc_hi is None else acc_hi + hi * g
            acc_lo = lo * g if acc_lo is None else acc_lo + lo * g
        ffn_ref[:, col * LANES:(col + 1) * LANES] = acc_hi
        ffn_ref[:, half + col * LANES:half + (col + 1) * LANES] = acc_lo

    h2 = _layer_norm(alpha * x1_ref[...] + ffn_ref[...], g2_ref[...], b2_ref[...])
    pg = jax.nn.sigmoid(_dot(h2.astype(BF16), wpg_ref[...]) + bpg_ref[...])
    pe = _dot(p_ref[...].astype(BF16), wple_ref[...])
    o_ref[...] = _layer_norm(alpha * h2 + pg * pe, g3_ref[...], b3_ref[...])


def _final(dest, gates_t, x1, p2, y, g2, b2, wpg, bpg, wple, g3, b3, cap, *, alpha):
    n, d = x1.shape
    slab = y.shape[0] // cap
    tm = MIX_TM
    smem_blk = pl.BlockSpec((TOP_K, tm), lambda i: (0, i), memory_space=pltpu.SMEM)
    const = lambda i: (0, 0)
    row_blk = lambda width: pl.BlockSpec((tm, width), lambda i: (i, 0))
    return pl.pallas_call(
        functools.partial(_final_kernel, alpha=alpha, slab=slab),
        grid_spec=pltpu.PrefetchScalarGridSpec(
            num_scalar_prefetch=0,
            grid=(n // tm,),
            in_specs=[smem_blk, row_blk(TOP_K), row_blk(d), row_blk(p2.shape[1]),
                      pl.BlockSpec(memory_space=pl.ANY),
                      pl.BlockSpec(g2.shape, const), pl.BlockSpec(b2.shape, const),
                      pl.BlockSpec(wpg.shape, const), pl.BlockSpec(bpg.shape, const),
                      pl.BlockSpec(wple.shape, const),
                      pl.BlockSpec(g3.shape, const), pl.BlockSpec(b3.shape, const)],
            out_specs=row_blk(d),
            scratch_shapes=[pltpu.VMEM((TOP_K * tm * slab, LANES), U32),
                            pltpu.VMEM((tm, d), F32),
                            pltpu.SemaphoreType.DMA(())],
        ),
        out_shape=jax.ShapeDtypeStruct((n, d), F32),
        compiler_params=pltpu.CompilerParams(
            dimension_semantics=("arbitrary",),
            vmem_limit_bytes=VMEM_LIMIT_BYTES),
        name="final",
    )(dest, gates_t, x1, p2, y, g2, b2, wpg, bpg, wple, g3, b3)


def _plan(counts, n_items):
    nblk = (counts + ROW_BLK - 1) // ROW_BLK
    blk_end = jnp.cumsum(nblk)
    blk_start = blk_end - nblk
    items_per = (nblk + ITEM_BLKS - 1) // ITEM_BLKS
    item_end = jnp.cumsum(items_per)
    item_start = item_end - items_per
    total = item_end[-1]
    w = jnp.arange(n_items, dtype=jnp.int32)
    w_eff = jnp.minimum(w, total - 1)
    e = jnp.minimum(jnp.searchsorted(item_end, w_eff, side="right"), N_EXPERTS - 1).astype(jnp.int32)
    j = w_eff - item_start[e]
    st = blk_start[e] + j * ITEM_BLKS
    nb = jnp.where(w < total, jnp.clip(nblk[e] - j * ITEM_BLKS, 0, ITEM_BLKS), 0)
    return ((blk_start * ROW_BLK).astype(jnp.int32), (blk_end * ROW_BLK).astype(jnp.int32),
            e, st.astype(jnp.int32), nb.astype(jnp.int32))


def kernel(x, p, w_in, w_pool, pool_scale, conv_w, w_br_a, w_br_b, w_o, ln1_g, ln1_b,
           w_router, b_router, w_gu, b_gu, w_down, b_down, ln2_g, ln2_b,
           w_pg, b_pg, w_ple, ln3_g, ln3_b):
    depth = w_in.shape[0]
    bsz, seq, d = x.shape
    n = bsz * seq
    alpha = (2.0 * depth) ** 0.25
    nk = n * TOP_K
    n_row_blocks = (nk + N_EXPERTS * (ROW_BLK - 1) + ROW_BLK - 1) // ROW_BLK
    cap = n_row_blocks * ROW_BLK
    n_items = N_EXPERTS + n_row_blocks // ITEM_BLKS
    row = lambda v: v.reshape(1, -1)

    h = x.reshape(n, d)
    for i in range(depth):
        proj = _proj(h, w_in[i])
        x1, x1p, tope, gates, rank, cnt = _mixer(
            proj, h, w_pool[i].astype(BF16), row(pool_scale[i]), conv_w[i],
            w_br_a[i].astype(BF16), w_br_b[i].astype(BF16), w_o[i].astype(BF16),
            row(ln1_g[i]), row(ln1_b[i]), w_router[i].T.astype(BF16),
            b_router[i].reshape(N_EXPERTS, 1), seq=seq, alpha=alpha)
        pad_starts, pad_ends, item_e, item_st, item_nb = _plan(cnt[:, 0], n_items)
        dest = pad_starts[tope] + rank
        xs = _dispatch(pad_starts, pad_ends, dest, x1p, n, cap)
        used_blocks = pad_ends[N_EXPERTS - 1:] // ROW_BLK
        y = _ffn(item_e, item_st, item_nb, used_blocks, xs, w_gu[i], b_gu[i], w_down[i], b_down[i], cap)
        h = _final(dest, gates.T, x1, p[i].reshape(n, -1), y,
                   row(ln2_g[i]), row(ln2_b[i]), w_pg[i].astype(BF16), row(b_pg[i]),
                   w_ple[i].astype(BF16), row(ln3_g[i]), row(ln3_b[i]), cap, alpha=alpha)
    return h.reshape(bsz, seq, d)
```

```python
import functools

import jax
import jax.numpy as jnp
from jax import lax
from jax.experimental import pallas as pl
from jax.experimental.pallas import tpu as pltpu

F32 = jnp.float32
BF16 = jnp.bfloat16
U32 = jnp.uint32

POOL_WINDOWS = (2, 4, 8, 16)
N_POOL_GROUPS = 4
CONV_K = 3
N_EXPERTS = 32
TOP_K = 4
SWIGLU_LIMIT = 7.0
SWIGLU_ALPHA = 1.702
LN_EPS = 1e-5

LANES = 128
VMEM_LIMIT_BYTES = 56 * 1024 * 1024

HALO = 16
PROJ_BM = 1024
PROJ_BN = 1024
MIX_TM = 256
DISP_TB = 512
ROW_BLK = 128
ITEM_BLKS = 10
FFN_TF = 512


def _layer_norm(v, g, b):
    mu = jnp.mean(v, axis=-1, keepdims=True)
    c = v - mu
    var = jnp.mean(c * c, axis=-1, keepdims=True)
    return c * lax.rsqrt(var + LN_EPS) * g + b


def _dot(a, b):
    return jnp.dot(a, b, preferred_element_type=F32)


def _pack_pairs(v):
    half = v.shape[1] // 2
    bits = pltpu.bitcast(v.astype(BF16).astype(F32), U32)
    return (bits[:, 0:half] & U32(0xFFFF0000)) | (bits[:, half:] >> 16)


def _store_slabs(ref, row0, packed):
    rows, width = packed.shape
    s = width // LANES
    for c in range(s):
        ref[pl.ds(row0 * s + c, rows, stride=s), :] = packed[:, c * LANES:(c + 1) * LANES]


def _load_slab_column(ref, row0, rows, s, c):
    p = ref[pl.ds(row0 * s + c, rows, stride=s), :]
    return pltpu.bitcast(p & U32(0xFFFF0000), F32), pltpu.bitcast(p << 16, F32)


def _proj_kernel(x_ref, w_ref, o_ref):
    o_ref[...] = _dot(x_ref[...].astype(BF16), w_ref[...].astype(BF16)).astype(o_ref.dtype)


def _proj(xb, wb):
    n, d = xb.shape
    n_in = wb.shape[1]
    return pl.pallas_call(
        _proj_kernel,
        grid=(n // PROJ_BM, n_in // PROJ_BN),
        in_specs=[pl.BlockSpec((PROJ_BM, d), lambda i, j: (i, 0)),
                  pl.BlockSpec((d, PROJ_BN), lambda i, j: (0, j))],
        out_specs=pl.BlockSpec((PROJ_BM, PROJ_BN), lambda i, j: (i, j)),
        out_shape=jax.ShapeDtypeStruct((n, n_in), BF16),
        compiler_params=pltpu.CompilerParams(
            dimension_semantics=("arbitrary", "arbitrary"),
            vmem_limit_bytes=VMEM_LIMIT_BYTES),
        name="proj",
    )(xb, wb)


def _mixer_kernel(proj_ref, halo_ref, x_ref, wpool_ref, pscale_ref, convw_ref,
                  wbra_ref, wbrb_ref, wo_ref, g1_ref, b1_ref, wrt_ref, br_ref,
                  x1_ref, x1p_ref, tope_ref, gate_ref, rank_ref, cnt_ref,
                  ext_ref, carry_ref, *, seq, alpha, pw, cw, d):
    tm = x_ref.shape[0]
    i = pl.program_id(0)
    blocks_per_seq = seq // tm
    j = i % blocks_per_seq
    keep_halo = (j > 0).astype(F32)
    row = lax.broadcasted_iota(jnp.int32, (tm, 1), 0)
    pos1 = (j * tm + row + 1).astype(F32)

    @pl.when(i == 0)
    def _():
        carry_ref[...] = jnp.zeros_like(carry_ref)

    u = proj_ref[:, 0:pw].astype(F32)
    ext_ref[0:HALO, :] = halo_ref[:, 0:pw].astype(F32) * keep_halo
    ext_ref[HALO:HALO + tm, :] = u
    gw = pw // N_POOL_GROUPS
    a_parts = []
    for g, w in enumerate(POOL_WINDOWS):
        cols = slice(g * gw, (g + 1) * gw)
        s = u[:, cols]
        for sh in range(1, w):
            s = s + ext_ref[HALO - sh:HALO - sh + tm, cols]
        cnt = jnp.minimum(pos1, float(w))
        pooled = s / cnt - u[:, cols]
        a_g = _dot(pooled.astype(BF16), wpool_ref[g]) * pscale_ref[:, cols]
        a_parts.append(a_g.astype(BF16))
    a = jnp.concatenate(a_parts, axis=1)
    br_a = _dot(a, wbra_ref[...])

    o1, o2, o3 = pw + cw, pw + 2 * cw, pw + 3 * cw
    s = proj_ref[:, o1:o2].astype(F32) * proj_ref[:, o2:o3].astype(F32)
    ext_ref[0:HALO, :] = (halo_ref[:, o1:o2].astype(F32) * halo_ref[:, o2:o3].astype(F32)) * keep_halo
    ext_ref[HALO:HALO + tm, :] = s
    conv = s * convw_ref[CONV_K - 1:CONV_K, :]
    for k in range(CONV_K - 1):
        sh = CONV_K - 1 - k
        conv = conv + ext_ref[HALO - sh:HALO - sh + tm, :] * convw_ref[k:k + 1, :]
    b = proj_ref[:, pw:o1].astype(F32) * conv
    br_b = _dot(b.astype(BF16), wbrb_ref[...])

    g_a = proj_ref[:, o3:o3 + d].astype(F32)
    g_b = proj_ref[:, o3 + d:o3 + 2 * d].astype(F32)
    m = jax.nn.sigmoid(g_a) * br_a + jax.nn.sigmoid(g_b) * br_b
    mix = _dot(m.astype(BF16), wo_ref[...])
    x1 = _layer_norm(alpha * x_ref[...] + mix, g1_ref[...], b1_ref[...])
    x1_ref[...] = x1
    _store_slabs(x1p_ref, 0, _pack_pairs(x1))

    logits = lax.dot_general(wrt_ref[...], x1.astype(BF16), (((1,), (1,)), ((), ())),
                             preferred_element_type=F32) + br_ref[...]
    e_iota = lax.broadcasted_iota(jnp.int32, logits.shape, 0)
    vals, idxs = [], []
    l = logits
    for _ in range(TOP_K):
        mx = jnp.max(l, axis=0, keepdims=True)
        ix = jnp.min(jnp.where(l == mx, e_iota, N_EXPERTS), axis=0, keepdims=True)
        vals.append(mx)
        idxs.append(ix)
        l = jnp.where(e_iota == ix, -jnp.inf, l)
    exps = [jnp.exp(v - vals[0]) for v in vals]
    denom = exps[0] + exps[1] + exps[2] + exps[3]
    onehot = jnp.zeros(logits.shape, F32)
    for k in range(TOP_K):
        tope_ref[k:k + 1, :] = idxs[k]
        gate_ref[k:k + 1, :] = exps[k] / denom
        onehot = onehot + (e_iota == idxs[k]).astype(F32)

    r_i = lax.broadcasted_iota(jnp.int32, (tm, tm), 0)
    c_i = lax.broadcasted_iota(jnp.int32, (tm, tm), 1)
    before = (r_i < c_i).astype(BF16)
    seen = _dot(onehot.astype(BF16), before) + carry_ref[:, 0:1]
    for k in range(TOP_K):
        rk = jnp.sum(jnp.where(e_iota == idxs[k], seen, 0.0), axis=0, keepdims=True)
        rank_ref[k:k + 1, :] = rk.astype(jnp.int32)
    carry_ref[...] = carry_ref[...] + jnp.sum(onehot, axis=1, keepdims=True)
    cnt_ref[...] = carry_ref[...].astype(jnp.int32)


def _mixer(proj, x2, wpool, pscale, convw, wbra, wbrb, wo, g1, b1, wrt, br, *, seq, alpha):
    n, d = x2.shape
    n_in = proj.shape[1]
    pw = wbra.shape[0]
    cw = wbrb.shape[0]
    tm = MIX_TM
    slab = d // 2 // LANES
    hb = tm // HALO
    const = lambda i: (0, 0)
    kern = functools.partial(_mixer_kernel, seq=seq, alpha=alpha, pw=pw, cw=cw, d=d)
    return pl.pallas_call(
        kern,
        grid=(n // tm,),
        in_specs=[
            pl.BlockSpec((tm, n_in), lambda i: (i, 0)),
            pl.BlockSpec((HALO, n_in // 2), lambda i: (jnp.maximum(i * hb - 1, 0), 0)),
            pl.BlockSpec((tm, d), lambda i: (i, 0)),
            pl.BlockSpec(wpool.shape, lambda i: (0, 0, 0)),
            pl.BlockSpec(pscale.shape, const),
            pl.BlockSpec(convw.shape, const),
            pl.BlockSpec(wbra.shape, const),
            pl.BlockSpec(wbrb.shape, const),
            pl.BlockSpec(wo.shape, const),
            pl.BlockSpec(g1.shape, const),
            pl.BlockSpec(b1.shape, const),
            pl.BlockSpec(wrt.shape, const),
            pl.BlockSpec(br.shape, const),
        ],
        out_specs=[
            pl.BlockSpec((tm, d), lambda i: (i, 0)),
            pl.BlockSpec((tm * slab, LANES), lambda i: (i, 0)),
            pl.BlockSpec((TOP_K, tm), lambda i: (0, i)),
            pl.BlockSpec((TOP_K, tm), lambda i: (0, i)),
            pl.BlockSpec((TOP_K, tm), lambda i: (0, i)),
            pl.BlockSpec((N_EXPERTS, LANES), const),
        ],
        out_shape=[
            jax.ShapeDtypeStruct((n, d), F32),
            jax.ShapeDtypeStruct((n * slab, LANES), U32),
            jax.ShapeDtypeStruct((TOP_K, n), jnp.int32),
            jax.ShapeDtypeStruct((TOP_K, n), F32),
            jax.ShapeDtypeStruct((TOP_K, n), jnp.int32),
            jax.ShapeDtypeStruct((N_EXPERTS, LANES), jnp.int32),
        ],
        scratch_shapes=[pltpu.VMEM((HALO + tm, pw), F32),
                        pltpu.VMEM((N_EXPERTS, LANES), F32)],
        compiler_params=pltpu.CompilerParams(
            dimension_semantics=("arbitrary",),
            vmem_limit_bytes=VMEM_LIMIT_BYTES),
        name="mixer",
    )(proj, proj, x2, wpool, pscale, convw, wbra, wbrb, wo, g1, b1, wrt, br)


def _dest_kernel(pst_ref, tope_ref, rank_ref, dest_ref):
    tope = tope_ref[...]
    dest = rank_ref[...]
    for e in range(N_EXPERTS):
        dest = dest + jnp.where(tope == e, pst_ref[e], 0)
    dest_ref[...] = dest


def _dest(pad_starts, tope, rank):
    whole = pl.BlockSpec(tope.shape, lambda i, *_: (0, 0))
    return pl.pallas_call(
        _dest_kernel,
        grid_spec=pltpu.PrefetchScalarGridSpec(
            num_scalar_prefetch=1, grid=(1,), in_specs=[whole, whole], out_specs=whole),
        out_shape=jax.ShapeDtypeStruct(tope.shape, jnp.int32),
        name="dest",
    )(pad_starts, tope, rank)


def _dispatch_kernel(pst_ref, pend_ref, dest_ref, x1p_ref, xs_hbm, zero_ref, sem, zsem,
                     *, slab):
    tb = dest_ref.shape[1]
    i = pl.program_id(0)
    blk = ROW_BLK * slab

    def zero_copy(b):
        return pltpu.make_async_copy(
            zero_ref, xs_hbm.at[pl.ds(pl.multiple_of(b * blk, blk), blk)], zsem)

    @pl.when(i == 0)
    def _():
        zero_ref[...] = jnp.zeros_like(zero_ref)

        def start(e, c):
            @pl.when(pend_ref[e] > pst_ref[e])
            def _():
                zero_copy(pend_ref[e] // ROW_BLK - 1).start()
            return c

        def wait(e, c):
            @pl.when(pend_ref[e] > pst_ref[e])
            def _():
                zero_copy(pend_ref[e] // ROW_BLK - 1).wait()
            return c

        def start_tail(b, c):
            zero_copy(b).start()
            return c

        def wait_tail(b, c):
            zero_copy(b).wait()
            return c

        used = pend_ref[N_EXPERTS - 1] // ROW_BLK
        total = xs_hbm.shape[0] // blk
        lax.fori_loop(0, N_EXPERTS, start, 0)
        lax.fori_loop(used, total, start_tail, 0)
        lax.fori_loop(0, N_EXPERTS, wait, 0)
        lax.fori_loop(used, total, wait_tail, 0)

    def start(t, c):
        src = x1p_ref.at[pl.ds(pl.multiple_of(t * slab, slab), slab)]
        for k in range(TOP_K):
            dst = dest_ref[k, t]
            pltpu.make_async_copy(
                src, xs_hbm.at[pl.ds(pl.multiple_of(dst * slab, slab), slab)], sem).start()
        return c

    lax.fori_loop(0, tb, start, 0)
    for k in range(TOP_K):
        pltpu.make_async_copy(x1p_ref, xs_hbm.at[pl.ds(0, tb * slab)], sem).wait()


def _dispatch(pad_starts, pad_ends, dest, x1p, n, cap):
    slab = x1p.shape[0] // n
    tb = DISP_TB
    smem_blk = pl.BlockSpec((TOP_K, tb), lambda i, *_: (0, i), memory_space=pltpu.SMEM)
    return pl.pallas_call(
        functools.partial(_dispatch_kernel, slab=slab),
        grid_spec=pltpu.PrefetchScalarGridSpec(
            num_scalar_prefetch=2,
            grid=(n // tb,),
            in_specs=[smem_blk,
                      pl.BlockSpec((tb * slab, LANES), lambda i, *_: (i, 0))],
            out_specs=pl.BlockSpec(memory_space=pl.ANY),
            scratch_shapes=[pltpu.VMEM((ROW_BLK * slab, LANES), U32),
                            pltpu.SemaphoreType.DMA(()),
                            pltpu.SemaphoreType.DMA(())],
        ),
        out_shape=jax.ShapeDtypeStruct((cap * slab, LANES), U32),
        compiler_params=pltpu.CompilerParams(
            dimension_semantics=("arbitrary",),
            has_side_effects=True),
        name="dispatch",
    )(pad_starts, pad_ends, dest, x1p)


def _ffn_kernel(ie_ref, ist_ref, inb_ref, used_ref,
                xs_hbm, wg_ref, wu_ref, bg_ref, bu_ref, wd_ref, bd_ref,
                y_hbm,
                slabs, xbuf, yacc, sem_in, sem_out, *, slab):
    w = pl.program_id(0)
    f = pl.program_id(1)
    n_items = pl.num_programs(0)
    nf = pl.num_programs(1)
    nb = inb_ref[w]
    blk = ROW_BLK * slab
    half = slab * LANES
    slot_rows = slabs.shape[0] // 2
    first_step = f == 0
    last_step = f == nf - 1
    prev_item = jnp.maximum(w - 1, 0)
    next_item = jnp.minimum(w + 1, n_items - 1)

    def rows(r):
        return pl.ds(pl.multiple_of(r * ROW_BLK, ROW_BLK), ROW_BLK)

    def slot_row0(item):
        return (item % 2) * (slot_rows // slab)

    def hbm_blk(ref, item, r):
        return ref.at[pl.ds(pl.multiple_of((ist_ref[item] + r) * blk, blk), blk)]

    def vmem_blk(item, r):
        return slabs.at[pl.ds(pl.multiple_of((item % 2) * slot_rows + r * blk, blk), blk)]

    def for_blocks(item, fn):
        def body(r, c):
            fn(r)
            return c
        lax.fori_loop(0, inb_ref[item], body, 0)

    def load(item):
        return lambda r: pltpu.make_async_copy(hbm_blk(xs_hbm, item, r), vmem_blk(item, r), sem_in)

    def store(item):
        return lambda r: pltpu.make_async_copy(
            vmem_blk(item, r), hbm_blk(y_hbm, item, r), sem_out.at[item % 2])

    @pl.when((w == 0) & first_step)
    def _():
        slabs[0:blk, :] = jnp.zeros((blk, LANES), U32)

        def tail_copy(b):
            return pltpu.make_async_copy(
                slabs.at[pl.ds(0, blk)], y_hbm.at[pl.ds(pl.multiple_of(b * blk, blk), blk)],
                sem_out.at[0])

        def start(b, c):
            tail_copy(b).start()
            return c

        def wait(b, c):
            tail_copy(b).wait()
            return c

        total = y_hbm.shape[0] // blk
        lax.fori_loop(used_ref[0], total, start, 0)
        lax.fori_loop(used_ref[0], total, wait, 0)
        for_blocks(w, lambda r: load(w)(r).start())

    @pl.when(last_step)
    def _():
        @pl.when(w >= 1)
        def _():
            for_blocks(prev_item, lambda r: store(prev_item)(r).wait())

        @pl.when(w + 1 < n_items)
        def _():
            for_blocks(next_item, lambda r: load(next_item)(r).start())

    @pl.when(nb > 0)
    def _():
        @pl.when(first_step)
        def _():
            for_blocks(w, lambda r: load(w)(r).wait())

            def unpack(r):
                for col in range(slab):
                    hi, lo = _load_slab_column(slabs, slot_row0(w) + r * ROW_BLK, ROW_BLK, slab, col)
                    xbuf[rows(r), col * LANES:(col + 1) * LANES] = hi.astype(BF16)
                    xbuf[rows(r), half + col * LANES:half + (col + 1) * LANES] = lo.astype(BF16)
                yacc[rows(r), :] = jnp.broadcast_to(bd_ref[...], (ROW_BLK, yacc.shape[1]))

            for_blocks(w, unpack)

        def sub(row0, n_rows):
            rs = pl.ds(pl.multiple_of(row0, ROW_BLK), n_rows)
            xt = xbuf[rs, :]
            gate = jnp.minimum(_dot(xt, wg_ref[...]) + bg_ref[...], SWIGLU_LIMIT)
            up = jnp.clip(_dot(xt, wu_ref[...]) + bu_ref[...], -SWIGLU_LIMIT, SWIGLU_LIMIT)
            act = (up + 1.0) * (gate * jax.nn.sigmoid(SWIGLU_ALPHA * gate))
            yacc[rs, :] += _dot(act.astype(BF16), wd_ref[...])

        def quad(q, c):
            sub(q * (4 * ROW_BLK), 4 * ROW_BLK)
            return c

        lax.fori_loop(0, nb // 4, quad, 0)

        @pl.when(nb % 4 >= 2)
        def _():
            sub((nb // 4) * (4 * ROW_BLK), 2 * ROW_BLK)

        @pl.when(nb % 2 == 1)
        def _():
            sub((nb - 1) * ROW_BLK, ROW_BLK)

        @pl.when(last_step)
        def _():
            def pack_and_store(r):
                _store_slabs(slabs, slot_row0(w) + r * ROW_BLK, _pack_pairs(yacc[rows(r), :]))
                store(w)(r).start()

            for_blocks(w, pack_and_store)

    @pl.when(last_step & (w == n_items - 1))
    def _():
        for_blocks(w, lambda r: store(w)(r).wait())


def _ffn(item_e, item_st, item_nb, used_blocks, xs, w_gu, b_gu, w_down, b_down, cap):
    slab = xs.shape[0] // cap
    n_e, d, f2 = w_gu.shape
    dff = f2 // 2
    tf = FFN_TF
    nf = dff // tf
    assert nf >= 2, "the ffn kernel prefetches the next item during a step that is not the first"
    n_items = item_e.shape[0]
    ts = ITEM_BLKS * ROW_BLK

    def fsel(w, f, inb):
        return jnp.where(inb[w] > 0, f, nf - 1)

    wg_spec = pl.BlockSpec((None, d, tf), lambda w, f, ie, ist, inb, u: (ie[w], 0, fsel(w, f, inb)))
    wu_spec = pl.BlockSpec((None, d, tf), lambda w, f, ie, ist, inb, u: (ie[w], 0, nf + fsel(w, f, inb)))
    bg_spec = pl.BlockSpec((None, 1, tf), lambda w, f, ie, ist, inb, u: (ie[w], 0, fsel(w, f, inb)))
    bu_spec = pl.BlockSpec((None, 1, tf), lambda w, f, ie, ist, inb, u: (ie[w], 0, nf + fsel(w, f, inb)))
    wd_spec = pl.BlockSpec((None, tf, d), lambda w, f, ie, ist, inb, u: (ie[w], fsel(w, f, inb), 0))
    bd_spec = pl.BlockSpec((None, 1, d), lambda w, f, ie, ist, inb, u: (ie[w], 0, 0))
    return pl.pallas_call(
        functools.partial(_ffn_kernel, slab=slab),
        grid_spec=pltpu.PrefetchScalarGridSpec(
            num_scalar_prefetch=4,
            grid=(n_items, nf),
            in_specs=[pl.BlockSpec(memory_space=pl.ANY),
                      wg_spec, wu_spec, bg_spec, bu_spec, wd_spec, bd_spec],
            out_specs=pl.BlockSpec(memory_space=pl.ANY),
            scratch_shapes=[pltpu.VMEM((2 * ts * slab, LANES), U32),
                            pltpu.VMEM((ts, d), BF16),
                            pltpu.VMEM((ts, d), F32),
                            pltpu.SemaphoreType.DMA(()),
                            pltpu.SemaphoreType.DMA((2,))],
        ),
        out_shape=jax.ShapeDtypeStruct((cap * slab, LANES), U32),
        compiler_params=pltpu.CompilerParams(
            dimension_semantics=("arbitrary", "arbitrary"),
            vmem_limit_bytes=VMEM_LIMIT_BYTES,
            has_side_effects=True),
        name="ffn",
    )(item_e, item_st, item_nb, used_blocks, xs, w_gu, w_gu, b_gu.reshape(n_e, 1, f2), b_gu.reshape(n_e, 1, f2),
      w_down, b_down.reshape(n_e, 1, d))


def _final_kernel(dest_ref, gate_ref, x1_ref, p_ref, y_hbm,
                  g2_ref, b2_ref, wpg_ref, bpg_ref, wple_ref, g3_ref, b3_ref,
                  o_ref, ybuf, ffn_ref, gbuf, sem, *, alpha, slab):
    tm = x1_ref.shape[0]
    half = slab * LANES

    def start(t, c):
        for k in range(TOP_K):
            src = dest_ref[k, t]
            pltpu.make_async_copy(
                y_hbm.at[pl.ds(pl.multiple_of(src * slab, slab), slab)],
                ybuf.at[pl.ds(pl.multiple_of((k * tm + t) * slab, slab), slab)], sem).start()
        return c

    lax.fori_loop(0, tm, start, 0)
    pltpu.make_async_copy(y_hbm.at[pl.ds(0, ybuf.shape[0])], ybuf, sem).wait()

    for k in range(TOP_K):
        gbuf[k] = jnp.broadcast_to(gate_ref[:, k:k + 1], (tm, LANES))
    for col in range(slab):
        acc_hi = acc_lo = None
        for k in range(TOP_K):
            hi, lo = _load_slab_column(ybuf, k * tm, tm, slab, col)
            g = gbuf[k]
            acc_hi = hi * g if acc_hi is None else acc_hi + hi * g
            acc_lo = lo * g if acc_lo is None else acc_lo + lo * g
        ffn_ref[:, col * LANES:(col + 1) * LANES] = acc_hi
        ffn_ref[:, half + col * LANES:half + (col + 1) * LANES] = acc_lo

    h2 = _layer_norm(alpha * x1_ref[...] + ffn_ref[...], g2_ref[...], b2_ref[...])
    pg = jax.nn.sigmoid(_dot(h2.astype(BF16), wpg_ref[...]) + bpg_ref[...])
    pe = _dot(p_ref[...].astype(BF16), wple_ref[...])
    o_ref[...] = _layer_norm(alpha * h2 + pg * pe, g3_ref[...], b3_ref[...])


def _final(dest, gates_t, x1, p2, y, g2, b2, wpg, bpg, wple, g3, b3, cap, *, alpha):
    n, d = x1.shape
    slab = y.shape[0] // cap
    tm = MIX_TM
    smem_blk = pl.BlockSpec((TOP_K, tm), lambda i: (0, i), memory_space=pltpu.SMEM)
    const = lambda i: (0, 0)
    row_blk = lambda width: pl.BlockSpec((tm, width), lambda i: (i, 0))
    return pl.pallas_call(
        functools.partial(_final_kernel, alpha=alpha, slab=slab),
        grid_spec=pltpu.PrefetchScalarGridSpec(
            num_scalar_prefetch=0,
            grid=(n // tm,),
            in_specs=[smem_blk, row_blk(TOP_K), row_blk(d), row_blk(p2.shape[1]),
                      pl.BlockSpec(memory_space=pl.ANY),
                      pl.BlockSpec(g2.shape, const), pl.BlockSpec(b2.shape, const),
                      pl.BlockSpec(wpg.shape, const), pl.BlockSpec(bpg.shape, const),
                      pl.BlockSpec(wple.shape, const),
                      pl.BlockSpec(g3.shape, const), pl.BlockSpec(b3.shape, const)],
            out_specs=row_blk(d),
            scratch_shapes=[pltpu.VMEM((TOP_K * tm * slab, LANES), U32),
                            pltpu.VMEM((tm, d), F32),
                            pltpu.VMEM((TOP_K, tm, LANES), F32),
                            pltpu.SemaphoreType.DMA(())],
        ),
        out_shape=jax.ShapeDtypeStruct((n, d), F32),
        compiler_params=pltpu.CompilerParams(
            dimension_semantics=("arbitrary",),
            vmem_limit_bytes=VMEM_LIMIT_BYTES),
        name="final",
    )(dest, gates_t, x1, p2, y, g2, b2, wpg, bpg, wple, g3, b3)


def _plan(counts, n_items):
    nblk = (counts + ROW_BLK - 1) // ROW_BLK
    blk_end = jnp.cumsum(nblk)
    blk_start = blk_end - nblk
    items_per = (nblk + ITEM_BLKS - 1) // ITEM_BLKS
    item_end = jnp.cumsum(items_per)
    item_start = item_end - items_per
    total = item_end[-1]
    w = jnp.arange(n_items, dtype=jnp.int32)
    w_eff = jnp.minimum(w, total - 1)
    e = jnp.minimum(jnp.searchsorted(item_end, w_eff, side="right"), N_EXPERTS - 1).astype(jnp.int32)
    j = w_eff - item_start[e]
    st = blk_start[e] + j * ITEM_BLKS
    nb = jnp.where(w < total, jnp.clip(nblk[e] - j * ITEM_BLKS, 0, ITEM_BLKS), 0)
    return ((blk_start * ROW_BLK).astype(jnp.int32), (blk_end * ROW_BLK).astype(jnp.int32),
            e, st.astype(jnp.int32), nb.astype(jnp.int32))


def kernel(x, p, w_in, w_pool, pool_scale, conv_w, w_br_a, w_br_b, w_o, ln1_g, ln1_b,
           w_router, b_router, w_gu, b_gu, w_down, b_down, ln2_g, ln2_b,
           w_pg, b_pg, w_ple, ln3_g, ln3_b):
    depth = w_in.shape[0]
    bsz, seq, d = x.shape
    n = bsz * seq
    alpha = (2.0 * depth) ** 0.25
    nk = n * TOP_K
    n_row_blocks = (nk + N_EXPERTS * (ROW_BLK - 1) + ROW_BLK - 1) // ROW_BLK
    cap = n_row_blocks * ROW_BLK
    n_items = N_EXPERTS + n_row_blocks // ITEM_BLKS
    row = lambda v: v.reshape(1, -1)

    h = x.reshape(n, d)
    for i in range(depth):
        proj = _proj(h, w_in[i])
        x1, x1p, tope, gates, rank, cnt = _mixer(
            proj, h, w_pool[i].astype(BF16), row(pool_scale[i]), conv_w[i],
            w_br_a[i].astype(BF16), w_br_b[i].astype(BF16), w_o[i].astype(BF16),
            row(ln1_g[i]), row(ln1_b[i]), w_router[i].T.astype(BF16),
            b_router[i].reshape(N_EXPERTS, 1), seq=seq, alpha=alpha)
        pad_starts, pad_ends, item_e, item_st, item_nb = _plan(cnt[:, 0], n_items)
        dest = _dest(pad_starts, tope, rank)
        xs = _dispatch(pad_starts, pad_ends, dest, x1p, n, cap)
        used_blocks = pad_ends[N_EXPERTS - 1:] // ROW_BLK
        y = _ffn(item_e, item_st, item_nb, used_blocks, xs, w_gu[i], b_gu[i], w_down[i], b_down[i], cap)
        h = _final(dest, gates.T, x1, p[i].reshape(n, -1), y,
                   row(ln2_g[i]), row(ln2_b[i]), w_pg[i].astype(BF16), row(b_pg[i]),
                   w_ple[i].astype(BF16), row(ln3_g[i]), row(ln3_b[i]), cap, alpha=alpha)
    return h.reshape(bsz, seq, d)
```

```python
import functools

import jax
import jax.numpy as jnp
from jax import lax
from jax.experimental import pallas as pl
from jax.experimental.pallas import tpu as pltpu

F32 = jnp.float32
BF16 = jnp.bfloat16
U32 = jnp.uint32

POOL_WINDOWS = (2, 4, 8, 16)
N_POOL_GROUPS = 4
CONV_K = 3
N_EXPERTS = 32
TOP_K = 4
SWIGLU_LIMIT = 7.0
SWIGLU_ALPHA = 1.702
LN_EPS = 1e-5

LANES = 128
VMEM_LIMIT_BYTES = 56 * 1024 * 1024

HALO = 16
PROJ_BM = 1024
PROJ_BN = 1024
MIX_TM = 256
MIX_SPLIT = 2
FINAL_SPLIT = 2
DISP_TB = 512
ROW_BLK = 128
ITEM_BLKS = 10
FFN_TF = 512


def _layer_norm(v, g, b):
    mu = jnp.mean(v, axis=-1, keepdims=True)
    c = v - mu
    var = jnp.mean(c * c, axis=-1, keepdims=True)
    return c * lax.rsqrt(var + LN_EPS) * g + b


def _dot(a, b):
    return jnp.dot(a, b, preferred_element_type=F32)


def _pack_pairs(v):
    half = v.shape[1] // 2
    bits = pltpu.bitcast(v.astype(BF16).astype(F32), U32)
    return (bits[:, 0:half] & U32(0xFFFF0000)) | (bits[:, half:] >> 16)


def _store_slabs(ref, row0, packed):
    rows, width = packed.shape
    s = width // LANES
    for c in range(s):
        ref[pl.ds(row0 * s + c, rows, stride=s), :] = packed[:, c * LANES:(c + 1) * LANES]


def _load_slab_column(ref, row0, rows, s, c):
    p = ref[pl.ds(row0 * s + c, rows, stride=s), :]
    return pltpu.bitcast(p & U32(0xFFFF0000), F32), pltpu.bitcast(p << 16, F32)


def _proj_kernel(x_ref, w_ref, o_ref):
    o_ref[...] = _dot(x_ref[...].astype(BF16), w_ref[...].astype(BF16)).astype(o_ref.dtype)


def _proj(xb, wb):
    n, d = xb.shape
    n_in = wb.shape[1]
    return pl.pallas_call(
        _proj_kernel,
        grid=(n // PROJ_BM, n_in // PROJ_BN),
        in_specs=[pl.BlockSpec((PROJ_BM, d), lambda i, j: (i, 0)),
                  pl.BlockSpec((d, PROJ_BN), lambda i, j: (0, j))],
        out_specs=pl.BlockSpec((PROJ_BM, PROJ_BN), lambda i, j: (i, j)),
        out_shape=jax.ShapeDtypeStruct((n, n_in), BF16),
        compiler_params=pltpu.CompilerParams(
            dimension_semantics=("arbitrary", "arbitrary"),
            vmem_limit_bytes=VMEM_LIMIT_BYTES),
        name="proj",
    )(xb, wb)


def _mixer_kernel(proj_ref, halo_ref, x_ref, wpool_ref, pscale_ref, convw_ref,
                  wbra_ref, wbrb_ref, wo_ref, g1_ref, b1_ref, wrt_ref, br_ref,
                  x1_ref, x1p_ref, tope_ref, gate_ref, rank_ref, cnt_ref,
                  ext_ref, cext_ref, carry_ref, *, seq, alpha, pw, cw, d):
    tm = x_ref.shape[0]
    i = pl.program_id(0)
    blocks_per_seq = seq // tm
    j = i % blocks_per_seq
    keep_halo = (j > 0).astype(F32)
    row = lax.broadcasted_iota(jnp.int32, (tm, 1), 0)
    pos1 = (j * tm + row + 1).astype(F32)

    @pl.when(i == 0)
    def _():
        carry_ref[...] = jnp.zeros_like(carry_ref)

    o1, o2, o3 = pw + cw, pw + 2 * cw, pw + 3 * cw
    ext_ref[0:HALO, :] = halo_ref[:, 0:pw].astype(F32) * keep_halo
    ext_ref[HALO:HALO + tm, :] = proj_ref[:, 0:pw].astype(F32)
    cext_ref[0:HALO, :] = (halo_ref[:, o1:o2].astype(F32) * halo_ref[:, o2:o3].astype(F32)) * keep_halo
    cext_ref[HALO:HALO + tm, :] = proj_ref[:, o1:o2].astype(F32) * proj_ref[:, o2:o3].astype(F32)

    gw = pw // N_POOL_GROUPS
    hr = tm // MIX_SPLIT
    logit_parts = []
    for part in range(MIX_SPLIT):
        r0 = part * hr
        rs = slice(r0, r0 + hr)

        a_parts = []
        for g, w in enumerate(POOL_WINDOWS):
            cols = slice(g * gw, (g + 1) * gw)
            u = ext_ref[HALO + r0:HALO + r0 + hr, cols]
            s = u
            for sh in range(1, w):
                s = s + ext_ref[HALO + r0 - sh:HALO + r0 - sh + hr, cols]
            cnt = jnp.minimum(pos1[rs, :], float(w))
            pooled = s / cnt - u
            a_g = _dot(pooled.astype(BF16), wpool_ref[g]) * pscale_ref[:, cols]
            a_parts.append(a_g.astype(BF16))
        br_a = _dot(jnp.concatenate(a_parts, axis=1), wbra_ref[...])

        conv = cext_ref[HALO + r0:HALO + r0 + hr, :] * convw_ref[CONV_K - 1:CONV_K, :]
        for k in range(CONV_K - 1):
            sh = CONV_K - 1 - k
            conv = conv + cext_ref[HALO + r0 - sh:HALO + r0 - sh + hr, :] * convw_ref[k:k + 1, :]
        b = proj_ref[rs, pw:o1].astype(F32) * conv
        br_b = _dot(b.astype(BF16), wbrb_ref[...])

        g_a = proj_ref[rs, o3:o3 + d].astype(F32)
        g_b = proj_ref[rs, o3 + d:o3 + 2 * d].astype(F32)
        m = jax.nn.sigmoid(g_a) * br_a + jax.nn.sigmoid(g_b) * br_b
        mix = _dot(m.astype(BF16), wo_ref[...])
        x1 = _layer_norm(alpha * x_ref[rs, :] + mix, g1_ref[...], b1_ref[...])
        x1_ref[rs, :] = x1
        _store_slabs(x1p_ref, r0, _pack_pairs(x1))

        logit_parts.append(lax.dot_general(wrt_ref[...], x1.astype(BF16), (((1,), (1,)), ((), ())),
                                           preferred_element_type=F32))
    logits = jnp.concatenate(logit_parts, axis=1) + br_ref[...]
    e_iota = lax.broadcasted_iota(jnp.int32, logits.shape, 0)
    vals, idxs = [], []
    l = logits
    for _ in range(TOP_K):
        mx = jnp.max(l, axis=0, keepdims=True)
        ix = jnp.min(jnp.where(l == mx, e_iota, N_EXPERTS), axis=0, keepdims=True)
        vals.append(mx)
        idxs.append(ix)
        l = jnp.where(e_iota == ix, -jnp.inf, l)
    exps = [jnp.exp(v - vals[0]) for v in vals]
    denom = exps[0] + exps[1] + exps[2] + exps[3]
    onehot = jnp.zeros(logits.shape, F32)
    for k in range(TOP_K):
        tope_ref[k:k + 1, :] = idxs[k]
        gate_ref[k:k + 1, :] = exps[k] / denom
        onehot = onehot + (e_iota == idxs[k]).astype(F32)

    r_i = lax.broadcasted_iota(jnp.int32, (tm, tm), 0)
    c_i = lax.broadcasted_iota(jnp.int32, (tm, tm), 1)
    before = (r_i < c_i).astype(BF16)
    seen = _dot(onehot.astype(BF16), before) + carry_ref[:, 0:1]
    for k in range(TOP_K):
        rk = jnp.sum(jnp.where(e_iota == idxs[k], seen, 0.0), axis=0, keepdims=True)
        rank_ref[k:k + 1, :] = rk.astype(jnp.int32)
    carry_ref[...] = carry_ref[...] + jnp.sum(onehot, axis=1, keepdims=True)
    cnt_ref[...] = carry_ref[...].astype(jnp.int32)


def _mixer(proj, x2, wpool, pscale, convw, wbra, wbrb, wo, g1, b1, wrt, br, *, seq, alpha):
    n, d = x2.shape
    n_in = proj.shape[1]
    pw = wbra.shape[0]
    cw = wbrb.shape[0]
    tm = MIX_TM
    slab = d // 2 // LANES
    hb = tm // HALO
    const = lambda i: (0, 0)
    kern = functools.partial(_mixer_kernel, seq=seq, alpha=alpha, pw=pw, cw=cw, d=d)
    return pl.pallas_call(
        kern,
        grid=(n // tm,),
        in_specs=[
            pl.BlockSpec((tm, n_in), lambda i: (i, 0)),
            pl.BlockSpec((HALO, n_in // 2), lambda i: (jnp.maximum(i * hb - 1, 0), 0)),
            pl.BlockSpec((tm, d), lambda i: (i, 0)),
            pl.BlockSpec(wpool.shape, lambda i: (0, 0, 0)),
            pl.BlockSpec(pscale.shape, const),
            pl.BlockSpec(convw.shape, const),
            pl.BlockSpec(wbra.shape, const),
            pl.BlockSpec(wbrb.shape, const),
            pl.BlockSpec(wo.shape, const),
            pl.BlockSpec(g1.shape, const),
            pl.BlockSpec(b1.shape, const),
            pl.BlockSpec(wrt.shape, const),
            pl.BlockSpec(br.shape, const),
        ],
        out_specs=[
            pl.BlockSpec((tm, d), lambda i: (i, 0)),
            pl.BlockSpec((tm * slab, LANES), lambda i: (i, 0)),
            pl.BlockSpec((TOP_K, tm), lambda i: (0, i)),
            pl.BlockSpec((TOP_K, tm), lambda i: (0, i)),
            pl.BlockSpec((TOP_K, tm), lambda i: (0, i)),
            pl.BlockSpec((N_EXPERTS, LANES), const),
        ],
        out_shape=[
            jax.ShapeDtypeStruct((n, d), F32),
            jax.ShapeDtypeStruct((n * slab, LANES), U32),
            jax.ShapeDtypeStruct((TOP_K, n), jnp.int32),
            jax.ShapeDtypeStruct((TOP_K, n), F32),
            jax.ShapeDtypeStruct((TOP_K, n), jnp.int32),
            jax.ShapeDtypeStruct((N_EXPERTS, LANES), jnp.int32),
        ],
        scratch_shapes=[pltpu.VMEM((HALO + tm, pw), F32),
                        pltpu.VMEM((HALO + tm, cw), F32),
                        pltpu.VMEM((N_EXPERTS, LANES), F32)],
        compiler_params=pltpu.CompilerParams(
            dimension_semantics=("arbitrary",),
            vmem_limit_bytes=VMEM_LIMIT_BYTES),
        name="mixer",
    )(proj, proj, x2, wpool, pscale, convw, wbra, wbrb, wo, g1, b1, wrt, br)


def _dest_kernel(pst_ref, tope_ref, rank_ref, dest_ref):
    tope = tope_ref[...]
    dest = rank_ref[...]
    for e in range(N_EXPERTS):
        dest = dest + jnp.where(tope == e, pst_ref[e], 0)
    dest_ref[...] = dest


def _dest(pad_starts, tope, rank):
    whole = pl.BlockSpec(tope.shape, lambda i, *_: (0, 0))
    return pl.pallas_call(
        _dest_kernel,
        grid_spec=pltpu.PrefetchScalarGridSpec(
            num_scalar_prefetch=1, grid=(1,), in_specs=[whole, whole], out_specs=whole),
        out_shape=jax.ShapeDtypeStruct(tope.shape, jnp.int32),
        name="dest",
    )(pad_starts, tope, rank)


def _dispatch_kernel(pst_ref, pend_ref, dest_ref, x1p_hbm, xs_hbm, xbuf, zero_ref, ld_sem, sem, zsem,
                     *, slab, n_steps):
    tb = dest_ref.shape[1]
    i = pl.program_id(0)
    blk = ROW_BLK * slab
    chunk = tb * slab

    def load(c):
        row0 = c * chunk if isinstance(c, int) else pl.multiple_of(c * chunk, chunk)
        rows = pl.ds(row0, chunk)
        return pltpu.make_async_copy(x1p_hbm.at[rows], xbuf.at[rows], ld_sem.at[c])

    def zero_copy(b):
        return pltpu.make_async_copy(
            zero_ref, xs_hbm.at[pl.ds(pl.multiple_of(b * blk, blk), blk)], zsem)

    @pl.when(i == 0)
    def _():
        zero_ref[...] = jnp.zeros_like(zero_ref)

        def start(e, c):
            @pl.when(pend_ref[e] > pst_ref[e])
            def _():
                zero_copy(pend_ref[e] // ROW_BLK - 1).start()
            return c

        def wait(e, c):
            @pl.when(pend_ref[e] > pst_ref[e])
            def _():
                zero_copy(pend_ref[e] // ROW_BLK - 1).wait()
            return c

        def start_tail(b, c):
            zero_copy(b).start()
            return c

        def wait_tail(b, c):
            zero_copy(b).wait()
            return c

        used = pend_ref[N_EXPERTS - 1] // ROW_BLK
        total = xs_hbm.shape[0] // blk
        lax.fori_loop(0, N_EXPERTS, start, 0)
        lax.fori_loop(used, total, start_tail, 0)
        for c in range(n_steps):
            load(c).start()
        lax.fori_loop(0, N_EXPERTS, wait, 0)
        lax.fori_loop(used, total, wait_tail, 0)

    load(i).wait()

    def start(t, c):
        src = xbuf.at[pl.ds(pl.multiple_of((i * tb + t) * slab, slab), slab)]
        for k in range(TOP_K):
            dst = dest_ref[k, t]
            pltpu.make_async_copy(
                src, xs_hbm.at[pl.ds(pl.multiple_of(dst * slab, slab), slab)], sem).start()
        return c

    lax.fori_loop(0, tb, start, 0)

    def wait_one_step():
        for k in range(TOP_K):
            pltpu.make_async_copy(
                xbuf.at[pl.ds(0, chunk)], xs_hbm.at[pl.ds(0, chunk)], sem).wait()

    @pl.when(i > 0)
    def _():
        wait_one_step()

    @pl.when(i == n_steps - 1)
    def _():
        wait_one_step()


def _dispatch(pad_starts, pad_ends, dest, x1p, n, cap):
    slab = x1p.shape[0] // n
    tb = DISP_TB
    n_steps = n // tb
    smem_blk = pl.BlockSpec((TOP_K, tb), lambda i, *_: (0, i), memory_space=pltpu.SMEM)
    return pl.pallas_call(
        functools.partial(_dispatch_kernel, slab=slab, n_steps=n_steps),
        grid_spec=pltpu.PrefetchScalarGridSpec(
            num_scalar_prefetch=2,
            grid=(n_steps,),
            in_specs=[smem_blk, pl.BlockSpec(memory_space=pl.ANY)],
            out_specs=pl.BlockSpec(memory_space=pl.ANY),
            scratch_shapes=[pltpu.VMEM(x1p.shape, U32),
                            pltpu.VMEM((ROW_BLK * slab, LANES), U32),
                            pltpu.SemaphoreType.DMA((n_steps,)),
                            pltpu.SemaphoreType.DMA(()),
                            pltpu.SemaphoreType.DMA(())],
        ),
        out_shape=jax.ShapeDtypeStruct((cap * slab, LANES), U32),
        compiler_params=pltpu.CompilerParams(
            dimension_semantics=("arbitrary",),
            vmem_limit_bytes=VMEM_LIMIT_BYTES,
            has_side_effects=True),
        name="dispatch",
    )(pad_starts, pad_ends, dest, x1p)


def _ffn_kernel(ie_ref, ist_ref, inb_ref, used_ref,
                xs_hbm, wg_ref, wu_ref, bg_ref, bu_ref, wd_ref, bd_ref,
                y_hbm,
                slabs, xbuf, yacc, sem_in, sem_out, *, slab):
    w = pl.program_id(0)
    f = pl.program_id(1)
    n_items = pl.num_programs(0)
    nf = pl.num_programs(1)
    nb = inb_ref[w]
    blk = ROW_BLK * slab
    half = slab * LANES
    slot_rows = slabs.shape[0] // 2
    first_step = f == 0
    last_step = f == nf - 1
    prev_item = jnp.maximum(w - 1, 0)
    next_item = jnp.minimum(w + 1, n_items - 1)

    def rows(r):
        return pl.ds(pl.multiple_of(r * ROW_BLK, ROW_BLK), ROW_BLK)

    def slot_row0(item):
        return (item % 2) * (slot_rows // slab)

    def hbm_blk(ref, item, r):
        return ref.at[pl.ds(pl.multiple_of((ist_ref[item] + r) * blk, blk), blk)]

    def vmem_blk(item, r):
        return slabs.at[pl.ds(pl.multiple_of((item % 2) * slot_rows + r * blk, blk), blk)]

    def for_blocks(item, fn):
        def body(r, c):
            fn(r)
            return c
        lax.fori_loop(0, inb_ref[item], body, 0)

    def load(item):
        return lambda r: pltpu.make_async_copy(hbm_blk(xs_hbm, item, r), vmem_blk(item, r), sem_in)

    def store(item):
        return lambda r: pltpu.make_async_copy(
            vmem_blk(item, r), hbm_blk(y_hbm, item, r), sem_out.at[item % 2])

    @pl.when((w == 0) & first_step)
    def _():
        slabs[0:blk, :] = jnp.zeros((blk, LANES), U32)

        def tail_copy(b):
            return pltpu.make_async_copy(
                slabs.at[pl.ds(0, blk)], y_hbm.at[pl.ds(pl.multiple_of(b * blk, blk), blk)],
                sem_out.at[0])

        def start(b, c):
            tail_copy(b).start()
            return c

        def wait(b, c):
            tail_copy(b).wait()
            return c

        total = y_hbm.shape[0] // blk
        lax.fori_loop(used_ref[0], total, start, 0)
        lax.fori_loop(used_ref[0], total, wait, 0)
        for_blocks(w, lambda r: load(w)(r).start())

    @pl.when(last_step)
    def _():
        @pl.when(w >= 1)
        def _():
            for_blocks(prev_item, lambda r: store(prev_item)(r).wait())

        @pl.when(w + 1 < n_items)
        def _():
            for_blocks(next_item, lambda r: load(next_item)(r).start())

    @pl.when(nb > 0)
    def _():
        @pl.when(first_step)
        def _():
            for_blocks(w, lambda r: load(w)(r).wait())

            def unpack(r):
                for col in range(slab):
                    hi, lo = _load_slab_column(slabs, slot_row0(w) + r * ROW_BLK, ROW_BLK, slab, col)
                    xbuf[rows(r), col * LANES:(col + 1) * LANES] = hi.astype(BF16)
                    xbuf[rows(r), half + col * LANES:half + (col + 1) * LANES] = lo.astype(BF16)
                yacc[rows(r), :] = jnp.broadcast_to(bd_ref[...], (ROW_BLK, yacc.shape[1]))

            for_blocks(w, unpack)

        def sub(row0, n_rows):
            rs = pl.ds(pl.multiple_of(row0, ROW_BLK), n_rows)
            xt = xbuf[rs, :]
            gate = jnp.minimum(_dot(xt, wg_ref[...]) + bg_ref[...], SWIGLU_LIMIT)
            up = jnp.clip(_dot(xt, wu_ref[...]) + bu_ref[...], -SWIGLU_LIMIT, SWIGLU_LIMIT)
            act = (up + 1.0) * (gate * jax.nn.sigmoid(SWIGLU_ALPHA * gate))
            yacc[rs, :] += _dot(act.astype(BF16), wd_ref[...])

        def quad(q, c):
            sub(q * (4 * ROW_BLK), 4 * ROW_BLK)
            return c

        lax.fori_loop(0, nb // 4, quad, 0)

        @pl.when(nb % 4 >= 2)
        def _():
            sub((nb // 4) * (4 * ROW_BLK), 2 * ROW_BLK)

        @pl.when(nb % 2 == 1)
        def _():
            sub((nb - 1) * ROW_BLK, ROW_BLK)

        @pl.when(last_step)
        def _():
            def pack_and_store(r):
                _store_slabs(slabs, slot_row0(w) + r * ROW_BLK, _pack_pairs(yacc[rows(r), :]))
                store(w)(r).start()

            for_blocks(w, pack_and_store)

    @pl.when(last_step & (w == n_items - 1))
    def _():
        for_blocks(w, lambda r: store(w)(r).wait())


def _ffn(item_e, item_st, item_nb, used_blocks, xs, w_gu, b_gu, w_down, b_down, cap):
    slab = xs.shape[0] // cap
    n_e, d, f2 = w_gu.shape
    dff = f2 // 2
    tf = FFN_TF
    nf = dff // tf
    assert nf >= 2, "the ffn kernel prefetches the next item during a step that is not the first"
    n_items = item_e.shape[0]
    ts = ITEM_BLKS * ROW_BLK

    def fsel(w, f, inb):
        return jnp.where(inb[w] > 0, f, nf - 1)

    wg_spec = pl.BlockSpec((None, d, tf), lambda w, f, ie, ist, inb, u: (ie[w], 0, fsel(w, f, inb)))
    wu_spec = pl.BlockSpec((None, d, tf), lambda w, f, ie, ist, inb, u: (ie[w], 0, nf + fsel(w, f, inb)))
    bg_spec = pl.BlockSpec((None, 1, tf), lambda w, f, ie, ist, inb, u: (ie[w], 0, fsel(w, f, inb)))
    bu_spec = pl.BlockSpec((None, 1, tf), lambda w, f, ie, ist, inb, u: (ie[w], 0, nf + fsel(w, f, inb)))
    wd_spec = pl.BlockSpec((None, tf, d), lambda w, f, ie, ist, inb, u: (ie[w], fsel(w, f, inb), 0))
    bd_spec = pl.BlockSpec((None, 1, d), lambda w, f, ie, ist, inb, u: (ie[w], 0, 0))
    return pl.pallas_call(
        functools.partial(_ffn_kernel, slab=slab),
        grid_spec=pltpu.PrefetchScalarGridSpec(
            num_scalar_prefetch=4,
            grid=(n_items, nf),
            in_specs=[pl.BlockSpec(memory_space=pl.ANY),
                      wg_spec, wu_spec, bg_spec, bu_spec, wd_spec, bd_spec],
            out_specs=pl.BlockSpec(memory_space=pl.ANY),
            scratch_shapes=[pltpu.VMEM((2 * ts * slab, LANES), U32),
                            pltpu.VMEM((ts, d), BF16),
                            pltpu.VMEM((ts, d), F32),
                            pltpu.SemaphoreType.DMA(()),
                            pltpu.SemaphoreType.DMA((2,))],
        ),
        out_shape=jax.ShapeDtypeStruct((cap * slab, LANES), U32),
        compiler_params=pltpu.CompilerParams(
            dimension_semantics=("arbitrary", "arbitrary"),
            vmem_limit_bytes=VMEM_LIMIT_BYTES,
            has_side_effects=True),
        name="ffn",
    )(item_e, item_st, item_nb, used_blocks, xs, w_gu, w_gu, b_gu.reshape(n_e, 1, f2), b_gu.reshape(n_e, 1, f2),
      w_down, b_down.reshape(n_e, 1, d))


def _final_kernel(dcur_ref, dnxt_ref, gate_ref, x1_ref, p_ref, y_hbm,
                  g2_ref, b2_ref, wpg_ref, bpg_ref, wple_ref, g3_ref, b3_ref,
                  o_ref, ybuf, ffn_ref, gbuf, sem, *, alpha, slab):
    tm = x1_ref.shape[0]
    half = slab * LANES
    i = pl.program_id(0)
    n_steps = pl.num_programs(0)
    slot = i % 2
    slot_tokens = TOP_K * tm
    slot_rows = slot_tokens * slab

    def gather(d_ref, into):
        def start(t, c):
            for k in range(TOP_K):
                src = d_ref[k, t]
                dst = (into * slot_tokens + k * tm + t) * slab
                pltpu.make_async_copy(
                    y_hbm.at[pl.ds(pl.multiple_of(src * slab, slab), slab)],
                    ybuf.at[pl.ds(pl.multiple_of(dst, slab), slab)], sem.at[into]).start()
            return c
        lax.fori_loop(0, tm, start, 0)

    @pl.when(i == 0)
    def _():
        gather(dcur_ref, slot)

    @pl.when(i + 1 < n_steps)
    def _():
        gather(dnxt_ref, 1 - slot)

    pltpu.make_async_copy(
        y_hbm.at[pl.ds(0, slot_rows)],
        ybuf.at[pl.ds(pl.multiple_of(slot * slot_rows, slot_rows), slot_rows)], sem.at[slot]).wait()

    for k in range(TOP_K):
        gbuf[k] = jnp.broadcast_to(gate_ref[:, k:k + 1], (tm, LANES))

    hr = tm // FINAL_SPLIT
    for part in range(FINAL_SPLIT):
        rs = slice(part * hr, (part + 1) * hr)
        for col in range(slab):
            acc_hi = acc_lo = None
            for k in range(TOP_K):
                hi, lo = _load_slab_column(
                    ybuf, slot * slot_tokens + k * tm + part * hr, hr, slab, col)
                g = gbuf[k, rs, :]
                acc_hi = hi * g if acc_hi is None else acc_hi + hi * g
                acc_lo = lo * g if acc_lo is None else acc_lo + lo * g
            ffn_ref[rs, col * LANES:(col + 1) * LANES] = acc_hi
            ffn_ref[rs, half + col * LANES:half + (col + 1) * LANES] = acc_lo
        h2 = _layer_norm(alpha * x1_ref[rs, :] + ffn_ref[rs, :], g2_ref[...], b2_ref[...])
        pg = jax.nn.sigmoid(_dot(h2.astype(BF16), wpg_ref[...]) + bpg_ref[...])
        pe = _dot(p_ref[rs, :].astype(BF16), wple_ref[...])
        o_ref[rs, :] = _layer_norm(alpha * h2 + pg * pe, g3_ref[...], b3_ref[...])


def _final(dest, gates_t, x1, p2, y, g2, b2, wpg, bpg, wple, g3, b3, cap, *, alpha):
    n, d = x1.shape
    slab = y.shape[0] // cap
    tm = MIX_TM
    n_steps = n // tm
    dest_cur = pl.BlockSpec((TOP_K, tm), lambda i: (0, i), memory_space=pltpu.SMEM)
    dest_next = pl.BlockSpec((TOP_K, tm), lambda i: (0, jnp.minimum(i + 1, n_steps - 1)),
                             memory_space=pltpu.SMEM)
    const = lambda i: (0, 0)
    row_blk = lambda width: pl.BlockSpec((tm, width), lambda i: (i, 0))
    return pl.pallas_call(
        functools.partial(_final_kernel, alpha=alpha, slab=slab),
        grid_spec=pltpu.PrefetchScalarGridSpec(
            num_scalar_prefetch=0,
            grid=(n_steps,),
            in_specs=[dest_cur, dest_next, row_blk(TOP_K), row_blk(d), row_blk(p2.shape[1]),
                      pl.BlockSpec(memory_space=pl.ANY),
                      pl.BlockSpec(g2.shape, const), pl.BlockSpec(b2.shape, const),
                      pl.BlockSpec(wpg.shape, const), pl.BlockSpec(bpg.shape, const),
                      pl.BlockSpec(wple.shape, const),
                      pl.BlockSpec(g3.shape, const), pl.BlockSpec(b3.shape, const)],
            out_specs=row_blk(d),
            scratch_shapes=[pltpu.VMEM((2 * TOP_K * tm * slab, LANES), U32),
                            pltpu.VMEM((tm, d), F32),
                            pltpu.VMEM((TOP_K, tm, LANES), F32),
                            pltpu.SemaphoreType.DMA((2,))],
        ),
        out_shape=jax.ShapeDtypeStruct((n, d), F32),
        compiler_params=pltpu.CompilerParams(
            dimension_semantics=("arbitrary",),
            vmem_limit_bytes=VMEM_LIMIT_BYTES),
        name="final",
    )(dest, dest, gates_t, x1, p2, y, g2, b2, wpg, bpg, wple, g3, b3)


def _plan(counts, n_items):
    nblk = (counts + ROW_BLK - 1) // ROW_BLK
    blk_end = jnp.cumsum(nblk)
    blk_start = blk_end - nblk
    items_per = (nblk + ITEM_BLKS - 1) // ITEM_BLKS
    item_end = jnp.cumsum(items_per)
    item_start = item_end - items_per
    total = item_end[-1]
    w = jnp.arange(n_items, dtype=jnp.int32)
    w_eff = jnp.minimum(w, total - 1)
    e = jnp.minimum(jnp.searchsorted(item_end, w_eff, side="right"), N_EXPERTS - 1).astype(jnp.int32)
    j = w_eff - item_start[e]
    st = blk_start[e] + j * ITEM_BLKS
    nb = jnp.where(w < total, jnp.clip(nblk[e] - j * ITEM_BLKS, 0, ITEM_BLKS), 0)
    return ((blk_start * ROW_BLK).astype(jnp.int32), (blk_end * ROW_BLK).astype(jnp.int32),
            e, st.astype(jnp.int32), nb.astype(jnp.int32))


def kernel(x, p, w_in, w_pool, pool_scale, conv_w, w_br_a, w_br_b, w_o, ln1_g, ln1_b,
           w_router, b_router, w_gu, b_gu, w_down, b_down, ln2_g, ln2_b,
           w_pg, b_pg, w_ple, ln3_g, ln3_b):
    depth = w_in.shape[0]
    bsz, seq, d = x.shape
    n = bsz * seq
    alpha = (2.0 * depth) ** 0.25
    nk = n * TOP_K
    n_row_blocks = (nk + N_EXPERTS * (ROW_BLK - 1) + ROW_BLK - 1) // ROW_BLK
    cap = n_row_blocks * ROW_BLK
    n_items = N_EXPERTS + n_row_blocks // ITEM_BLKS
    row = lambda v: v.reshape(1, -1)

    h = x.reshape(n, d)
    for i in range(depth):
        proj = _proj(h, w_in[i])
        x1, x1p, tope, gates, rank, cnt = _mixer(
            proj, h, w_pool[i].astype(BF16), row(pool_scale[i]), conv_w[i],
            w_br_a[i].astype(BF16), w_br_b[i].astype(BF16), w_o[i].astype(BF16),
            row(ln1_g[i]), row(ln1_b[i]), w_router[i].T.astype(BF16),
            b_router[i].reshape(N_EXPERTS, 1), seq=seq, alpha=alpha)
        pad_starts, pad_ends, item_e, item_st, item_nb = _plan(cnt[:, 0], n_items)
        dest = _dest(pad_starts, tope, rank)
        xs = _dispatch(pad_starts, pad_ends, dest, x1p, n, cap)
        used_blocks = pad_ends[N_EXPERTS - 1:] // ROW_BLK
        y = _ffn(item_e, item_st, item_nb, used_blocks, xs, w_gu[i], b_gu[i], w_down[i], b_down[i], cap)
        h = _final(dest, gates.T, x1, p[i].reshape(n, -1), y,
                   row(ln2_g[i]), row(ln2_b[i]), w_pg[i].astype(BF16), row(b_pg[i]),
                   w_ple[i].astype(BF16), row(ln3_g[i]), row(ln3_b[i]), cap, alpha=alpha)
    return h.reshape(bsz, seq, d)
```

```python
import functools

import jax
import jax.numpy as jnp
from jax import lax
from jax.experimental import pallas as pl
from jax.experimental.pallas import tpu as pltpu

F32 = jnp.float32
BF16 = jnp.bfloat16
U32 = jnp.uint32

POOL_WINDOWS = (2, 4, 8, 16)
N_POOL_GROUPS = 4
CONV_K = 3
N_EXPERTS = 32
TOP_K = 4
SWIGLU_LIMIT = 7.0
SWIGLU_ALPHA = 1.702
LN_EPS = 1e-5

LANES = 128
VMEM_LIMIT_BYTES = 56 * 1024 * 1024

HALO = 16
PROJ_BM = 1024
PROJ_BN = 1024
MIX_TM = 256
MIX_SPLIT = 1
FINAL_SPLIT = 2
DISP_TB = 512
ROW_BLK = 128
ITEM_BLKS = 10
FFN_TF = 512


def _layer_norm(v, g, b):
    mu = jnp.mean(v, axis=-1, keepdims=True)
    c = v - mu
    var = jnp.mean(c * c, axis=-1, keepdims=True)
    return c * lax.rsqrt(var + LN_EPS) * g + b


def _dot(a, b):
    return jnp.dot(a, b, preferred_element_type=F32)


def _pack_pairs(v):
    half = v.shape[1] // 2
    bits = pltpu.bitcast(v.astype(BF16).astype(F32), U32)
    return (bits[:, 0:half] & U32(0xFFFF0000)) | (bits[:, half:] >> 16)


def _store_slabs(ref, row0, packed):
    rows, width = packed.shape
    s = width // LANES
    for c in range(s):
        ref[pl.ds(row0 * s + c, rows, stride=s), :] = packed[:, c * LANES:(c + 1) * LANES]


def _load_slab_column(ref, row0, rows, s, c):
    p = ref[pl.ds(row0 * s + c, rows, stride=s), :]
    return pltpu.bitcast(p & U32(0xFFFF0000), F32), pltpu.bitcast(p << 16, F32)


def _proj_kernel(x_ref, w_ref, o_ref):
    o_ref[...] = _dot(x_ref[...].astype(BF16), w_ref[...].astype(BF16)).astype(o_ref.dtype)


def _proj(xb, wb):
    n, d = xb.shape
    n_in = wb.shape[1]
    return pl.pallas_call(
        _proj_kernel,
        grid=(n // PROJ_BM, n_in // PROJ_BN),
        in_specs=[pl.BlockSpec((PROJ_BM, d), lambda i, j: (i, 0)),
                  pl.BlockSpec((d, PROJ_BN), lambda i, j: (0, j))],
        out_specs=pl.BlockSpec((PROJ_BM, PROJ_BN), lambda i, j: (i, j)),
        out_shape=jax.ShapeDtypeStruct((n, n_in), BF16),
        compiler_params=pltpu.CompilerParams(
            dimension_semantics=("arbitrary", "arbitrary"),
            vmem_limit_bytes=VMEM_LIMIT_BYTES),
        name="proj",
    )(xb, wb)


def _mixer_kernel(proj_ref, halo_ref, x_ref, wpool_ref, pscale_ref, convw_ref,
                  wbra_ref, wbrb_ref, wo_ref, g1_ref, b1_ref, wrt_ref, br_ref,
                  x1_ref, x1p_ref, tope_ref, gate_ref, rank_ref, cnt_ref,
                  ext_ref, cext_ref, carry_ref, *, seq, alpha, pw, cw, d):
    tm = x_ref.shape[0]
    i = pl.program_id(0)
    blocks_per_seq = seq // tm
    j = i % blocks_per_seq
    keep_halo = (j > 0).astype(F32)
    row = lax.broadcasted_iota(jnp.int32, (tm, 1), 0)
    pos1 = (j * tm + row + 1).astype(F32)

    @pl.when(i == 0)
    def _():
        carry_ref[...] = jnp.zeros_like(carry_ref)

    o1, o2, o3 = pw + cw, pw + 2 * cw, pw + 3 * cw
    ext_ref[0:HALO, :] = halo_ref[:, 0:pw].astype(F32) * keep_halo
    ext_ref[HALO:HALO + tm, :] = proj_ref[:, 0:pw].astype(F32)
    cext_ref[0:HALO, :] = (halo_ref[:, o1:o2].astype(F32) * halo_ref[:, o2:o3].astype(F32)) * keep_halo
    cext_ref[HALO:HALO + tm, :] = proj_ref[:, o1:o2].astype(F32) * proj_ref[:, o2:o3].astype(F32)

    gw = pw // N_POOL_GROUPS
    hr = tm // MIX_SPLIT
    logit_parts = []
    for part in range(MIX_SPLIT):
        r0 = part * hr
        rs = slice(r0, r0 + hr)

        a_parts = []
        for g, w in enumerate(POOL_WINDOWS):
            cols = slice(g * gw, (g + 1) * gw)
            u = ext_ref[HALO + r0:HALO + r0 + hr, cols]
            s = u
            for sh in range(1, w):
                s = s + ext_ref[HALO + r0 - sh:HALO + r0 - sh + hr, cols]
            cnt = jnp.minimum(pos1[rs, :], float(w))
            pooled = s / cnt - u
            a_g = _dot(pooled.astype(BF16), wpool_ref[g]) * pscale_ref[:, cols]
            a_parts.append(a_g.astype(BF16))
        br_a = _dot(jnp.concatenate(a_parts, axis=1), wbra_ref[...])

        conv = cext_ref[HALO + r0:HALO + r0 + hr, :] * convw_ref[CONV_K - 1:CONV_K, :]
        for k in range(CONV_K - 1):
            sh = CONV_K - 1 - k
            conv = conv + cext_ref[HALO + r0 - sh:HALO + r0 - sh + hr, :] * convw_ref[k:k + 1, :]
        b = proj_ref[rs, pw:o1].astype(F32) * conv
        br_b = _dot(b.astype(BF16), wbrb_ref[...])

        g_a = proj_ref[rs, o3:o3 + d].astype(F32)
        g_b = proj_ref[rs, o3 + d:o3 + 2 * d].astype(F32)
        m = jax.nn.sigmoid(g_a) * br_a + jax.nn.sigmoid(g_b) * br_b
        mix = _dot(m.astype(BF16), wo_ref[...])
        x1 = _layer_norm(alpha * x_ref[rs, :] + mix, g1_ref[...], b1_ref[...])
        x1_ref[rs, :] = x1
        _store_slabs(x1p_ref, r0, _pack_pairs(x1))

        logit_parts.append(lax.dot_general(wrt_ref[...], x1.astype(BF16), (((1,), (1,)), ((), ())),
                                           preferred_element_type=F32))
    logits = jnp.concatenate(logit_parts, axis=1) + br_ref[...]
    e_iota = lax.broadcasted_iota(jnp.int32, logits.shape, 0)
    vals, idxs = [], []
    l = logits
    for _ in range(TOP_K):
        mx = jnp.max(l, axis=0, keepdims=True)
        ix = jnp.min(jnp.where(l == mx, e_iota, N_EXPERTS), axis=0, keepdims=True)
        vals.append(mx)
        idxs.append(ix)
        l = jnp.where(e_iota == ix, -jnp.inf, l)
    exps = [jnp.exp(v - vals[0]) for v in vals]
    denom = exps[0] + exps[1] + exps[2] + exps[3]
    onehot = jnp.zeros(logits.shape, F32)
    for k in range(TOP_K):
        tope_ref[k:k + 1, :] = idxs[k]
        gate_ref[k:k + 1, :] = exps[k] / denom
        onehot = onehot + (e_iota == idxs[k]).astype(F32)

    r_i = lax.broadcasted_iota(jnp.int32, (tm, tm), 0)
    c_i = lax.broadcasted_iota(jnp.int32, (tm, tm), 1)
    before = (r_i < c_i).astype(BF16)
    seen = _dot(onehot.astype(BF16), before) + carry_ref[:, 0:1]
    for k in range(TOP_K):
        rk = jnp.sum(jnp.where(e_iota == idxs[k], seen, 0.0), axis=0, keepdims=True)
        rank_ref[k:k + 1, :] = rk.astype(jnp.int32)
    carry_ref[...] = carry_ref[...] + jnp.sum(onehot, axis=1, keepdims=True)
    cnt_ref[...] = carry_ref[...].astype(jnp.int32)


def _mixer(proj, x2, wpool, pscale, convw, wbra, wbrb, wo, g1, b1, wrt, br, *, seq, alpha):
    n, d = x2.shape
    n_in = proj.shape[1]
    pw = wbra.shape[0]
    cw = wbrb.shape[0]
    tm = MIX_TM
    slab = d // 2 // LANES
    hb = tm // HALO
    const = lambda i: (0, 0)
    kern = functools.partial(_mixer_kernel, seq=seq, alpha=alpha, pw=pw, cw=cw, d=d)
    return pl.pallas_call(
        kern,
        grid=(n // tm,),
        in_specs=[
            pl.BlockSpec((tm, n_in), lambda i: (i, 0)),
            pl.BlockSpec((HALO, n_in // 2), lambda i: (jnp.maximum(i * hb - 1, 0), 0)),
            pl.BlockSpec((tm, d), lambda i: (i, 0)),
            pl.BlockSpec(wpool.shape, lambda i: (0, 0, 0)),
            pl.BlockSpec(pscale.shape, const),
            pl.BlockSpec(convw.shape, const),
            pl.BlockSpec(wbra.shape, const),
            pl.BlockSpec(wbrb.shape, const),
            pl.BlockSpec(wo.shape, const),
            pl.BlockSpec(g1.shape, const),
            pl.BlockSpec(b1.shape, const),
            pl.BlockSpec(wrt.shape, const),
            pl.BlockSpec(br.shape, const),
        ],
        out_specs=[
            pl.BlockSpec((tm, d), lambda i: (i, 0)),
            pl.BlockSpec((tm * slab, LANES), lambda i: (i, 0)),
            pl.BlockSpec((TOP_K, tm), lambda i: (0, i)),
            pl.BlockSpec((TOP_K, tm), lambda i: (0, i)),
            pl.BlockSpec((TOP_K, tm), lambda i: (0, i)),
            pl.BlockSpec((N_EXPERTS, LANES), const),
        ],
        out_shape=[
            jax.ShapeDtypeStruct((n, d), F32),
            jax.ShapeDtypeStruct((n * slab, LANES), U32),
            jax.ShapeDtypeStruct((TOP_K, n), jnp.int32),
            jax.ShapeDtypeStruct((TOP_K, n), F32),
            jax.ShapeDtypeStruct((TOP_K, n), jnp.int32),
            jax.ShapeDtypeStruct((N_EXPERTS, LANES), jnp.int32),
        ],
        scratch_shapes=[pltpu.VMEM((HALO + tm, pw), F32),
                        pltpu.VMEM((HALO + tm, cw), F32),
                        pltpu.VMEM((N_EXPERTS, LANES), F32)],
        compiler_params=pltpu.CompilerParams(
            dimension_semantics=("arbitrary",),
            vmem_limit_bytes=VMEM_LIMIT_BYTES),
        name="mixer",
    )(proj, proj, x2, wpool, pscale, convw, wbra, wbrb, wo, g1, b1, wrt, br)


def _dest_kernel(pst_ref, tope_ref, rank_ref, dest_ref):
    tope = tope_ref[...]
    dest = rank_ref[...]
    for e in range(N_EXPERTS):
        dest = dest + jnp.where(tope == e, pst_ref[e], 0)
    dest_ref[...] = dest


def _dest(pad_starts, tope, rank):
    whole = pl.BlockSpec(tope.shape, lambda i, *_: (0, 0))
    return pl.pallas_call(
        _dest_kernel,
        grid_spec=pltpu.PrefetchScalarGridSpec(
            num_scalar_prefetch=1, grid=(1,), in_specs=[whole, whole], out_specs=whole),
        out_shape=jax.ShapeDtypeStruct(tope.shape, jnp.int32),
        name="dest",
    )(pad_starts, tope, rank)


def _dispatch_kernel(pst_ref, pend_ref, dest_ref, x1p_hbm, xs_hbm, xbuf, zero_ref, ld_sem, sem, zsem,
                     *, slab, n_steps):
    tb = dest_ref.shape[1]
    i = pl.program_id(0)
    blk = ROW_BLK * slab
    chunk = tb * slab

    def load(c):
        row0 = c * chunk if isinstance(c, int) else pl.multiple_of(c * chunk, chunk)
        rows = pl.ds(row0, chunk)
        return pltpu.make_async_copy(x1p_hbm.at[rows], xbuf.at[rows], ld_sem.at[c])

    def zero_copy(b):
        return pltpu.make_async_copy(
            zero_ref, xs_hbm.at[pl.ds(pl.multiple_of(b * blk, blk), blk)], zsem)

    @pl.when(i == 0)
    def _():
        zero_ref[...] = jnp.zeros_like(zero_ref)

        def start(e, c):
            @pl.when(pend_ref[e] > pst_ref[e])
            def _():
                zero_copy(pend_ref[e] // ROW_BLK - 1).start()
            return c

        def wait(e, c):
            @pl.when(pend_ref[e] > pst_ref[e])
            def _():
                zero_copy(pend_ref[e] // ROW_BLK - 1).wait()
            return c

        def start_tail(b, c):
            zero_copy(b).start()
            return c

        def wait_tail(b, c):
            zero_copy(b).wait()
            return c

        used = pend_ref[N_EXPERTS - 1] // ROW_BLK
        total = xs_hbm.shape[0] // blk
        lax.fori_loop(0, N_EXPERTS, start, 0)
        lax.fori_loop(used, total, start_tail, 0)
        for c in range(n_steps):
            load(c).start()
        lax.fori_loop(0, N_EXPERTS, wait, 0)
        lax.fori_loop(used, total, wait_tail, 0)

    load(i).wait()

    def start(t, c):
        src = xbuf.at[pl.ds(pl.multiple_of((i * tb + t) * slab, slab), slab)]
        for k in range(TOP_K):
            dst = dest_ref[k, t]
            pltpu.make_async_copy(
                src, xs_hbm.at[pl.ds(pl.multiple_of(dst * slab, slab), slab)], sem,
            ).start(priority=k % 2)
        return c

    lax.fori_loop(0, tb, start, 0)

    def wait_one_step():
        for k in range(TOP_K):
            pltpu.make_async_copy(
                xbuf.at[pl.ds(0, chunk)], xs_hbm.at[pl.ds(0, chunk)], sem).wait()

    @pl.when(i > 0)
    def _():
        wait_one_step()

    @pl.when(i == n_steps - 1)
    def _():
        wait_one_step()


def _dispatch(pad_starts, pad_ends, dest, x1p, n, cap):
    slab = x1p.shape[0] // n
    tb = DISP_TB
    n_steps = n // tb
    smem_blk = pl.BlockSpec((TOP_K, tb), lambda i, *_: (0, i), memory_space=pltpu.SMEM)
    return pl.pallas_call(
        functools.partial(_dispatch_kernel, slab=slab, n_steps=n_steps),
        grid_spec=pltpu.PrefetchScalarGridSpec(
            num_scalar_prefetch=2,
            grid=(n_steps,),
            in_specs=[smem_blk, pl.BlockSpec(memory_space=pl.ANY)],
            out_specs=pl.BlockSpec(memory_space=pl.ANY),
            scratch_shapes=[pltpu.VMEM(x1p.shape, U32),
                            pltpu.VMEM((ROW_BLK * slab, LANES), U32),
                            pltpu.SemaphoreType.DMA((n_steps,)),
                            pltpu.SemaphoreType.DMA(()),
                            pltpu.SemaphoreType.DMA(())],
        ),
        out_shape=jax.ShapeDtypeStruct((cap * slab, LANES), U32),
        compiler_params=pltpu.CompilerParams(
            dimension_semantics=("arbitrary",),
            vmem_limit_bytes=VMEM_LIMIT_BYTES,
            has_side_effects=True),
        name="dispatch",
    )(pad_starts, pad_ends, dest, x1p)


def _ffn_kernel(ie_ref, ist_ref, inb_ref, used_ref,
                xs_hbm, wg_ref, wu_ref, bg_ref, bu_ref, wd_ref, bd_ref,
                y_hbm,
                slabs, xbuf, yacc, sem_in, sem_out, *, slab):
    w = pl.program_id(0)
    f = pl.program_id(1)
    n_items = pl.num_programs(0)
    nf = pl.num_programs(1)
    nb = inb_ref[w]
    blk = ROW_BLK * slab
    half = slab * LANES
    slot_rows = slabs.shape[0] // 2
    first_step = f == 0
    last_step = f == nf - 1
    prev_item = jnp.maximum(w - 1, 0)
    next_item = jnp.minimum(w + 1, n_items - 1)

    def rows(r):
        return pl.ds(pl.multiple_of(r * ROW_BLK, ROW_BLK), ROW_BLK)

    def slot_row0(item):
        return (item % 2) * (slot_rows // slab)

    def hbm_blk(ref, item, r):
        return ref.at[pl.ds(pl.multiple_of((ist_ref[item] + r) * blk, blk), blk)]

    def vmem_blk(item, r):
        return slabs.at[pl.ds(pl.multiple_of((item % 2) * slot_rows + r * blk, blk), blk)]

    def for_blocks(item, fn):
        def body(r, c):
            fn(r)
            return c
        lax.fori_loop(0, inb_ref[item], body, 0)

    def load(item):
        return lambda r: pltpu.make_async_copy(hbm_blk(xs_hbm, item, r), vmem_blk(item, r), sem_in)

    def store(item):
        return lambda r: pltpu.make_async_copy(
            vmem_blk(item, r), hbm_blk(y_hbm, item, r), sem_out.at[item % 2])

    @pl.when((w == 0) & first_step)
    def _():
        slabs[0:blk, :] = jnp.zeros((blk, LANES), U32)

        def tail_copy(b):
            return pltpu.make_async_copy(
                slabs.at[pl.ds(0, blk)], y_hbm.at[pl.ds(pl.multiple_of(b * blk, blk), blk)],
                sem_out.at[0])

        def start(b, c):
            tail_copy(b).start()
            return c

        def wait(b, c):
            tail_copy(b).wait()
            return c

        total = y_hbm.shape[0] // blk
        lax.fori_loop(used_ref[0], total, start, 0)
        lax.fori_loop(used_ref[0], total, wait, 0)
        for_blocks(w, lambda r: load(w)(r).start())

    @pl.when(last_step)
    def _():
        @pl.when(w >= 1)
        def _():
            for_blocks(prev_item, lambda r: store(prev_item)(r).wait())

        @pl.when(w + 1 < n_items)
        def _():
            for_blocks(next_item, lambda r: load(next_item)(r).start())

    @pl.when(nb > 0)
    def _():
        n_quads = lax.shift_right_logical(nb, 2)
        rem0 = n_quads * 4

        def unpack(row0, n_rows):
            rs = pl.ds(pl.multiple_of(row0, ROW_BLK), n_rows)
            for col in range(slab):
                hi, lo = _load_slab_column(slabs, slot_row0(w) + row0, n_rows, slab, col)
                xbuf[rs, col * LANES:(col + 1) * LANES] = hi.astype(BF16)
                xbuf[rs, half + col * LANES:half + (col + 1) * LANES] = lo.astype(BF16)

        def pack_and_store(blk0, n_blocks, acc):
            _store_slabs(slabs, slot_row0(w) + blk0 * ROW_BLK, _pack_pairs(acc))
            for r in range(n_blocks):
                store(w)(blk0 + r).start()

        def sub(blk0, n_blocks, mode):
            row0 = blk0 * ROW_BLK
            n_rows = n_blocks * ROW_BLK
            rs = pl.ds(pl.multiple_of(row0, ROW_BLK), n_rows)
            if mode == "first":
                unpack(row0, n_rows)
            xt = xbuf[rs, :]
            gate = jnp.minimum(_dot(xt, wg_ref[...]) + bg_ref[...], SWIGLU_LIMIT)
            up = jnp.clip(_dot(xt, wu_ref[...]) + bu_ref[...], -SWIGLU_LIMIT, SWIGLU_LIMIT)
            act = (up + 1.0) * (gate * jax.nn.sigmoid(SWIGLU_ALPHA * gate))
            contrib = _dot(act.astype(BF16), wd_ref[...])
            if mode == "first":
                yacc[rs, :] = contrib + bd_ref[...]
            elif mode == "last":
                pack_and_store(blk0, n_blocks, yacc[rs, :] + contrib)
            else:
                yacc[rs, :] += contrib

        def quads(mode):
            def body(q, c):
                sub(q * 4, 4, mode)
                return c
            lax.fori_loop(0, n_quads, body, 0)

        @pl.when(first_step)
        def _():
            for_blocks(w, lambda r: load(w)(r).wait())

            def prepare(r, c):
                unpack(r * ROW_BLK, ROW_BLK)
                yacc[rows(r), :] = jnp.broadcast_to(bd_ref[...], (ROW_BLK, yacc.shape[1]))
                return c

            lax.fori_loop(rem0, nb, prepare, 0)
            quads("first")

        @pl.when(jnp.logical_not(first_step | last_step))
        def _():
            quads("mid")

        @pl.when(last_step)
        def _():
            quads("last")

        @pl.when((nb & 2) != 0)
        def _():
            sub(rem0, 2, "mid")

        @pl.when((nb & 1) != 0)
        def _():
            sub(nb - 1, 1, "mid")

        @pl.when(last_step)
        def _():
            def finish(r, c):
                pack_and_store(r, 1, yacc[rows(r), :])
                return c

            lax.fori_loop(rem0, nb, finish, 0)

    @pl.when(last_step & (w == n_items - 1))
    def _():
        for_blocks(w, lambda r: store(w)(r).wait())


def _ffn(item_e, item_st, item_nb, used_blocks, xs, w_gu, b_gu, w_down, b_down, cap):
    slab = xs.shape[0] // cap
    n_e, d, f2 = w_gu.shape
    dff = f2 // 2
    tf = FFN_TF
    nf = dff // tf
    assert nf >= 2, "the ffn kernel prefetches the next item during a step that is not the first"
    n_items = item_e.shape[0]
    ts = ITEM_BLKS * ROW_BLK

    def fsel(w, f, inb):
        return jnp.where(inb[w] > 0, f, nf - 1)

    wg_spec = pl.BlockSpec((None, d, tf), lambda w, f, ie, ist, inb, u: (ie[w], 0, fsel(w, f, inb)))
    wu_spec = pl.BlockSpec((None, d, tf), lambda w, f, ie, ist, inb, u: (ie[w], 0, nf + fsel(w, f, inb)))
    bg_spec = pl.BlockSpec((None, 1, tf), lambda w, f, ie, ist, inb, u: (ie[w], 0, fsel(w, f, inb)))
    bu_spec = pl.BlockSpec((None, 1, tf), lambda w, f, ie, ist, inb, u: (ie[w], 0, nf + fsel(w, f, inb)))
    wd_spec = pl.BlockSpec((None, tf, d), lambda w, f, ie, ist, inb, u: (ie[w], fsel(w, f, inb), 0))
    bd_spec = pl.BlockSpec((None, 1, d), lambda w, f, ie, ist, inb, u: (ie[w], 0, 0))
    return pl.pallas_call(
        functools.partial(_ffn_kernel, slab=slab),
        grid_spec=pltpu.PrefetchScalarGridSpec(
            num_scalar_prefetch=4,
            grid=(n_items, nf),
            in_specs=[pl.BlockSpec(memory_space=pl.ANY),
                      wg_spec, wu_spec, bg_spec, bu_spec, wd_spec, bd_spec],
            out_specs=pl.BlockSpec(memory_space=pl.ANY),
            scratch_shapes=[pltpu.VMEM((2 * ts * slab, LANES), U32),
                            pltpu.VMEM((ts, d), BF16),
                            pltpu.VMEM((ts, d), F32),
                            pltpu.SemaphoreType.DMA(()),
                            pltpu.SemaphoreType.DMA((2,))],
        ),
        out_shape=jax.ShapeDtypeStruct((cap * slab, LANES), U32),
        compiler_params=pltpu.CompilerParams(
            dimension_semantics=("arbitrary", "arbitrary"),
            vmem_limit_bytes=VMEM_LIMIT_BYTES,
            has_side_effects=True),
        name="ffn",
    )(item_e, item_st, item_nb, used_blocks, xs, w_gu, w_gu, b_gu.reshape(n_e, 1, f2), b_gu.reshape(n_e, 1, f2),
      w_down, b_down.reshape(n_e, 1, d))


def _final_kernel(dcur_ref, dnxt_ref, gate_ref, x1_ref, p_ref, y_hbm,
                  g2_ref, b2_ref, wpg_ref, bpg_ref, wple_ref, g3_ref, b3_ref,
                  o_ref, ybuf, ffn_ref, gbuf, sem, *, alpha, slab):
    tm = x1_ref.shape[0]
    half = slab * LANES
    i = pl.program_id(0)
    n_steps = pl.num_programs(0)
    slot = i % 2
    slot_tokens = TOP_K * tm
    slot_rows = slot_tokens * slab

    def gather(d_ref, into):
        def start(t, c):
            for k in range(TOP_K):
                src = d_ref[k, t]
                dst = (into * slot_tokens + k * tm + t) * slab
                pltpu.make_async_copy(
                    y_hbm.at[pl.ds(pl.multiple_of(src * slab, slab), slab)],
                    ybuf.at[pl.ds(pl.multiple_of(dst, slab), slab)], sem.at[into],
                ).start(priority=k % 2)
            return c
        lax.fori_loop(0, tm, start, 0)

    @pl.when(i == 0)
    def _():
        gather(dcur_ref, slot)

    @pl.when(i + 1 < n_steps)
    def _():
        gather(dnxt_ref, 1 - slot)

    pltpu.make_async_copy(
        y_hbm.at[pl.ds(0, slot_rows)],
        ybuf.at[pl.ds(pl.multiple_of(slot * slot_rows, slot_rows), slot_rows)], sem.at[slot]).wait()

    for k in range(TOP_K):
        gbuf[k] = jnp.broadcast_to(gate_ref[:, k:k + 1], (tm, LANES))

    hr = tm // FINAL_SPLIT
    for part in range(FINAL_SPLIT):
        rs = slice(part * hr, (part + 1) * hr)
        for col in range(slab):
            acc_hi = acc_lo = None
            for k in range(TOP_K):
                hi, lo = _load_slab_column(
                    ybuf, slot * slot_tokens + k * tm + part * hr, hr, slab, col)
                g = gbuf[k, rs, :]
                acc_hi = hi * g if acc_hi is None else acc_hi + hi * g
                acc_lo = lo * g if acc_lo is None else acc_lo + lo * g
            ffn_ref[rs, col * LANES:(col + 1) * LANES] = acc_hi
            ffn_ref[rs, half + col * LANES:half + (col + 1) * LANES] = acc_lo
        h2 = _layer_norm(alpha * x1_ref[rs, :] + ffn_ref[rs, :], g2_ref[...], b2_ref[...])
        pg = jax.nn.sigmoid(_dot(h2.astype(BF16), wpg_ref[...]) + bpg_ref[...])
        pe = _dot(p_ref[rs, :].astype(BF16), wple_ref[...])
        o_ref[rs, :] = _layer_norm(alpha * h2 + pg * pe, g3_ref[...], b3_ref[...])


def _final(dest, gates_t, x1, p2, y, g2, b2, wpg, bpg, wple, g3, b3, cap, *, alpha):
    n, d = x1.shape
    slab = y.shape[0] // cap
    tm = MIX_TM
    n_steps = n // tm
    dest_cur = pl.BlockSpec((TOP_K, tm), lambda i: (0, i), memory_space=pltpu.SMEM)
    dest_next = pl.BlockSpec((TOP_K, tm), lambda i: (0, jnp.minimum(i + 1, n_steps - 1)),
                             memory_space=pltpu.SMEM)
    const = lambda i: (0, 0)
    row_blk = lambda width: pl.BlockSpec((tm, width), lambda i: (i, 0))
    return pl.pallas_call(
        functools.partial(_final_kernel, alpha=alpha, slab=slab),
        grid_spec=pltpu.PrefetchScalarGridSpec(
            num_scalar_prefetch=0,
            grid=(n_steps,),
            in_specs=[dest_cur, dest_next, row_blk(TOP_K), row_blk(d), row_blk(p2.shape[1]),
                      pl.BlockSpec(memory_space=pl.ANY),
                      pl.BlockSpec(g2.shape, const), pl.BlockSpec(b2.shape, const),
                      pl.BlockSpec(wpg.shape, const), pl.BlockSpec(bpg.shape, const),
                      pl.BlockSpec(wple.shape, const),
                      pl.BlockSpec(g3.shape, const), pl.BlockSpec(b3.shape, const)],
            out_specs=row_blk(d),
            scratch_shapes=[pltpu.VMEM((2 * TOP_K * tm * slab, LANES), U32),
                            pltpu.VMEM((tm, d), F32),
                            pltpu.VMEM((TOP_K, tm, LANES), F32),
                            pltpu.SemaphoreType.DMA((2,))],
        ),
        out_shape=jax.ShapeDtypeStruct((n, d), F32),
        compiler_params=pltpu.CompilerParams(
            dimension_semantics=("arbitrary",),
            vmem_limit_bytes=VMEM_LIMIT_BYTES),
        name="final",
    )(dest, dest, gates_t, x1, p2, y, g2, b2, wpg, bpg, wple, g3, b3)


def _plan(counts, n_items):
    nblk = (counts + ROW_BLK - 1) // ROW_BLK
    blk_end = jnp.cumsum(nblk)
    blk_start = blk_end - nblk
    items_per = (nblk + ITEM_BLKS - 1) // ITEM_BLKS
    item_end = jnp.cumsum(items_per)
    item_start = item_end - items_per
    total = item_end[-1]
    w = jnp.arange(n_items, dtype=jnp.int32)
    w_eff = jnp.minimum(w, total - 1)
    e = jnp.minimum(jnp.searchsorted(item_end, w_eff, side="right"), N_EXPERTS - 1).astype(jnp.int32)
    j = w_eff - item_start[e]
    st = blk_start[e] + j * ITEM_BLKS
    nb = jnp.where(w < total, jnp.clip(nblk[e] - j * ITEM_BLKS, 0, ITEM_BLKS), 0)
    return ((blk_start * ROW_BLK).astype(jnp.int32), (blk_end * ROW_BLK).astype(jnp.int32),
            e, st.astype(jnp.int32), nb.astype(jnp.int32))


def kernel(x, p, w_in, w_pool, pool_scale, conv_w, w_br_a, w_br_b, w_o, ln1_g, ln1_b,
           w_router, b_router, w_gu, b_gu, w_down, b_down, ln2_g, ln2_b,
           w_pg, b_pg, w_ple, ln3_g, ln3_b):
    depth = w_in.shape[0]
    bsz, seq, d = x.shape
    n = bsz * seq
    alpha = (2.0 * depth) ** 0.25
    nk = n * TOP_K
    n_row_blocks = (nk + N_EXPERTS * (ROW_BLK - 1) + ROW_BLK - 1) // ROW_BLK
    cap = n_row_blocks * ROW_BLK
    n_items = N_EXPERTS + n_row_blocks // ITEM_BLKS
    row = lambda v: v.reshape(1, -1)

    h = x.reshape(n, d)
    for i in range(depth):
        proj = _proj(h, w_in[i])
        x1, x1p, tope, gates, rank, cnt = _mixer(
            proj, h, w_pool[i].astype(BF16), row(pool_scale[i]), conv_w[i],
            w_br_a[i].astype(BF16), w_br_b[i].astype(BF16), w_o[i].astype(BF16),
            row(ln1_g[i]), row(ln1_b[i]), w_router[i].T.astype(BF16),
            b_router[i].reshape(N_EXPERTS, 1), seq=seq, alpha=alpha)
        pad_starts, pad_ends, item_e, item_st, item_nb = _plan(cnt[:, 0], n_items)
        dest = _dest(pad_starts, tope, rank)
        xs = _dispatch(pad_starts, pad_ends, dest, x1p, n, cap)
        used_blocks = pad_ends[N_EXPERTS - 1:] // ROW_BLK
        y = _ffn(item_e, item_st, item_nb, used_blocks, xs, w_gu[i], b_gu[i], w_down[i], b_down[i], cap)
        h = _final(dest, gates.T, x1, p[i].reshape(n, -1), y,
                   row(ln2_g[i]), row(ln2_b[i]), w_pg[i].astype(BF16), row(b_pg[i]),
                   w_ple[i].astype(BF16), row(ln3_g[i]), row(ln3_b[i]), cap, alpha=alpha)
    return h.reshape(bsz, seq, d)
```

```python
import functools

import jax
import jax.numpy as jnp
from jax import lax
from jax.experimental import pallas as pl
from jax.experimental.pallas import tpu as pltpu

F32 = jnp.float32
BF16 = jnp.bfloat16
U32 = jnp.uint32

POOL_WINDOWS = (2, 4, 8, 16)
N_POOL_GROUPS = 4
CONV_K = 3
N_EXPERTS = 32
TOP_K = 4
SWIGLU_LIMIT = 7.0
SWIGLU_ALPHA = 1.702
LN_EPS = 1e-5

LANES = 128
VMEM_LIMIT_BYTES = 56 * 1024 * 1024

HALO = 16
PROJ_BM = 1024
PROJ_BN = 1024
MIX_TM = 256
MIX_SPLIT = 1
FINAL_SPLIT = 2
DISP_TB = 512
ROW_BLK = 128
ITEM_BLKS = 10
FFN_TF = 512


def _layer_norm(v, g, b):
    mu = jnp.mean(v, axis=-1, keepdims=True)
    c = v - mu
    var = jnp.mean(c * c, axis=-1, keepdims=True)
    return c * lax.rsqrt(var + LN_EPS) * g + b


def _dot(a, b):
    return jnp.dot(a, b, preferred_element_type=F32)


def _pack_pairs(v):
    half = v.shape[1] // 2
    bits = pltpu.bitcast(v.astype(BF16).astype(F32), U32)
    return (bits[:, 0:half] & U32(0xFFFF0000)) | (bits[:, half:] >> 16)


def _store_slabs(ref, row0, packed):
    rows, width = packed.shape
    s = width // LANES
    for c in range(s):
        ref[pl.ds(row0 * s + c, rows, stride=s), :] = packed[:, c * LANES:(c + 1) * LANES]


def _load_slab_column(ref, row0, rows, s, c):
    p = ref[pl.ds(row0 * s + c, rows, stride=s), :]
    return pltpu.bitcast(p & U32(0xFFFF0000), F32), pltpu.bitcast(p << 16, F32)


def _proj_kernel(x_ref, w_ref, o_ref):
    o_ref[...] = _dot(x_ref[...].astype(BF16), w_ref[...].astype(BF16)).astype(o_ref.dtype)


def _proj(xb, wb):
    n, d = xb.shape
    n_in = wb.shape[1]
    return pl.pallas_call(
        _proj_kernel,
        grid=(n // PROJ_BM, n_in // PROJ_BN),
        in_specs=[pl.BlockSpec((PROJ_BM, d), lambda i, j: (i, 0)),
                  pl.BlockSpec((d, PROJ_BN), lambda i, j: (0, j))],
        out_specs=pl.BlockSpec((PROJ_BM, PROJ_BN), lambda i, j: (i, j)),
        out_shape=jax.ShapeDtypeStruct((n, n_in), BF16),
        compiler_params=pltpu.CompilerParams(
            dimension_semantics=("arbitrary", "arbitrary"),
            vmem_limit_bytes=VMEM_LIMIT_BYTES),
        name="proj",
    )(xb, wb)


def _mixer_kernel(proj_ref, halo_ref, x_ref, wpool_ref, pscale_ref, convw_ref,
                  wbra_ref, wbrb_ref, wo_ref, g1_ref, b1_ref, wrt_ref, br_ref,
                  x1_ref, x1p_ref, tope_ref, gate_ref, rank_ref, cnt_ref,
                  ext_ref, cext_ref, carry_ref, *, seq, alpha, pw, cw, d):
    tm = x_ref.shape[0]
    i = pl.program_id(0)
    blocks_per_seq = seq // tm
    j = i % blocks_per_seq
    keep_halo = (j > 0).astype(F32)
    row = lax.broadcasted_iota(jnp.int32, (tm, 1), 0)
    pos1 = (j * tm + row + 1).astype(F32)

    @pl.when(i == 0)
    def _():
        carry_ref[...] = jnp.zeros_like(carry_ref)

    o1, o2, o3 = pw + cw, pw + 2 * cw, pw + 3 * cw
    ext_ref[0:HALO, :] = halo_ref[:, 0:pw].astype(F32) * keep_halo
    ext_ref[HALO:HALO + tm, :] = proj_ref[:, 0:pw].astype(F32)
    cext_ref[0:HALO, :] = (halo_ref[:, o1:o2].astype(F32) * halo_ref[:, o2:o3].astype(F32)) * keep_halo
    cext_ref[HALO:HALO + tm, :] = proj_ref[:, o1:o2].astype(F32) * proj_ref[:, o2:o3].astype(F32)

    gw = pw // N_POOL_GROUPS
    hr = tm // MIX_SPLIT
    logit_parts = []
    for part in range(MIX_SPLIT):
        r0 = part * hr
        rs = slice(r0, r0 + hr)

        a_parts = []
        for g, w in enumerate(POOL_WINDOWS):
            cols = slice(g * gw, (g + 1) * gw)
            u = ext_ref[HALO + r0:HALO + r0 + hr, cols]
            s = u
            for sh in range(1, w):
                s = s + ext_ref[HALO + r0 - sh:HALO + r0 - sh + hr, cols]
            cnt = jnp.minimum(pos1[rs, :], float(w))
            pooled = s / cnt - u
            a_g = _dot(pooled.astype(BF16), wpool_ref[g]) * pscale_ref[:, cols]
            a_parts.append(a_g.astype(BF16))
        br_a = _dot(jnp.concatenate(a_parts, axis=1), wbra_ref[...])

        conv = cext_ref[HALO + r0:HALO + r0 + hr, :] * convw_ref[CONV_K - 1:CONV_K, :]
        for k in range(CONV_K - 1):
            sh = CONV_K - 1 - k
            conv = conv + cext_ref[HALO + r0 - sh:HALO + r0 - sh + hr, :] * convw_ref[k:k + 1, :]
        b = proj_ref[rs, pw:o1].astype(F32) * conv
        br_b = _dot(b.astype(BF16), wbrb_ref[...])

        g_a = proj_ref[rs, o3:o3 + d].astype(F32)
        g_b = proj_ref[rs, o3 + d:o3 + 2 * d].astype(F32)
        m = jax.nn.sigmoid(g_a) * br_a + jax.nn.sigmoid(g_b) * br_b
        mix = _dot(m.astype(BF16), wo_ref[...])
        x1 = _layer_norm(alpha * x_ref[rs, :] + mix, g1_ref[...], b1_ref[...])
        x1_ref[rs, :] = x1
        _store_slabs(x1p_ref, r0, _pack_pairs(x1))

        logit_parts.append(lax.dot_general(wrt_ref[...], x1.astype(BF16), (((1,), (1,)), ((), ())),
                                           preferred_element_type=F32))
    logits = jnp.concatenate(logit_parts, axis=1) + br_ref[...]
    e_iota = lax.broadcasted_iota(jnp.int32, logits.shape, 0)
    vals, idxs = [], []
    l = logits
    for _ in range(TOP_K):
        mx = jnp.max(l, axis=0, keepdims=True)
        ix = jnp.min(jnp.where(l == mx, e_iota, N_EXPERTS), axis=0, keepdims=True)
        vals.append(mx)
        idxs.append(ix)
        l = jnp.where(e_iota == ix, -jnp.inf, l)
    exps = [jnp.exp(v - vals[0]) for v in vals]
    denom = exps[0] + exps[1] + exps[2] + exps[3]
    onehot = jnp.zeros(logits.shape, F32)
    for k in range(TOP_K):
        tope_ref[k:k + 1, :] = idxs[k]
        gate_ref[k:k + 1, :] = exps[k] / denom
        onehot = onehot + (e_iota == idxs[k]).astype(F32)

    r_i = lax.broadcasted_iota(jnp.int32, (tm, tm), 0)
    c_i = lax.broadcasted_iota(jnp.int32, (tm, tm), 1)
    before = (r_i < c_i).astype(BF16)
    seen = _dot(onehot.astype(BF16), before) + carry_ref[:, 0:1]
    for k in range(TOP_K):
        rk = jnp.sum(jnp.where(e_iota == idxs[k], seen, 0.0), axis=0, keepdims=True)
        rank_ref[k:k + 1, :] = rk.astype(jnp.int32)
    carry_ref[...] = carry_ref[...] + jnp.sum(onehot, axis=1, keepdims=True)
    cnt_ref[...] = carry_ref[...].astype(jnp.int32)


def _mixer(proj, x2, wpool, pscale, convw, wbra, wbrb, wo, g1, b1, wrt, br, *, seq, alpha):
    n, d = x2.shape
    n_in = proj.shape[1]
    pw = wbra.shape[0]
    cw = wbrb.shape[0]
    tm = MIX_TM
    slab = d // 2 // LANES
    hb = tm // HALO
    const = lambda i: (0, 0)
    kern = functools.partial(_mixer_kernel, seq=seq, alpha=alpha, pw=pw, cw=cw, d=d)
    return pl.pallas_call(
        kern,
        grid=(n // tm,),
        in_specs=[
            pl.BlockSpec((tm, n_in), lambda i: (i, 0)),
            pl.BlockSpec((HALO, n_in // 2), lambda i: (jnp.maximum(i * hb - 1, 0), 0)),
            pl.BlockSpec((tm, d), lambda i: (i, 0)),
            pl.BlockSpec(wpool.shape, lambda i: (0, 0, 0)),
            pl.BlockSpec(pscale.shape, const),
            pl.BlockSpec(convw.shape, const),
            pl.BlockSpec(wbra.shape, const),
            pl.BlockSpec(wbrb.shape, const),
            pl.BlockSpec(wo.shape, const),
            pl.BlockSpec(g1.shape, const),
            pl.BlockSpec(b1.shape, const),
            pl.BlockSpec(wrt.shape, const),
            pl.BlockSpec(br.shape, const),
        ],
        out_specs=[
            pl.BlockSpec((tm, d), lambda i: (i, 0)),
            pl.BlockSpec((tm * slab, LANES), lambda i: (i, 0)),
            pl.BlockSpec((TOP_K, tm), lambda i: (0, i)),
            pl.BlockSpec((TOP_K, tm), lambda i: (0, i)),
            pl.BlockSpec((TOP_K, tm), lambda i: (0, i)),
            pl.BlockSpec((N_EXPERTS, LANES), const),
        ],
        out_shape=[
            jax.ShapeDtypeStruct((n, d), F32),
            jax.ShapeDtypeStruct((n * slab, LANES), U32),
            jax.ShapeDtypeStruct((TOP_K, n), jnp.int32),
            jax.ShapeDtypeStruct((TOP_K, n), F32),
            jax.ShapeDtypeStruct((TOP_K, n), jnp.int32),
            jax.ShapeDtypeStruct((N_EXPERTS, LANES), jnp.int32),
        ],
        scratch_shapes=[pltpu.VMEM((HALO + tm, pw), F32),
                        pltpu.VMEM((HALO + tm, cw), F32),
                        pltpu.VMEM((N_EXPERTS, LANES), F32)],
        compiler_params=pltpu.CompilerParams(
            dimension_semantics=("arbitrary",),
            vmem_limit_bytes=VMEM_LIMIT_BYTES),
        name="mixer",
    )(proj, proj, x2, wpool, pscale, convw, wbra, wbrb, wo, g1, b1, wrt, br)


def _dest_kernel(pst_ref, tope_ref, rank_ref, dest_ref):
    tope = tope_ref[...]
    dest = rank_ref[...]
    for e in range(N_EXPERTS):
        dest = dest + jnp.where(tope == e, pst_ref[e], 0)
    dest_ref[...] = dest


def _dest(pad_starts, tope, rank):
    whole = pl.BlockSpec(tope.shape, lambda i, *_: (0, 0))
    return pl.pallas_call(
        _dest_kernel,
        grid_spec=pltpu.PrefetchScalarGridSpec(
            num_scalar_prefetch=1, grid=(1,), in_specs=[whole, whole], out_specs=whole),
        out_shape=jax.ShapeDtypeStruct(tope.shape, jnp.int32),
        name="dest",
    )(pad_starts, tope, rank)


def _dispatch_kernel(pst_ref, pend_ref, dest_ref, x1p_hbm, xs_hbm, xbuf, zero_ref, ld_sem, sem, zsem,
                     *, slab, n_steps):
    tb = dest_ref.shape[1]
    i = pl.program_id(0)
    blk = ROW_BLK * slab
    chunk = tb * slab

    def load(c):
        row0 = c * chunk if isinstance(c, int) else pl.multiple_of(c * chunk, chunk)
        rows = pl.ds(row0, chunk)
        return pltpu.make_async_copy(x1p_hbm.at[rows], xbuf.at[rows], ld_sem.at[c])

    def zero_copy(b):
        return pltpu.make_async_copy(
            zero_ref, xs_hbm.at[pl.ds(pl.multiple_of(b * blk, blk), blk)], zsem)

    @pl.when(i == 0)
    def _():
        zero_ref[...] = jnp.zeros_like(zero_ref)

        def start(e, c):
            @pl.when(pend_ref[e] > pst_ref[e])
            def _():
                zero_copy(pend_ref[e] // ROW_BLK - 1).start()
            return c

        def wait(e, c):
            @pl.when(pend_ref[e] > pst_ref[e])
            def _():
                zero_copy(pend_ref[e] // ROW_BLK - 1).wait()
            return c

        def start_tail(b, c):
            zero_copy(b).start()
            return c

        def wait_tail(b, c):
            zero_copy(b).wait()
            return c

        used = pend_ref[N_EXPERTS - 1] // ROW_BLK
        total = xs_hbm.shape[0] // blk
        lax.fori_loop(0, N_EXPERTS, start, 0)
        lax.fori_loop(used, total, start_tail, 0)
        for c in range(n_steps):
            load(c).start()
        lax.fori_loop(0, N_EXPERTS, wait, 0)
        lax.fori_loop(used, total, wait_tail, 0)

    load(i).wait()

    def start(t, c):
        src = xbuf.at[pl.ds(pl.multiple_of((i * tb + t) * slab, slab), slab)]
        for k in range(TOP_K):
            dst = dest_ref[k, t]
            pltpu.make_async_copy(
                src, xs_hbm.at[pl.ds(pl.multiple_of(dst * slab, slab), slab)], sem,
            ).start(priority=k % 2)
        return c

    lax.fori_loop(0, tb, start, 0, unroll=4)

    def wait_one_step():
        for k in range(TOP_K):
            pltpu.make_async_copy(
                xbuf.at[pl.ds(0, chunk)], xs_hbm.at[pl.ds(0, chunk)], sem).wait()

    @pl.when(i > 0)
    def _():
        wait_one_step()

    @pl.when(i == n_steps - 1)
    def _():
        wait_one_step()


def _dispatch(pad_starts, pad_ends, dest, x1p, n, cap):
    slab = x1p.shape[0] // n
    tb = DISP_TB
    n_steps = n // tb
    smem_blk = pl.BlockSpec((TOP_K, tb), lambda i, *_: (0, i), memory_space=pltpu.SMEM)
    return pl.pallas_call(
        functools.partial(_dispatch_kernel, slab=slab, n_steps=n_steps),
        grid_spec=pltpu.PrefetchScalarGridSpec(
            num_scalar_prefetch=2,
            grid=(n_steps,),
            in_specs=[smem_blk, pl.BlockSpec(memory_space=pl.ANY)],
            out_specs=pl.BlockSpec(memory_space=pl.ANY),
            scratch_shapes=[pltpu.VMEM(x1p.shape, U32),
                            pltpu.VMEM((ROW_BLK * slab, LANES), U32),
                            pltpu.SemaphoreType.DMA((n_steps,)),
                            pltpu.SemaphoreType.DMA(()),
                            pltpu.SemaphoreType.DMA(())],
        ),
        out_shape=jax.ShapeDtypeStruct((cap * slab, LANES), U32),
        compiler_params=pltpu.CompilerParams(
            dimension_semantics=("arbitrary",),
            vmem_limit_bytes=VMEM_LIMIT_BYTES,
            has_side_effects=True),
        name="dispatch",
    )(pad_starts, pad_ends, dest, x1p)


def _ffn_kernel(ie_ref, ist_ref, inb_ref, used_ref,
                xs_hbm, wg_ref, wu_ref, bg_ref, bu_ref, wd_ref, bd_ref,
                y_hbm,
                slabs, xbuf, yacc, sem_in, sem_out, *, slab):
    w = pl.program_id(0)
    f = pl.program_id(1)
    n_items = pl.num_programs(0)
    nf = pl.num_programs(1)
    nb = inb_ref[w]
    blk = ROW_BLK * slab
    half = slab * LANES
    slot_rows = slabs.shape[0] // 2
    first_step = f == 0
    last_step = f == nf - 1
    prev_item = jnp.maximum(w - 1, 0)
    next_item = jnp.minimum(w + 1, n_items - 1)

    def rows(r):
        return pl.ds(pl.multiple_of(r * ROW_BLK, ROW_BLK), ROW_BLK)

    def slot_row0(item):
        return (item % 2) * (slot_rows // slab)

    def hbm_blk(ref, item, r):
        return ref.at[pl.ds(pl.multiple_of((ist_ref[item] + r) * blk, blk), blk)]

    def vmem_blk(item, r):
        return slabs.at[pl.ds(pl.multiple_of((item % 2) * slot_rows + r * blk, blk), blk)]

    def for_blocks(item, fn):
        def body(r, c):
            fn(r)
            return c
        lax.fori_loop(0, inb_ref[item], body, 0)

    def load(item):
        return lambda r: pltpu.make_async_copy(hbm_blk(xs_hbm, item, r), vmem_blk(item, r), sem_in)

    def store(item):
        return lambda r: pltpu.make_async_copy(
            vmem_blk(item, r), hbm_blk(y_hbm, item, r), sem_out.at[item % 2])

    @pl.when((w == 0) & first_step)
    def _():
        slabs[0:blk, :] = jnp.zeros((blk, LANES), U32)

        def tail_copy(b):
            return pltpu.make_async_copy(
                slabs.at[pl.ds(0, blk)], y_hbm.at[pl.ds(pl.multiple_of(b * blk, blk), blk)],
                sem_out.at[0])

        def start(b, c):
            tail_copy(b).start()
            return c

        def wait(b, c):
            tail_copy(b).wait()
            return c

        total = y_hbm.shape[0] // blk
        lax.fori_loop(used_ref[0], total, start, 0)
        lax.fori_loop(used_ref[0], total, wait, 0)
        for_blocks(w, lambda r: load(w)(r).start())

    @pl.when(last_step)
    def _():
        @pl.when(w >= 1)
        def _():
            for_blocks(prev_item, lambda r: store(prev_item)(r).wait())

        @pl.when(w + 1 < n_items)
        def _():
            for_blocks(next_item, lambda r: load(next_item)(r).start())

    @pl.when(nb > 0)
    def _():
        n_quads = lax.shift_right_logical(nb, 2)
        rem0 = n_quads * 4

        def unpack(row0, n_rows):
            rs = pl.ds(pl.multiple_of(row0, ROW_BLK), n_rows)
            for col in range(slab):
                hi, lo = _load_slab_column(slabs, slot_row0(w) + row0, n_rows, slab, col)
                xbuf[rs, col * LANES:(col + 1) * LANES] = hi.astype(BF16)
                xbuf[rs, half + col * LANES:half + (col + 1) * LANES] = lo.astype(BF16)

        def pack_and_store(blk0, n_blocks, acc):
            _store_slabs(slabs, slot_row0(w) + blk0 * ROW_BLK, _pack_pairs(acc))
            for r in range(n_blocks):
                store(w)(blk0 + r).start()

        def sub(blk0, n_blocks, mode):
            row0 = blk0 * ROW_BLK
            n_rows = n_blocks * ROW_BLK
            rs = pl.ds(pl.multiple_of(row0, ROW_BLK), n_rows)
            if mode == "first":
                unpack(row0, n_rows)
            xt = xbuf[rs, :]
            gate = jnp.minimum(_dot(xt, wg_ref[...]) + bg_ref[...], SWIGLU_LIMIT)
            up = jnp.clip(_dot(xt, wu_ref[...]) + bu_ref[...], -SWIGLU_LIMIT, SWIGLU_LIMIT)
            act = (up + 1.0) * (gate * jax.nn.sigmoid(SWIGLU_ALPHA * gate))
            contrib = _dot(act.astype(BF16), wd_ref[...])
            if mode == "first":
                yacc[rs, :] = contrib + bd_ref[...]
            elif mode == "last":
                pack_and_store(blk0, n_blocks, yacc[rs, :] + contrib)
            else:
                yacc[rs, :] += contrib

        def quads(mode):
            def body(q, c):
                sub(q * 4, 4, mode)
                return c
            lax.fori_loop(0, n_quads, body, 0)

        @pl.when(first_step)
        def _():
            for_blocks(w, lambda r: load(w)(r).wait())

            def prepare(r, c):
                unpack(r * ROW_BLK, ROW_BLK)
                yacc[rows(r), :] = jnp.broadcast_to(bd_ref[...], (ROW_BLK, yacc.shape[1]))
                return c

            lax.fori_loop(rem0, nb, prepare, 0)
            quads("first")

        @pl.when(jnp.logical_not(first_step | last_step))
        def _():
            quads("mid")

        @pl.when(last_step)
        def _():
            quads("last")

        @pl.when((nb & 2) != 0)
        def _():
            sub(rem0, 2, "mid")

        @pl.when((nb & 1) != 0)
        def _():
            sub(nb - 1, 1, "mid")

        @pl.when(last_step)
        def _():
            def finish(r, c):
                pack_and_store(r, 1, yacc[rows(r), :])
                return c

            lax.fori_loop(rem0, nb, finish, 0)

    @pl.when(last_step & (w == n_items - 1))
    def _():
        for_blocks(w, lambda r: store(w)(r).wait())


def _ffn(item_e, item_st, item_nb, used_items, used_blocks, xs, w_gu, b_gu, w_down, b_down, cap):
    slab = xs.shape[0] // cap
    n_e, d, f2 = w_gu.shape
    dff = f2 // 2
    tf = FFN_TF
    nf = dff // tf
    assert nf >= 2, "the ffn kernel prefetches the next item during a step that is not the first"
    n_items = item_e.shape[0]
    ts = ITEM_BLKS * ROW_BLK

    def fsel(w, f, inb):
        return jnp.where(inb[w] > 0, f, nf - 1)

    wg_spec = pl.BlockSpec((None, d, tf), lambda w, f, ie, ist, inb, u: (ie[w], 0, fsel(w, f, inb)))
    wu_spec = pl.BlockSpec((None, d, tf), lambda w, f, ie, ist, inb, u: (ie[w], 0, nf + fsel(w, f, inb)))
    bg_spec = pl.BlockSpec((None, 1, tf), lambda w, f, ie, ist, inb, u: (ie[w], 0, fsel(w, f, inb)))
    bu_spec = pl.BlockSpec((None, 1, tf), lambda w, f, ie, ist, inb, u: (ie[w], 0, nf + fsel(w, f, inb)))
    wd_spec = pl.BlockSpec((None, tf, d), lambda w, f, ie, ist, inb, u: (ie[w], fsel(w, f, inb), 0))
    bd_spec = pl.BlockSpec((None, 1, d), lambda w, f, ie, ist, inb, u: (ie[w], 0, 0))
    return pl.pallas_call(
        functools.partial(_ffn_kernel, slab=slab),
        grid_spec=pltpu.PrefetchScalarGridSpec(
            num_scalar_prefetch=4,
            grid=(used_items[0], nf),
            in_specs=[pl.BlockSpec(memory_space=pl.ANY),
                      wg_spec, wu_spec, bg_spec, bu_spec, wd_spec, bd_spec],
            out_specs=pl.BlockSpec(memory_space=pl.ANY),
            scratch_shapes=[pltpu.VMEM((2 * ts * slab, LANES), U32),
                            pltpu.VMEM((ts, d), BF16),
                            pltpu.VMEM((ts, d), F32),
                            pltpu.SemaphoreType.DMA(()),
                            pltpu.SemaphoreType.DMA((2,))],
        ),
        out_shape=jax.ShapeDtypeStruct((cap * slab, LANES), U32),
        compiler_params=pltpu.CompilerParams(
            dimension_semantics=("arbitrary", "arbitrary"),
            vmem_limit_bytes=VMEM_LIMIT_BYTES,
            has_side_effects=True),
        name="ffn",
    )(item_e, item_st, item_nb, used_blocks, xs, w_gu, w_gu, b_gu.reshape(n_e, 1, f2), b_gu.reshape(n_e, 1, f2),
      w_down, b_down.reshape(n_e, 1, d))


def _final_kernel(dcur_ref, dnxt_ref, gate_ref, x1_ref, p_ref, y_hbm,
                  g2_ref, b2_ref, wpg_ref, bpg_ref, wple_ref, g3_ref, b3_ref,
                  o_ref, ybuf, ffn_ref, gbuf, sem, *, alpha, slab):
    tm = x1_ref.shape[0]
    half = slab * LANES
    i = pl.program_id(0)
    n_steps = pl.num_programs(0)
    slot = i % 2
    slot_tokens = TOP_K * tm
    slot_rows = slot_tokens * slab

    def gather(d_ref, into):
        def start(t, c):
            for k in range(TOP_K):
                src = d_ref[k, t]
                dst = (into * slot_tokens + k * tm + t) * slab
                pltpu.make_async_copy(
                    y_hbm.at[pl.ds(pl.multiple_of(src * slab, slab), slab)],
                    ybuf.at[pl.ds(pl.multiple_of(dst, slab), slab)], sem.at[into],
                ).start(priority=k % 2)
            return c
        lax.fori_loop(0, tm, start, 0, unroll=4)

    @pl.when(i == 0)
    def _():
        gather(dcur_ref, slot)

    @pl.when(i + 1 < n_steps)
    def _():
        gather(dnxt_ref, 1 - slot)

    pltpu.make_async_copy(
        y_hbm.at[pl.ds(0, slot_rows)],
        ybuf.at[pl.ds(pl.multiple_of(slot * slot_rows, slot_rows), slot_rows)], sem.at[slot]).wait()

    for k in range(TOP_K):
        gbuf[k] = jnp.broadcast_to(gate_ref[:, k:k + 1], (tm, LANES))

    hr = tm // FINAL_SPLIT
    for part in range(FINAL_SPLIT):
        rs = slice(part * hr, (part + 1) * hr)
        for col in range(slab):
            acc_hi = acc_lo = None
            for k in range(TOP_K):
                hi, lo = _load_slab_column(
                    ybuf, slot * slot_tokens + k * tm + part * hr, hr, slab, col)
                g = gbuf[k, rs, :]
                acc_hi = hi * g if acc_hi is None else acc_hi + hi * g
                acc_lo = lo * g if acc_lo is None else acc_lo + lo * g
            ffn_ref[rs, col * LANES:(col + 1) * LANES] = acc_hi
            ffn_ref[rs, half + col * LANES:half + (col + 1) * LANES] = acc_lo
        h2 = _layer_norm(alpha * x1_ref[rs, :] + ffn_ref[rs, :], g2_ref[...], b2_ref[...])
        pg = jax.nn.sigmoid(_dot(h2.astype(BF16), wpg_ref[...]) + bpg_ref[...])
        pe = _dot(p_ref[rs, :].astype(BF16), wple_ref[...])
        o_ref[rs, :] = _layer_norm(alpha * h2 + pg * pe, g3_ref[...], b3_ref[...])


def _final(dest, gates_t, x1, p2, y, g2, b2, wpg, bpg, wple, g3, b3, cap, *, alpha):
    n, d = x1.shape
    slab = y.shape[0] // cap
    tm = MIX_TM
    n_steps = n // tm
    dest_cur = pl.BlockSpec((TOP_K, tm), lambda i: (0, i), memory_space=pltpu.SMEM)
    dest_next = pl.BlockSpec((TOP_K, tm), lambda i: (0, jnp.minimum(i + 1, n_steps - 1)),
                             memory_space=pltpu.SMEM)
    const = lambda i: (0, 0)
    row_blk = lambda width: pl.BlockSpec((tm, width), lambda i: (i, 0))
    return pl.pallas_call(
        functools.partial(_final_kernel, alpha=alpha, slab=slab),
        grid_spec=pltpu.PrefetchScalarGridSpec(
            num_scalar_prefetch=0,
            grid=(n_steps,),
            in_specs=[dest_cur, dest_next, row_blk(TOP_K), row_blk(d), row_blk(p2.shape[1]),
                      pl.BlockSpec(memory_space=pl.ANY),
                      pl.BlockSpec(g2.shape, const), pl.BlockSpec(b2.shape, const),
                      pl.BlockSpec(wpg.shape, const), pl.BlockSpec(bpg.shape, const),
                      pl.BlockSpec(wple.shape, const),
                      pl.BlockSpec(g3.shape, const), pl.BlockSpec(b3.shape, const)],
            out_specs=row_blk(d),
            scratch_shapes=[pltpu.VMEM((2 * TOP_K * tm * slab, LANES), U32),
                            pltpu.VMEM((tm, d), F32),
                            pltpu.VMEM((TOP_K, tm, LANES), F32),
                            pltpu.SemaphoreType.DMA((2,))],
        ),
        out_shape=jax.ShapeDtypeStruct((n, d), F32),
        compiler_params=pltpu.CompilerParams(
            dimension_semantics=("arbitrary",),
            vmem_limit_bytes=VMEM_LIMIT_BYTES),
        name="final",
    )(dest, dest, gates_t, x1, p2, y, g2, b2, wpg, bpg, wple, g3, b3)


def _plan(counts, n_items):
    nblk = (counts + ROW_BLK - 1) // ROW_BLK
    blk_end = jnp.cumsum(nblk)
    blk_start = blk_end - nblk
    items_per = (nblk + ITEM_BLKS - 1) // ITEM_BLKS
    item_end = jnp.cumsum(items_per)
    item_start = item_end - items_per
    total = item_end[-1]
    w = jnp.arange(n_items, dtype=jnp.int32)
    w_eff = jnp.minimum(w, total - 1)
    e = jnp.sum((item_end[None, :] <= w_eff[:, None]).astype(jnp.int32), axis=1)
    e = jnp.minimum(e, N_EXPERTS - 1)
    j = w_eff - item_start[e]
    st = blk_start[e] + j * ITEM_BLKS
    nb = jnp.where(w < total, jnp.clip(nblk[e] - j * ITEM_BLKS, 0, ITEM_BLKS), 0)
    return ((blk_start * ROW_BLK).astype(jnp.int32), (blk_end * ROW_BLK).astype(jnp.int32),
            e, st.astype(jnp.int32), nb.astype(jnp.int32), total.reshape(1).astype(jnp.int32))


def kernel(x, p, w_in, w_pool, pool_scale, conv_w, w_br_a, w_br_b, w_o, ln1_g, ln1_b,
           w_router, b_router, w_gu, b_gu, w_down, b_down, ln2_g, ln2_b,
           w_pg, b_pg, w_ple, ln3_g, ln3_b):
    depth = w_in.shape[0]
    bsz, seq, d = x.shape
    n = bsz * seq
    alpha = (2.0 * depth) ** 0.25
    nk = n * TOP_K
    n_row_blocks = (nk + N_EXPERTS * (ROW_BLK - 1) + ROW_BLK - 1) // ROW_BLK
    cap = n_row_blocks * ROW_BLK
    n_items = N_EXPERTS + n_row_blocks // ITEM_BLKS
    row = lambda v: v.reshape(1, -1)

    h = x.reshape(n, d)
    for i in range(depth):
        proj = _proj(h, w_in[i])
        x1, x1p, tope, gates, rank, cnt = _mixer(
            proj, h, w_pool[i].astype(BF16), row(pool_scale[i]), conv_w[i],
            w_br_a[i].astype(BF16), w_br_b[i].astype(BF16), w_o[i].astype(BF16),
            row(ln1_g[i]), row(ln1_b[i]), w_router[i].T.astype(BF16),
            b_router[i].reshape(N_EXPERTS, 1), seq=seq, alpha=alpha)
        pad_starts, pad_ends, item_e, item_st, item_nb, used_items = _plan(cnt[:, 0], n_items)
        dest = _dest(pad_starts, tope, rank)
        xs = _dispatch(pad_starts, pad_ends, dest, x1p, n, cap)
        used_blocks = pad_ends[N_EXPERTS - 1:] // ROW_BLK
        y = _ffn(item_e, item_st, item_nb, used_items, used_blocks, xs,
                 w_gu[i], b_gu[i], w_down[i], b_down[i], cap)
        h = _final(dest, gates.T, x1, p[i].reshape(n, -1), y,
                   row(ln2_g[i]), row(ln2_b[i]), w_pg[i].astype(BF16), row(b_pg[i]),
                   w_ple[i].astype(BF16), row(ln3_g[i]), row(ln3_b[i]), cap, alpha=alpha)
    return h.reshape(bsz, seq, d)
```

```python
import functools

import jax
import jax.numpy as jnp
import numpy as np
from jax import lax
from jax.experimental import pallas as pl
from jax.experimental.pallas import tpu as pltpu

F32 = jnp.float32
BF16 = jnp.bfloat16
U32 = jnp.uint32

POOL_WINDOWS = (2, 4, 8, 16)
N_POOL_GROUPS = 4
CONV_K = 3
N_EXPERTS = 32
TOP_K = 4
SWIGLU_LIMIT = 7.0
SWIGLU_ALPHA = 1.702
LN_EPS = 1e-5

LANES = 128
VMEM_LIMIT_BYTES = 56 * 1024 * 1024

HALO = 16
POOL_PAD = LANES
PROJ_BM = 1024
PROJ_BN = 1024
MIX_TM = 256
MIX_SPLIT = 1
FINAL_SPLIT = 2
DISP_TB = 512
ROW_BLK = 128
ITEM_BLKS = 10
FFN_TF = 512


def _layer_norm(v, g, b):
    mu = jnp.mean(v, axis=-1, keepdims=True)
    c = v - mu
    var = jnp.mean(c * c, axis=-1, keepdims=True)
    return c * lax.rsqrt(var + LN_EPS) * g + b


def _dot(a, b):
    return jnp.dot(a, b, preferred_element_type=F32)


def _sigmoid(v):
    return 0.5 * jnp.tanh(0.5 * v) + 0.5


def _pack_pairs(v):
    half = v.shape[1] // 2
    bits = pltpu.bitcast(v.astype(BF16).astype(F32), U32)
    return (bits[:, 0:half] & U32(0xFFFF0000)) | (bits[:, half:] >> 16)


def _store_slabs(ref, row0, packed):
    rows, width = packed.shape
    s = width // LANES
    for c in range(s):
        ref[pl.ds(row0 * s + c, rows, stride=s), :] = packed[:, c * LANES:(c + 1) * LANES]


def _load_slab_column(ref, row0, rows, s, c):
    p = ref[pl.ds(row0 * s + c, rows, stride=s), :]
    return pltpu.bitcast(p & U32(0xFFFF0000), F32), pltpu.bitcast(p << 16, F32)


def _proj_kernel(x_ref, w_ref, o_ref):
    o_ref[...] = _dot(x_ref[...].astype(BF16), w_ref[...].astype(BF16)).astype(o_ref.dtype)


def _proj(xb, wb):
    n, d = xb.shape
    n_in = wb.shape[1]
    return pl.pallas_call(
        _proj_kernel,
        grid=(n // PROJ_BM, n_in // PROJ_BN),
        in_specs=[pl.BlockSpec((PROJ_BM, d), lambda i, j: (i, 0)),
                  pl.BlockSpec((d, PROJ_BN), lambda i, j: (0, j))],
        out_specs=pl.BlockSpec((PROJ_BM, PROJ_BN), lambda i, j: (i, j)),
        out_shape=jax.ShapeDtypeStruct((n, n_in), BF16),
        compiler_params=pltpu.CompilerParams(
            dimension_semantics=("arbitrary", "arbitrary"),
            vmem_limit_bytes=VMEM_LIMIT_BYTES),
        name="proj",
    )(xb, wb)


def _mixer_kernel(proj_ref, halo_ref, x_ref, band_ref, wpool_ref, pscale_ref, convw_ref,
                  wbra_ref, wbrb_ref, wo_ref, g1_ref, b1_ref, wrt_ref, br_ref,
                  x1_ref, x1p_ref, tope_ref, gate_ref, rank_ref, cnt_ref,
                  ext_ref, cext_ref, carry_ref, *, seq, alpha, pw, cw, d):
    tm = x_ref.shape[0]
    i = pl.program_id(0)
    blocks_per_seq = seq // tm
    j = i % blocks_per_seq
    keep_halo = (j > 0).astype(F32)
    row = lax.broadcasted_iota(jnp.int32, (tm, 1), 0)
    pos1 = (j * tm + row + 1).astype(F32)

    @pl.when(i == 0)
    def _():
        carry_ref[...] = jnp.zeros_like(carry_ref)
        ext_ref[0:POOL_PAD - HALO, :] = jnp.zeros((POOL_PAD - HALO, pw), BF16)

    o1, o2, o3 = pw + cw, pw + 2 * cw, pw + 3 * cw
    ext_ref[POOL_PAD - HALO:POOL_PAD, :] = halo_ref[:, 0:pw] * keep_halo.astype(BF16)
    ext_ref[POOL_PAD:POOL_PAD + tm, :] = proj_ref[:, 0:pw]
    cext_ref[0:HALO, :] = (halo_ref[:, o1:o2].astype(F32) * halo_ref[:, o2:o3].astype(F32)) * keep_halo
    cext_ref[HALO:HALO + tm, :] = proj_ref[:, o1:o2].astype(F32) * proj_ref[:, o2:o3].astype(F32)

    gw = pw // N_POOL_GROUPS
    hr = tm // MIX_SPLIT
    logit_parts = []
    for part in range(MIX_SPLIT):
        r0 = part * hr
        rs = slice(r0, r0 + hr)

        a_parts = []
        for g, w in enumerate(POOL_WINDOWS):
            cols = slice(g * gw, (g + 1) * gw)
            s = _dot(band_ref[g, rs, :], ext_ref[:, cols])
            cnt = jnp.minimum(pos1[rs, :], float(w))
            pooled = s * (1.0 / cnt) - proj_ref[rs, cols].astype(F32)
            a_g = _dot(pooled.astype(BF16), wpool_ref[g]) * pscale_ref[:, cols]
            a_parts.append(a_g.astype(BF16))
        br_a = _dot(jnp.concatenate(a_parts, axis=1), wbra_ref[...])

        conv = cext_ref[HALO + r0:HALO + r0 + hr, :] * convw_ref[CONV_K - 1:CONV_K, :]
        for k in range(CONV_K - 1):
            sh = CONV_K - 1 - k
            conv = conv + cext_ref[HALO + r0 - sh:HALO + r0 - sh + hr, :] * convw_ref[k:k + 1, :]
        b = proj_ref[rs, pw:o1].astype(F32) * conv
        br_b = _dot(b.astype(BF16), wbrb_ref[...])

        g_a = proj_ref[rs, o3:o3 + d].astype(F32)
        g_b = proj_ref[rs, o3 + d:o3 + 2 * d].astype(F32)
        m = _sigmoid(g_a) * br_a + _sigmoid(g_b) * br_b
        mix = _dot(m.astype(BF16), wo_ref[...])
        x1 = _layer_norm(alpha * x_ref[rs, :] + mix, g1_ref[...], b1_ref[...])
        x1_ref[rs, :] = x1
        _store_slabs(x1p_ref, r0, _pack_pairs(x1))

        logit_parts.append(lax.dot_general(wrt_ref[...], x1.astype(BF16), (((1,), (1,)), ((), ())),
                                           preferred_element_type=F32))
    logits = jnp.concatenate(logit_parts, axis=1) + br_ref[...]
    e_iota = lax.broadcasted_iota(jnp.int32, logits.shape, 0)
    vals, idxs = [], []
    l = logits
    for _ in range(TOP_K):
        mx = jnp.max(l, axis=0, keepdims=True)
        ix = jnp.min(jnp.where(l == mx, e_iota, N_EXPERTS), axis=0, keepdims=True)
        vals.append(mx)
        idxs.append(ix)
        l = jnp.where(e_iota == ix, -jnp.inf, l)
    exps = [jnp.exp(v - vals[0]) for v in vals]
    denom = exps[0] + exps[1] + exps[2] + exps[3]
    onehot = jnp.zeros(logits.shape, F32)
    for k in range(TOP_K):
        tope_ref[k:k + 1, :] = idxs[k]
        gate_ref[k:k + 1, :] = exps[k] / denom
        onehot = onehot + (e_iota == idxs[k]).astype(F32)

    r_i = lax.broadcasted_iota(jnp.int32, (tm, tm), 0)
    c_i = lax.broadcasted_iota(jnp.int32, (tm, tm), 1)
    before = (r_i < c_i).astype(BF16)
    seen = _dot(onehot.astype(BF16), before) + carry_ref[:, 0:1]
    for k in range(TOP_K):
        rk = jnp.sum(jnp.where(e_iota == idxs[k], seen, 0.0), axis=0, keepdims=True)
        rank_ref[k:k + 1, :] = rk.astype(jnp.int32)
    carry_ref[...] = carry_ref[...] + jnp.sum(onehot, axis=1, keepdims=True)
    cnt_ref[...] = carry_ref[...].astype(jnp.int32)


def _mixer(proj, x2, wpool, pscale, convw, wbra, wbrb, wo, g1, b1, wrt, br, *, seq, alpha):
    n, d = x2.shape
    n_in = proj.shape[1]
    pw = wbra.shape[0]
    cw = wbrb.shape[0]
    tm = MIX_TM
    slab = d // 2 // LANES
    hb = tm // HALO
    const = lambda i: (0, 0)
    kern = functools.partial(_mixer_kernel, seq=seq, alpha=alpha, pw=pw, cw=cw, d=d)
    t_idx = np.arange(tm)[:, None] + POOL_PAD
    s_idx = np.arange(POOL_PAD + tm)[None, :]
    band = jnp.asarray(np.stack([(s_idx > t_idx - w) & (s_idx <= t_idx) for w in POOL_WINDOWS]), BF16)
    return pl.pallas_call(
        kern,
        grid=(n // tm,),
        in_specs=[
            pl.BlockSpec((tm, n_in), lambda i: (i, 0)),
            pl.BlockSpec((HALO, n_in // 2), lambda i: (jnp.maximum(i * hb - 1, 0), 0)),
            pl.BlockSpec((tm, d), lambda i: (i, 0)),
            pl.BlockSpec(band.shape, lambda i: (0, 0, 0)),
            pl.BlockSpec(wpool.shape, lambda i: (0, 0, 0)),
            pl.BlockSpec(pscale.shape, const),
            pl.BlockSpec(convw.shape, const),
            pl.BlockSpec(wbra.shape, const),
            pl.BlockSpec(wbrb.shape, const),
            pl.BlockSpec(wo.shape, const),
            pl.BlockSpec(g1.shape, const),
            pl.BlockSpec(b1.shape, const),
            pl.BlockSpec(wrt.shape, const),
            pl.BlockSpec(br.shape, const),
        ],
        out_specs=[
            pl.BlockSpec((tm, d), lambda i: (i, 0)),
            pl.BlockSpec((tm * slab, LANES), lambda i: (i, 0)),
            pl.BlockSpec((TOP_K, tm), lambda i: (0, i)),
            pl.BlockSpec((TOP_K, tm), lambda i: (0, i)),
            pl.BlockSpec((TOP_K, tm), lambda i: (0, i)),
            pl.BlockSpec((N_EXPERTS, LANES), const),
        ],
        out_shape=[
            jax.ShapeDtypeStruct((n, d), F32),
            jax.ShapeDtypeStruct((n * slab, LANES), U32),
            jax.ShapeDtypeStruct((TOP_K, n), jnp.int32),
            jax.ShapeDtypeStruct((TOP_K, n), F32),
            jax.ShapeDtypeStruct((TOP_K, n), jnp.int32),
            jax.ShapeDtypeStruct((N_EXPERTS, LANES), jnp.int32),
        ],
        scratch_shapes=[pltpu.VMEM((POOL_PAD + tm, pw), BF16),
                        pltpu.VMEM((HALO + tm, cw), F32),
                        pltpu.VMEM((N_EXPERTS, LANES), F32)],
        compiler_params=pltpu.CompilerParams(
            dimension_semantics=("arbitrary",),
            vmem_limit_bytes=VMEM_LIMIT_BYTES),
        name="mixer",
    )(proj, proj, x2, band, wpool, pscale, convw, wbra, wbrb, wo, g1, b1, wrt, br)


def _dest_kernel(pst_ref, tope_ref, rank_ref, dest_ref):
    tope = tope_ref[...]
    dest = rank_ref[...]
    for e in range(N_EXPERTS):
        dest = dest + jnp.where(tope == e, pst_ref[e], 0)
    dest_ref[...] = dest


def _dest(pad_starts, tope, rank):
    whole = pl.BlockSpec(tope.shape, lambda i, *_: (0, 0))
    return pl.pallas_call(
        _dest_kernel,
        grid_spec=pltpu.PrefetchScalarGridSpec(
            num_scalar_prefetch=1, grid=(1,), in_specs=[whole, whole], out_specs=whole),
        out_shape=jax.ShapeDtypeStruct(tope.shape, jnp.int32),
        name="dest",
    )(pad_starts, tope, rank)


def _dispatch_kernel(pst_ref, pend_ref, dest_ref, x1p_hbm, xs_hbm, xbuf, zero_ref, ld_sem, sem, zsem,
                     *, slab, n_steps):
    tb = dest_ref.shape[1]
    i = pl.program_id(0)
    blk = ROW_BLK * slab
    chunk = tb * slab

    def load(c):
        row0 = c * chunk if isinstance(c, int) else pl.multiple_of(c * chunk, chunk)
        rows = pl.ds(row0, chunk)
        return pltpu.make_async_copy(x1p_hbm.at[rows], xbuf.at[rows], ld_sem.at[c])

    def zero_copy(b):
        return pltpu.make_async_copy(
            zero_ref, xs_hbm.at[pl.ds(pl.multiple_of(b * blk, blk), blk)], zsem)

    @pl.when(i == 0)
    def _():
        zero_ref[...] = jnp.zeros_like(zero_ref)

        def start(e, c):
            @pl.when(pend_ref[e] > pst_ref[e])
            def _():
                zero_copy(pend_ref[e] // ROW_BLK - 1).start()
            return c

        def wait(e, c):
            @pl.when(pend_ref[e] > pst_ref[e])
            def _():
                zero_copy(pend_ref[e] // ROW_BLK - 1).wait()
            return c

        def start_tail(b, c):
            zero_copy(b).start()
            return c

        def wait_tail(b, c):
            zero_copy(b).wait()
            return c

        used = pend_ref[N_EXPERTS - 1] // ROW_BLK
        total = xs_hbm.shape[0] // blk
        lax.fori_loop(0, N_EXPERTS, start, 0)
        lax.fori_loop(used, total, start_tail, 0)
        for c in range(n_steps):
            load(c).start()
        lax.fori_loop(0, N_EXPERTS, wait, 0)
        lax.fori_loop(used, total, wait_tail, 0)

    load(i).wait()

    def start(t, c):
        src = xbuf.at[pl.ds(pl.multiple_of((i * tb + t) * slab, slab), slab)]
        for k in range(TOP_K):
            dst = dest_ref[k, t]
            pltpu.make_async_copy(
                src, xs_hbm.at[pl.ds(pl.multiple_of(dst * slab, slab), slab)], sem,
            ).start(priority=k % 2)
        return c

    lax.fori_loop(0, tb, start, 0, unroll=4)

    def wait_one_step():
        for k in range(TOP_K):
            pltpu.make_async_copy(
                xbuf.at[pl.ds(0, chunk)], xs_hbm.at[pl.ds(0, chunk)], sem).wait()

    @pl.when(i > 0)
    def _():
        wait_one_step()

    @pl.when(i == n_steps - 1)
    def _():
        wait_one_step()


def _dispatch(pad_starts, pad_ends, dest, x1p, n, cap):
    slab = x1p.shape[0] // n
    tb = DISP_TB
    n_steps = n // tb
    smem_blk = pl.BlockSpec((TOP_K, tb), lambda i, *_: (0, i), memory_space=pltpu.SMEM)
    return pl.pallas_call(
        functools.partial(_dispatch_kernel, slab=slab, n_steps=n_steps),
        grid_spec=pltpu.PrefetchScalarGridSpec(
            num_scalar_prefetch=2,
            grid=(n_steps,),
            in_specs=[smem_blk, pl.BlockSpec(memory_space=pl.ANY)],
            out_specs=pl.BlockSpec(memory_space=pl.ANY),
            scratch_shapes=[pltpu.VMEM(x1p.shape, U32),
                            pltpu.VMEM((ROW_BLK * slab, LANES), U32),
                            pltpu.SemaphoreType.DMA((n_steps,)),
                            pltpu.SemaphoreType.DMA(()),
                            pltpu.SemaphoreType.DMA(())],
        ),
        out_shape=jax.ShapeDtypeStruct((cap * slab, LANES), U32),
        compiler_params=pltpu.CompilerParams(
            dimension_semantics=("arbitrary",),
            vmem_limit_bytes=VMEM_LIMIT_BYTES,
            has_side_effects=True),
        name="dispatch",
    )(pad_starts, pad_ends, dest, x1p)


def _ffn_kernel(ie_ref, ist_ref, inb_ref, used_ref,
                xs_hbm, wg_ref, wu_ref, bg_ref, bu_ref, wd_ref, bd_ref,
                y_hbm,
                slabs, xbuf, yacc, sem_in, sem_out, *, slab):
    w = pl.program_id(0)
    f = pl.program_id(1)
    n_items = pl.num_programs(0)
    nf = pl.num_programs(1)
    nb = inb_ref[w]
    blk = ROW_BLK * slab
    half = slab * LANES
    slot_rows = slabs.shape[0] // 2
    first_step = f == 0
    last_step = f == nf - 1
    prev_item = jnp.maximum(w - 1, 0)
    next_item = jnp.minimum(w + 1, n_items - 1)

    def rows(r):
        return pl.ds(pl.multiple_of(r * ROW_BLK, ROW_BLK), ROW_BLK)

    def slot_row0(item):
        return (item % 2) * (slot_rows // slab)

    def hbm_blk(ref, item, r):
        return ref.at[pl.ds(pl.multiple_of((ist_ref[item] + r) * blk, blk), blk)]

    def vmem_blk(item, r):
        return slabs.at[pl.ds(pl.multiple_of((item % 2) * slot_rows + r * blk, blk), blk)]

    def for_blocks(item, fn):
        def body(r, c):
            fn(r)
            return c
        lax.fori_loop(0, inb_ref[item], body, 0)

    def load(item):
        return lambda r: pltpu.make_async_copy(hbm_blk(xs_hbm, item, r), vmem_blk(item, r), sem_in)

    def store(item):
        return lambda r: pltpu.make_async_copy(
            vmem_blk(item, r), hbm_blk(y_hbm, item, r), sem_out.at[item % 2])

    @pl.when((w == 0) & first_step)
    def _():
        slabs[0:blk, :] = jnp.zeros((blk, LANES), U32)

        def tail_copy(b):
            return pltpu.make_async_copy(
                slabs.at[pl.ds(0, blk)], y_hbm.at[pl.ds(pl.multiple_of(b * blk, blk), blk)],
                sem_out.at[0])

        def start(b, c):
            tail_copy(b).start()
            return c

        def wait(b, c):
            tail_copy(b).wait()
            return c

        total = y_hbm.shape[0] // blk
        lax.fori_loop(used_ref[0], total, start, 0)
        lax.fori_loop(used_ref[0], total, wait, 0)
        for_blocks(w, lambda r: load(w)(r).start())

    @pl.when(last_step)
    def _():
        @pl.when(w >= 1)
        def _():
            for_blocks(prev_item, lambda r: store(prev_item)(r).wait())

        @pl.when(w + 1 < n_items)
        def _():
            for_blocks(next_item, lambda r: load(next_item)(r).start())

    @pl.when(nb > 0)
    def _():
        n_quads = lax.shift_right_logical(nb, 2)
        rem0 = n_quads * 4

        def unpack(row0, n_rows):
            rs = pl.ds(pl.multiple_of(row0, ROW_BLK), n_rows)
            for col in range(slab):
                hi, lo = _load_slab_column(slabs, slot_row0(w) + row0, n_rows, slab, col)
                xbuf[rs, col * LANES:(col + 1) * LANES] = hi.astype(BF16)
                xbuf[rs, half + col * LANES:half + (col + 1) * LANES] = lo.astype(BF16)

        def pack_and_store(blk0, n_blocks, acc):
            _store_slabs(slabs, slot_row0(w) + blk0 * ROW_BLK, _pack_pairs(acc))
            for r in range(n_blocks):
                store(w)(blk0 + r).start()

        def sub(blk0, n_blocks, mode):
            row0 = blk0 * ROW_BLK
            n_rows = n_blocks * ROW_BLK
            rs = pl.ds(pl.multiple_of(row0, ROW_BLK), n_rows)
            if mode == "first":
                unpack(row0, n_rows)
            xt = xbuf[rs, :]
            gate = jnp.minimum(_dot(xt, wg_ref[...]) + bg_ref[...], SWIGLU_LIMIT)
            up = jnp.clip(_dot(xt, wu_ref[...]) + bu_ref[...], -SWIGLU_LIMIT, SWIGLU_LIMIT)
            act = (up + 1.0) * (gate * _sigmoid(SWIGLU_ALPHA * gate))
            contrib = _dot(act.astype(BF16), wd_ref[...])
            if mode == "first":
                yacc[rs, :] = contrib + bd_ref[...]
            elif mode == "last":
                pack_and_store(blk0, n_blocks, yacc[rs, :] + contrib)
            else:
                yacc[rs, :] += contrib

        def quads(mode):
            def body(q, c):
                sub(q * 4, 4, mode)
                return c
            lax.fori_loop(0, n_quads, body, 0)

        @pl.when(first_step)
        def _():
            for_blocks(w, lambda r: load(w)(r).wait())

            def prepare(r, c):
                unpack(r * ROW_BLK, ROW_BLK)
                yacc[rows(r), :] = jnp.broadcast_to(bd_ref[...], (ROW_BLK, yacc.shape[1]))
                return c

            lax.fori_loop(rem0, nb, prepare, 0)
            quads("first")

        @pl.when(jnp.logical_not(first_step | last_step))
        def _():
            quads("mid")

        @pl.when(last_step)
        def _():
            quads("last")

        @pl.when((nb & 2) != 0)
        def _():
            sub(rem0, 2, "mid")

        @pl.when((nb & 1) != 0)
        def _():
            sub(nb - 1, 1, "mid")

        @pl.when(last_step)
        def _():
            def finish(r, c):
                pack_and_store(r, 1, yacc[rows(r), :])
                return c

            lax.fori_loop(rem0, nb, finish, 0)

    @pl.when(last_step & (w == n_items - 1))
    def _():
        for_blocks(w, lambda r: store(w)(r).wait())


def _ffn(item_e, item_st, item_nb, used_items, used_blocks, xs, w_gu, b_gu, w_down, b_down, cap):
    slab = xs.shape[0] // cap
    n_e, d, f2 = w_gu.shape
    dff = f2 // 2
    tf = FFN_TF
    nf = dff // tf
    assert nf >= 2, "the ffn kernel prefetches the next item during a step that is not the first"
    n_items = item_e.shape[0]
    ts = ITEM_BLKS * ROW_BLK

    def fsel(w, f, inb):
        return jnp.where(inb[w] > 0, f, nf - 1)

    wg_spec = pl.BlockSpec((None, d, tf), lambda w, f, ie, ist, inb, u: (ie[w], 0, fsel(w, f, inb)))
    wu_spec = pl.BlockSpec((None, d, tf), lambda w, f, ie, ist, inb, u: (ie[w], 0, nf + fsel(w, f, inb)))
    bg_spec = pl.BlockSpec((None, 1, tf), lambda w, f, ie, ist, inb, u: (ie[w], 0, fsel(w, f, inb)))
    bu_spec = pl.BlockSpec((None, 1, tf), lambda w, f, ie, ist, inb, u: (ie[w], 0, nf + fsel(w, f, inb)))
    wd_spec = pl.BlockSpec((None, tf, d), lambda w, f, ie, ist, inb, u: (ie[w], fsel(w, f, inb), 0))
    bd_spec = pl.BlockSpec((None, 1, d), lambda w, f, ie, ist, inb, u: (ie[w], 0, 0))
    return pl.pallas_call(
        functools.partial(_ffn_kernel, slab=slab),
        grid_spec=pltpu.PrefetchScalarGridSpec(
            num_scalar_prefetch=4,
            grid=(used_items[0], nf),
            in_specs=[pl.BlockSpec(memory_space=pl.ANY),
                      wg_spec, wu_spec, bg_spec, bu_spec, wd_spec, bd_spec],
            out_specs=pl.BlockSpec(memory_space=pl.ANY),
            scratch_shapes=[pltpu.VMEM((2 * ts * slab, LANES), U32),
                            pltpu.VMEM((ts, d), BF16),
                            pltpu.VMEM((ts, d), F32),
                            pltpu.SemaphoreType.DMA(()),
                            pltpu.SemaphoreType.DMA((2,))],
        ),
        out_shape=jax.ShapeDtypeStruct((cap * slab, LANES), U32),
        compiler_params=pltpu.CompilerParams(
            dimension_semantics=("arbitrary", "arbitrary"),
            vmem_limit_bytes=VMEM_LIMIT_BYTES,
            has_side_effects=True),
        name="ffn",
    )(item_e, item_st, item_nb, used_blocks, xs, w_gu, w_gu, b_gu.reshape(n_e, 1, f2), b_gu.reshape(n_e, 1, f2),
      w_down, b_down.reshape(n_e, 1, d))


def _final_kernel(dcur_ref, dnxt_ref, gate_ref, x1_ref, p_ref, y_hbm,
                  g2_ref, b2_ref, wpg_ref, bpg_ref, wple_ref, g3_ref, b3_ref,
                  o_ref, ybuf, ffn_ref, gbuf, sem, *, alpha, slab):
    tm = x1_ref.shape[0]
    half = slab * LANES
    i = pl.program_id(0)
    n_steps = pl.num_programs(0)
    slot = i % 2
    slot_tokens = TOP_K * tm
    slot_rows = slot_tokens * slab

    def gather(d_ref, into):
        def start(t, c):
            for k in range(TOP_K):
                src = d_ref[k, t]
                dst = (into * slot_tokens + k * tm + t) * slab
                pltpu.make_async_copy(
                    y_hbm.at[pl.ds(pl.multiple_of(src * slab, slab), slab)],
                    ybuf.at[pl.ds(pl.multiple_of(dst, slab), slab)], sem.at[into],
                ).start(priority=k % 2)
            return c
        lax.fori_loop(0, tm, start, 0, unroll=4)

    @pl.when(i == 0)
    def _():
        gather(dcur_ref, slot)

    @pl.when(i + 1 < n_steps)
    def _():
        gather(dnxt_ref, 1 - slot)

    pltpu.make_async_copy(
        y_hbm.at[pl.ds(0, slot_rows)],
        ybuf.at[pl.ds(pl.multiple_of(slot * slot_rows, slot_rows), slot_rows)], sem.at[slot]).wait()

    for k in range(TOP_K):
        gbuf[k] = jnp.broadcast_to(gate_ref[:, k:k + 1], (tm, LANES))

    hr = tm // FINAL_SPLIT
    for part in range(FINAL_SPLIT):
        rs = slice(part * hr, (part + 1) * hr)
        for col in range(slab):
            acc_hi = acc_lo = None
            for k in range(TOP_K):
                hi, lo = _load_slab_column(
                    ybuf, slot * slot_tokens + k * tm + part * hr, hr, slab, col)
                g = gbuf[k, rs, :]
                acc_hi = hi * g if acc_hi is None else acc_hi + hi * g
                acc_lo = lo * g if acc_lo is None else acc_lo + lo * g
            ffn_ref[rs, col * LANES:(col + 1) * LANES] = acc_hi
            ffn_ref[rs, half + col * LANES:half + (col + 1) * LANES] = acc_lo
        h2 = _layer_norm(alpha * x1_ref[rs, :] + ffn_ref[rs, :], g2_ref[...], b2_ref[...])
        pg = _sigmoid(_dot(h2.astype(BF16), wpg_ref[...]) + bpg_ref[...])
        pe = _dot(p_ref[rs, :].astype(BF16), wple_ref[...])
        o_ref[rs, :] = _layer_norm(alpha * h2 + pg * pe, g3_ref[...], b3_ref[...])


def _final(dest, gates_t, x1, p2, y, g2, b2, wpg, bpg, wple, g3, b3, cap, *, alpha):
    n, d = x1.shape
    slab = y.shape[0] // cap
    tm = MIX_TM
    n_steps = n // tm
    dest_cur = pl.BlockSpec((TOP_K, tm), lambda i: (0, i), memory_space=pltpu.SMEM)
    dest_next = pl.BlockSpec((TOP_K, tm), lambda i: (0, jnp.minimum(i + 1, n_steps - 1)),
                             memory_space=pltpu.SMEM)
    const = lambda i: (0, 0)
    row_blk = lambda width: pl.BlockSpec((tm, width), lambda i: (i, 0))
    return pl.pallas_call(
        functools.partial(_final_kernel, alpha=alpha, slab=slab),
        grid_spec=pltpu.PrefetchScalarGridSpec(
            num_scalar_prefetch=0,
            grid=(n_steps,),
            in_specs=[dest_cur, dest_next, row_blk(TOP_K), row_blk(d), row_blk(p2.shape[1]),
                      pl.BlockSpec(memory_space=pl.ANY),
                      pl.BlockSpec(g2.shape, const), pl.BlockSpec(b2.shape, const),
                      pl.BlockSpec(wpg.shape, const), pl.BlockSpec(bpg.shape, const),
                      pl.BlockSpec(wple.shape, const),
                      pl.BlockSpec(g3.shape, const), pl.BlockSpec(b3.shape, const)],
            out_specs=row_blk(d),
            scratch_shapes=[pltpu.VMEM((2 * TOP_K * tm * slab, LANES), U32),
                            pltpu.VMEM((tm, d), F32),
                            pltpu.VMEM((TOP_K, tm, LANES), F32),
                            pltpu.SemaphoreType.DMA((2,))],
        ),
        out_shape=jax.ShapeDtypeStruct((n, d), F32),
        compiler_params=pltpu.CompilerParams(
            dimension_semantics=("arbitrary",),
            vmem_limit_bytes=VMEM_LIMIT_BYTES),
        name="final",
    )(dest, dest, gates_t, x1, p2, y, g2, b2, wpg, bpg, wple, g3, b3)


def _plan(counts, n_items):
    nblk = (counts + ROW_BLK - 1) // ROW_BLK
    blk_end = jnp.cumsum(nblk)
    blk_start = blk_end - nblk
    items_per = (nblk + ITEM_BLKS - 1) // ITEM_BLKS
    item_end = jnp.cumsum(items_per)
    item_start = item_end - items_per
    total = item_end[-1]
    w = jnp.arange(n_items, dtype=jnp.int32)
    w_eff = jnp.minimum(w, total - 1)
    e = jnp.sum((item_end[None, :] <= w_eff[:, None]).astype(jnp.int32), axis=1)
    e = jnp.minimum(e, N_EXPERTS - 1)
    j = w_eff - item_start[e]
    st = blk_start[e] + j * ITEM_BLKS
    nb = jnp.where(w < total, jnp.clip(nblk[e] - j * ITEM_BLKS, 0, ITEM_BLKS), 0)
    return ((blk_start * ROW_BLK).astype(jnp.int32), (blk_end * ROW_BLK).astype(jnp.int32),
            e, st.astype(jnp.int32), nb.astype(jnp.int32), total.reshape(1).astype(jnp.int32))


def kernel(x, p, w_in, w_pool, pool_scale, conv_w, w_br_a, w_br_b, w_o, ln1_g, ln1_b,
           w_router, b_router, w_gu, b_gu, w_down, b_down, ln2_g, ln2_b,
           w_pg, b_pg, w_ple, ln3_g, ln3_b):
    depth = w_in.shape[0]
    bsz, seq, d = x.shape
    n = bsz * seq
    alpha = (2.0 * depth) ** 0.25
    nk = n * TOP_K
    n_row_blocks = (nk + N_EXPERTS * (ROW_BLK - 1) + ROW_BLK - 1) // ROW_BLK
    cap = n_row_blocks * ROW_BLK
    n_items = N_EXPERTS + n_row_blocks // ITEM_BLKS
    row = lambda v: v.reshape(1, -1)

    h = x.reshape(n, d)
    for i in range(depth):
        proj = _proj(h, w_in[i])
        x1, x1p, tope, gates, rank, cnt = _mixer(
            proj, h, w_pool[i].astype(BF16), row(pool_scale[i]), conv_w[i],
            w_br_a[i].astype(BF16), w_br_b[i].astype(BF16), w_o[i].astype(BF16),
            row(ln1_g[i]), row(ln1_b[i]), w_router[i].T.astype(BF16),
            b_router[i].reshape(N_EXPERTS, 1), seq=seq, alpha=alpha)
        pad_starts, pad_ends, item_e, item_st, item_nb, used_items = _plan(cnt[:, 0], n_items)
        dest = _dest(pad_starts, tope, rank)
        xs = _dispatch(pad_starts, pad_ends, dest, x1p, n, cap)
        used_blocks = pad_ends[N_EXPERTS - 1:] // ROW_BLK
        y = _ffn(item_e, item_st, item_nb, used_items, used_blocks, xs,
                 w_gu[i], b_gu[i], w_down[i], b_down[i], cap)
        h = _final(dest, gates.T, x1, p[i].reshape(n, -1), y,
                   row(ln2_g[i]), row(ln2_b[i]), w_pg[i].astype(BF16), row(b_pg[i]),
                   w_ple[i].astype(BF16), row(ln3_g[i]), row(ln3_b[i]), cap, alpha=alpha)
    return h.reshape(bsz, seq, d)
```

```python
import functools

import jax
import jax.numpy as jnp
import numpy as np
from jax import lax
from jax.experimental import pallas as pl
from jax.experimental.pallas import tpu as pltpu

F32 = jnp.float32
BF16 = jnp.bfloat16
U32 = jnp.uint32

POOL_WINDOWS = (2, 4, 8, 16)
N_POOL_GROUPS = 4
CONV_K = 3
N_EXPERTS = 32
TOP_K = 4
SWIGLU_LIMIT = 7.0
SWIGLU_ALPHA = 1.702
LN_EPS = 1e-5

LANES = 128
VMEM_LIMIT_BYTES = 56 * 1024 * 1024

HALO = 16
POOL_PAD = LANES
PROJ_BM = 1024
PROJ_BN = 1024
MIX_TM = 256
MIX_SPLIT = 1
FINAL_SPLIT = 2
DISP_TB = 1024
ROW_BLK = 128
ITEM_BLKS = 10
FFN_TF = 512


def _layer_norm(v, g, b):
    mu = jnp.mean(v, axis=-1, keepdims=True)
    c = v - mu
    var = jnp.mean(c * c, axis=-1, keepdims=True)
    return c * lax.rsqrt(var + LN_EPS) * g + b


def _dot(a, b):
    return jnp.dot(a, b, preferred_element_type=F32)


def _sigmoid(v):
    return 0.5 * jnp.tanh(0.5 * v) + 0.5


def _pack_pairs(v):
    half = v.shape[1] // 2
    bits = pltpu.bitcast(v.astype(BF16).astype(F32), U32)
    return (bits[:, 0:half] & U32(0xFFFF0000)) | (bits[:, half:] >> 16)


def _store_slabs(ref, row0, packed):
    rows, width = packed.shape
    s = width // LANES
    for c in range(s):
        ref[pl.ds(row0 * s + c, rows, stride=s), :] = packed[:, c * LANES:(c + 1) * LANES]


def _load_slab_column(ref, row0, rows, s, c):
    p = ref[pl.ds(row0 * s + c, rows, stride=s), :]
    return pltpu.bitcast(p & U32(0xFFFF0000), F32), pltpu.bitcast(p << 16, F32)


def _proj_kernel(x_ref, w_ref, o_ref):
    o_ref[...] = _dot(x_ref[...].astype(BF16), w_ref[...].astype(BF16)).astype(o_ref.dtype)


def _proj(xb, wb):
    n, d = xb.shape
    n_in = wb.shape[1]
    return pl.pallas_call(
        _proj_kernel,
        grid=(n // PROJ_BM, n_in // PROJ_BN),
        in_specs=[pl.BlockSpec((PROJ_BM, d), lambda i, j: (i, 0)),
                  pl.BlockSpec((d, PROJ_BN), lambda i, j: (0, j))],
        out_specs=pl.BlockSpec((PROJ_BM, PROJ_BN), lambda i, j: (i, j)),
        out_shape=jax.ShapeDtypeStruct((n, n_in), BF16),
        compiler_params=pltpu.CompilerParams(
            dimension_semantics=("arbitrary", "arbitrary"),
            vmem_limit_bytes=VMEM_LIMIT_BYTES),
        name="proj",
    )(xb, wb)


def _mixer_kernel(proj_ref, halo_ref, x_ref, band_ref, wpool_ref, pscale_ref, convw_ref,
                  wbra_ref, wbrb_ref, wo_ref, g1_ref, b1_ref, wrt_ref, br_ref,
                  x1_ref, x1p_ref, tope_ref, gate_ref, rank_ref, cnt_ref,
                  ext_ref, cext_ref, carry_ref, *, seq, alpha, pw, cw, d):
    tm = x_ref.shape[0]
    i = pl.program_id(0)
    blocks_per_seq = seq // tm
    j = i % blocks_per_seq
    keep_halo = (j > 0).astype(F32)
    row = lax.broadcasted_iota(jnp.int32, (tm, 1), 0)
    pos1 = (j * tm + row + 1).astype(F32)

    @pl.when(i == 0)
    def _():
        carry_ref[...] = jnp.zeros_like(carry_ref)
        ext_ref[0:POOL_PAD - HALO, :] = jnp.zeros((POOL_PAD - HALO, pw), BF16)

    o1, o2, o3 = pw + cw, pw + 2 * cw, pw + 3 * cw
    ext_ref[POOL_PAD - HALO:POOL_PAD, :] = halo_ref[:, 0:pw] * keep_halo.astype(BF16)
    ext_ref[POOL_PAD:POOL_PAD + tm, :] = proj_ref[:, 0:pw]
    cext_ref[0:HALO, :] = (halo_ref[:, o1:o2].astype(F32) * halo_ref[:, o2:o3].astype(F32)) * keep_halo
    cext_ref[HALO:HALO + tm, :] = proj_ref[:, o1:o2].astype(F32) * proj_ref[:, o2:o3].astype(F32)

    gw = pw // N_POOL_GROUPS
    hr = tm // MIX_SPLIT
    logit_parts = []
    for part in range(MIX_SPLIT):
        r0 = part * hr
        rs = slice(r0, r0 + hr)

        a_parts = []
        for g, w in enumerate(POOL_WINDOWS):
            cols = slice(g * gw, (g + 1) * gw)
            s = _dot(band_ref[g, rs, :], ext_ref[:, cols])
            cnt = jnp.minimum(pos1[rs, :], float(w))
            pooled = s * (1.0 / cnt) - proj_ref[rs, cols].astype(F32)
            a_g = _dot(pooled.astype(BF16), wpool_ref[g]) * pscale_ref[:, cols]
            a_parts.append(a_g.astype(BF16))
        br_a = _dot(jnp.concatenate(a_parts, axis=1), wbra_ref[...])

        conv = cext_ref[HALO + r0:HALO + r0 + hr, :] * convw_ref[CONV_K - 1:CONV_K, :]
        for k in range(CONV_K - 1):
            sh = CONV_K - 1 - k
            conv = conv + cext_ref[HALO + r0 - sh:HALO + r0 - sh + hr, :] * convw_ref[k:k + 1, :]
        b = proj_ref[rs, pw:o1].astype(F32) * conv
        br_b = _dot(b.astype(BF16), wbrb_ref[...])

        g_a = proj_ref[rs, o3:o3 + d].astype(F32)
        g_b = proj_ref[rs, o3 + d:o3 + 2 * d].astype(F32)
        m = _sigmoid(g_a) * br_a + _sigmoid(g_b) * br_b
        mix = _dot(m.astype(BF16), wo_ref[...])
        x1 = _layer_norm(alpha * x_ref[rs, :] + mix, g1_ref[...], b1_ref[...])
        x1_ref[rs, :] = x1
        _store_slabs(x1p_ref, r0, _pack_pairs(x1))

        logit_parts.append(lax.dot_general(wrt_ref[...], x1.astype(BF16), (((1,), (1,)), ((), ())),
                                           preferred_element_type=F32))
    logits = jnp.concatenate(logit_parts, axis=1) + br_ref[...]
    e_iota = lax.broadcasted_iota(jnp.int32, logits.shape, 0)
    vals, idxs = [], []
    l = logits
    for _ in range(TOP_K):
        mx = jnp.max(l, axis=0, keepdims=True)
        ix = jnp.min(jnp.where(l == mx, e_iota, N_EXPERTS), axis=0, keepdims=True)
        vals.append(mx)
        idxs.append(ix)
        l = jnp.where(e_iota == ix, -jnp.inf, l)
    exps = [jnp.exp(v - vals[0]) for v in vals]
    denom = exps[0] + exps[1] + exps[2] + exps[3]
    onehot = jnp.zeros(logits.shape, F32)
    for k in range(TOP_K):
        tope_ref[k:k + 1, :] = idxs[k]
        gate_ref[k:k + 1, :] = exps[k] / denom
        onehot = onehot + (e_iota == idxs[k]).astype(F32)

    r_i = lax.broadcasted_iota(jnp.int32, (tm, tm), 0)
    c_i = lax.broadcasted_iota(jnp.int32, (tm, tm), 1)
    before = (r_i < c_i).astype(BF16)
    seen = _dot(onehot.astype(BF16), before) + carry_ref[:, 0:1]
    for k in range(TOP_K):
        rk = jnp.sum(jnp.where(e_iota == idxs[k], seen, 0.0), axis=0, keepdims=True)
        rank_ref[k:k + 1, :] = rk.astype(jnp.int32)
    carry_ref[...] = carry_ref[...] + jnp.sum(onehot, axis=1, keepdims=True)
    cnt_ref[...] = carry_ref[...].astype(jnp.int32)


def _mixer(proj, x2, wpool, pscale, convw, wbra, wbrb, wo, g1, b1, wrt, br, *, seq, alpha):
    n, d = x2.shape
    n_in = proj.shape[1]
    pw = wbra.shape[0]
    cw = wbrb.shape[0]
    tm = MIX_TM
    slab = d // 2 // LANES
    hb = tm // HALO
    const = lambda i: (0, 0)
    kern = functools.partial(_mixer_kernel, seq=seq, alpha=alpha, pw=pw, cw=cw, d=d)
    t_idx = np.arange(tm)[:, None] + POOL_PAD
    s_idx = np.arange(POOL_PAD + tm)[None, :]
    band = jnp.asarray(np.stack([(s_idx > t_idx - w) & (s_idx <= t_idx) for w in POOL_WINDOWS]), BF16)
    return pl.pallas_call(
        kern,
        grid=(n // tm,),
        in_specs=[
            pl.BlockSpec((tm, n_in), lambda i: (i, 0)),
            pl.BlockSpec((HALO, n_in // 2), lambda i: (jnp.maximum(i * hb - 1, 0), 0)),
            pl.BlockSpec((tm, d), lambda i: (i, 0)),
            pl.BlockSpec(band.shape, lambda i: (0, 0, 0)),
            pl.BlockSpec(wpool.shape, lambda i: (0, 0, 0)),
            pl.BlockSpec(pscale.shape, const),
            pl.BlockSpec(convw.shape, const),
            pl.BlockSpec(wbra.shape, const),
            pl.BlockSpec(wbrb.shape, const),
            pl.BlockSpec(wo.shape, const),
            pl.BlockSpec(g1.shape, const),
            pl.BlockSpec(b1.shape, const),
            pl.BlockSpec(wrt.shape, const),
            pl.BlockSpec(br.shape, const),
        ],
        out_specs=[
            pl.BlockSpec((tm, d), lambda i: (i, 0)),
            pl.BlockSpec((tm * slab, LANES), lambda i: (i, 0)),
            pl.BlockSpec((TOP_K, tm), lambda i: (0, i)),
            pl.BlockSpec((TOP_K, tm), lambda i: (0, i)),
            pl.BlockSpec((TOP_K, tm), lambda i: (0, i)),
            pl.BlockSpec((N_EXPERTS, LANES), const),
        ],
        out_shape=[
            jax.ShapeDtypeStruct((n, d), F32),
            jax.ShapeDtypeStruct((n * slab, LANES), U32),
            jax.ShapeDtypeStruct((TOP_K, n), jnp.int32),
            jax.ShapeDtypeStruct((TOP_K, n), F32),
            jax.ShapeDtypeStruct((TOP_K, n), jnp.int32),
            jax.ShapeDtypeStruct((N_EXPERTS, LANES), jnp.int32),
        ],
        scratch_shapes=[pltpu.VMEM((POOL_PAD + tm, pw), BF16),
                        pltpu.VMEM((HALO + tm, cw), F32),
                        pltpu.VMEM((N_EXPERTS, LANES), F32)],
        compiler_params=pltpu.CompilerParams(
            dimension_semantics=("arbitrary",),
            vmem_limit_bytes=VMEM_LIMIT_BYTES),
        name="mixer",
    )(proj, proj, x2, band, wpool, pscale, convw, wbra, wbrb, wo, g1, b1, wrt, br)


def _dest_kernel(pst_ref, tope_ref, rank_ref, dest_ref):
    tope = tope_ref[...]
    dest = rank_ref[...]
    for e in range(N_EXPERTS):
        dest = dest + jnp.where(tope == e, pst_ref[e], 0)
    dest_ref[...] = dest


def _dest(pad_starts, tope, rank):
    whole = pl.BlockSpec(tope.shape, lambda i, *_: (0, 0))
    return pl.pallas_call(
        _dest_kernel,
        grid_spec=pltpu.PrefetchScalarGridSpec(
            num_scalar_prefetch=1, grid=(1,), in_specs=[whole, whole], out_specs=whole),
        out_shape=jax.ShapeDtypeStruct(tope.shape, jnp.int32),
        name="dest",
    )(pad_starts, tope, rank)


def _dispatch_kernel(pst_ref, pend_ref, dest_ref, x1p_hbm, xs_hbm, xbuf, zero_ref, ld_sem, sem, zsem,
                     *, slab, n_steps):
    tb = dest_ref.shape[1]
    i = pl.program_id(0)
    blk = ROW_BLK * slab
    chunk = tb * slab

    def load(c):
        row0 = c * chunk if isinstance(c, int) else pl.multiple_of(c * chunk, chunk)
        rows = pl.ds(row0, chunk)
        return pltpu.make_async_copy(x1p_hbm.at[rows], xbuf.at[rows], ld_sem.at[c])

    def zero_copy(b):
        return pltpu.make_async_copy(
            zero_ref, xs_hbm.at[pl.ds(pl.multiple_of(b * blk, blk), blk)], zsem)

    @pl.when(i == 0)
    def _():
        zero_ref[...] = jnp.zeros_like(zero_ref)

        def start(e, c):
            @pl.when(pend_ref[e] > pst_ref[e])
            def _():
                zero_copy(pend_ref[e] // ROW_BLK - 1).start()
            return c

        def wait(e, c):
            @pl.when(pend_ref[e] > pst_ref[e])
            def _():
                zero_copy(pend_ref[e] // ROW_BLK - 1).wait()
            return c

        def start_tail(b, c):
            zero_copy(b).start()
            return c

        def wait_tail(b, c):
            zero_copy(b).wait()
            return c

        used = pend_ref[N_EXPERTS - 1] // ROW_BLK
        total = xs_hbm.shape[0] // blk
        lax.fori_loop(0, N_EXPERTS, start, 0)
        lax.fori_loop(used, total, start_tail, 0)
        for c in range(n_steps):
            load(c).start()
        lax.fori_loop(0, N_EXPERTS, wait, 0)
        lax.fori_loop(used, total, wait_tail, 0)

    load(i).wait()

    def start(t, c):
        src = xbuf.at[pl.ds(pl.multiple_of((i * tb + t) * slab, slab), slab)]
        for k in range(TOP_K):
            dst = dest_ref[k, t]
            pltpu.make_async_copy(
                src, xs_hbm.at[pl.ds(pl.multiple_of(dst * slab, slab), slab)], sem,
            ).start(priority=k % 2)
        return c

    lax.fori_loop(0, tb, start, 0, unroll=4)

    def wait_one_step():
        for k in range(TOP_K):
            pltpu.make_async_copy(
                xbuf.at[pl.ds(0, chunk)], xs_hbm.at[pl.ds(0, chunk)], sem).wait()

    @pl.when(i > 0)
    def _():
        wait_one_step()

    @pl.when(i == n_steps - 1)
    def _():
        wait_one_step()


def _dispatch(pad_starts, pad_ends, dest, x1p, n, cap):
    slab = x1p.shape[0] // n
    tb = DISP_TB
    n_steps = n // tb
    smem_blk = pl.BlockSpec((TOP_K, tb), lambda i, *_: (0, i), memory_space=pltpu.SMEM)
    return pl.pallas_call(
        functools.partial(_dispatch_kernel, slab=slab, n_steps=n_steps),
        grid_spec=pltpu.PrefetchScalarGridSpec(
            num_scalar_prefetch=2,
            grid=(n_steps,),
            in_specs=[smem_blk, pl.BlockSpec(memory_space=pl.ANY)],
            out_specs=pl.BlockSpec(memory_space=pl.ANY),
            scratch_shapes=[pltpu.VMEM(x1p.shape, U32),
                            pltpu.VMEM((ROW_BLK * slab, LANES), U32),
                            pltpu.SemaphoreType.DMA((n_steps,)),
                            pltpu.SemaphoreType.DMA(()),
                            pltpu.SemaphoreType.DMA(())],
        ),
        out_shape=jax.ShapeDtypeStruct((cap * slab, LANES), U32),
        compiler_params=pltpu.CompilerParams(
            dimension_semantics=("arbitrary",),
            vmem_limit_bytes=VMEM_LIMIT_BYTES,
            has_side_effects=True),
        name="dispatch",
    )(pad_starts, pad_ends, dest, x1p)


def _ffn_kernel(ie_ref, ist_ref, inb_ref, used_ref,
                xs_hbm, wg_ref, wu_ref, bgu_ref, wd_ref, bd_ref,
                y_hbm,
                slabs, xbuf, yacc, sem_in, sem_out, *, slab):
    w = pl.program_id(0)
    f = pl.program_id(1)
    n_items = pl.num_programs(0)
    nf = pl.num_programs(1)
    nb = inb_ref[w]
    blk = ROW_BLK * slab
    half = slab * LANES
    slot_rows = slabs.shape[0] // 2
    first_step = f == 0
    last_step = f == nf - 1
    prev_item = jnp.maximum(w - 1, 0)
    next_item = jnp.minimum(w + 1, n_items - 1)

    def rows(r):
        return pl.ds(pl.multiple_of(r * ROW_BLK, ROW_BLK), ROW_BLK)

    def slot_row0(item):
        return (item % 2) * (slot_rows // slab)

    def hbm_blk(ref, item, r):
        return ref.at[pl.ds(pl.multiple_of((ist_ref[item] + r) * blk, blk), blk)]

    def vmem_blk(item, r):
        return slabs.at[pl.ds(pl.multiple_of((item % 2) * slot_rows + r * blk, blk), blk)]

    def for_blocks(item, fn):
        def body(r, c):
            fn(r)
            return c
        lax.fori_loop(0, inb_ref[item], body, 0)

    def load(item):
        return lambda r: pltpu.make_async_copy(hbm_blk(xs_hbm, item, r), vmem_blk(item, r), sem_in)

    def store(item):
        return lambda r: pltpu.make_async_copy(
            vmem_blk(item, r), hbm_blk(y_hbm, item, r), sem_out.at[item % 2])

    @pl.when((w == 0) & first_step)
    def _():
        slabs[0:blk, :] = jnp.zeros((blk, LANES), U32)

        def tail_copy(b):
            return pltpu.make_async_copy(
                slabs.at[pl.ds(0, blk)], y_hbm.at[pl.ds(pl.multiple_of(b * blk, blk), blk)],
                sem_out.at[0])

        def start(b, c):
            tail_copy(b).start()
            return c

        def wait(b, c):
            tail_copy(b).wait()
            return c

        total = y_hbm.shape[0] // blk
        lax.fori_loop(used_ref[0], total, start, 0)
        lax.fori_loop(used_ref[0], total, wait, 0)
        for_blocks(w, lambda r: load(w)(r).start())

    @pl.when(last_step)
    def _():
        @pl.when(w >= 1)
        def _():
            for_blocks(prev_item, lambda r: store(prev_item)(r).wait())

        @pl.when(w + 1 < n_items)
        def _():
            for_blocks(next_item, lambda r: load(next_item)(r).start())

    @pl.when(nb > 0)
    def _():
        n_quads = lax.shift_right_logical(nb, 2)
        rem0 = n_quads * 4

        tf = wg_ref.shape[1]
        expert = ie_ref[w]
        bias_g = bgu_ref[expert, :, pl.ds(pl.multiple_of(f * tf, tf), tf)]
        bias_u = bgu_ref[expert, :, pl.ds(pl.multiple_of((nf + f) * tf, tf), tf)]
        bias_d = bd_ref[expert]

        def unpack(row0, n_rows):
            rs = pl.ds(pl.multiple_of(row0, ROW_BLK), n_rows)
            for col in range(slab):
                hi, lo = _load_slab_column(slabs, slot_row0(w) + row0, n_rows, slab, col)
                xbuf[rs, col * LANES:(col + 1) * LANES] = hi.astype(BF16)
                xbuf[rs, half + col * LANES:half + (col + 1) * LANES] = lo.astype(BF16)

        def pack_and_store(blk0, n_blocks, acc):
            _store_slabs(slabs, slot_row0(w) + blk0 * ROW_BLK, _pack_pairs(acc))
            for r in range(n_blocks):
                store(w)(blk0 + r).start()

        def sub(blk0, n_blocks, mode):
            row0 = blk0 * ROW_BLK
            n_rows = n_blocks * ROW_BLK
            rs = pl.ds(pl.multiple_of(row0, ROW_BLK), n_rows)
            if mode == "first":
                unpack(row0, n_rows)
            xt = xbuf[rs, :]
            gate = jnp.minimum(_dot(xt, wg_ref[...]) + bias_g, SWIGLU_LIMIT)
            up = jnp.clip(_dot(xt, wu_ref[...]) + bias_u, -SWIGLU_LIMIT, SWIGLU_LIMIT)
            act = (up + 1.0) * (gate * _sigmoid(SWIGLU_ALPHA * gate))
            contrib = _dot(act.astype(BF16), wd_ref[...])
            if mode == "first":
                yacc[rs, :] = contrib + bias_d
            elif mode == "last":
                pack_and_store(blk0, n_blocks, yacc[rs, :] + contrib)
            else:
                yacc[rs, :] += contrib

        def quads(mode):
            def body(q, c):
                sub(q * 4, 4, mode)
                return c
            lax.fori_loop(0, n_quads, body, 0)

        @pl.when(first_step)
        def _():
            for_blocks(w, lambda r: load(w)(r).wait())

            def prepare(r, c):
                unpack(r * ROW_BLK, ROW_BLK)
                yacc[rows(r), :] = jnp.broadcast_to(bias_d, (ROW_BLK, yacc.shape[1]))
                return c

            lax.fori_loop(rem0, nb, prepare, 0)
            quads("first")

        @pl.when(jnp.logical_not(first_step | last_step))
        def _():
            quads("mid")

        @pl.when(last_step)
        def _():
            quads("last")

        @pl.when((nb & 2) != 0)
        def _():
            sub(rem0, 2, "mid")

        @pl.when((nb & 1) != 0)
        def _():
            sub(nb - 1, 1, "mid")

        @pl.when(last_step)
        def _():
            def finish(r, c):
                pack_and_store(r, 1, yacc[rows(r), :])
                return c

            lax.fori_loop(rem0, nb, finish, 0)

    @pl.when(last_step & (w == n_items - 1))
    def _():
        for_blocks(w, lambda r: store(w)(r).wait())


def _ffn(item_e, item_st, item_nb, used_items, used_blocks, xs, w_gu, b_gu, w_down, b_down, cap):
    slab = xs.shape[0] // cap
    n_e, d, f2 = w_gu.shape
    dff = f2 // 2
    tf = FFN_TF
    nf = dff // tf
    assert nf >= 2, "the ffn kernel prefetches the next item during a step that is not the first"
    n_items = item_e.shape[0]
    ts = ITEM_BLKS * ROW_BLK

    def fsel(w, f, inb):
        return jnp.where(inb[w] > 0, f, nf - 1)

    wg_spec = pl.BlockSpec((None, d, tf), lambda w, f, ie, ist, inb, u: (ie[w], 0, fsel(w, f, inb)))
    wu_spec = pl.BlockSpec((None, d, tf), lambda w, f, ie, ist, inb, u: (ie[w], 0, nf + fsel(w, f, inb)))
    wd_spec = pl.BlockSpec((None, tf, d), lambda w, f, ie, ist, inb, u: (ie[w], fsel(w, f, inb), 0))
    resident = pl.BlockSpec(memory_space=pltpu.VMEM)
    return pl.pallas_call(
        functools.partial(_ffn_kernel, slab=slab),
        grid_spec=pltpu.PrefetchScalarGridSpec(
            num_scalar_prefetch=4,
            grid=(used_items[0], nf),
            in_specs=[pl.BlockSpec(memory_space=pl.ANY),
                      wg_spec, wu_spec, resident, wd_spec, resident],
            out_specs=pl.BlockSpec(memory_space=pl.ANY),
            scratch_shapes=[pltpu.VMEM((2 * ts * slab, LANES), U32),
                            pltpu.VMEM((ts, d), BF16),
                            pltpu.VMEM((ts, d), F32),
                            pltpu.SemaphoreType.DMA(()),
                            pltpu.SemaphoreType.DMA((2,))],
        ),
        out_shape=jax.ShapeDtypeStruct((cap * slab, LANES), U32),
        compiler_params=pltpu.CompilerParams(
            dimension_semantics=("arbitrary", "arbitrary"),
            vmem_limit_bytes=VMEM_LIMIT_BYTES,
            has_side_effects=True),
        name="ffn",
    )(item_e, item_st, item_nb, used_blocks, xs, w_gu, w_gu, b_gu.reshape(n_e, 1, f2),
      w_down, b_down.reshape(n_e, 1, d))


def _final_kernel(dcur_ref, dnxt_ref, gate_ref, x1_ref, p_ref, y_hbm,
                  g2_ref, b2_ref, wpg_ref, bpg_ref, wple_ref, g3_ref, b3_ref,
                  o_ref, ybuf, ffn_ref, gbuf, sem, *, alpha, slab):
    tm = x1_ref.shape[0]
    half = slab * LANES
    i = pl.program_id(0)
    n_steps = pl.num_programs(0)
    slot = i % 2
    slot_tokens = TOP_K * tm
    slot_rows = slot_tokens * slab

    def gather(d_ref, into):
        def start(t, c):
            for k in range(TOP_K):
                src = d_ref[k, t]
                dst = (into * slot_tokens + k * tm + t) * slab
                pltpu.make_async_copy(
                    y_hbm.at[pl.ds(pl.multiple_of(src * slab, slab), slab)],
                    ybuf.at[pl.ds(pl.multiple_of(dst, slab), slab)], sem.at[into],
                ).start(priority=k % 2)
            return c
        lax.fori_loop(0, tm, start, 0, unroll=4)

    @pl.when(i == 0)
    def _():
        gather(dcur_ref, slot)

    @pl.when(i + 1 < n_steps)
    def _():
        gather(dnxt_ref, 1 - slot)

    pltpu.make_async_copy(
        y_hbm.at[pl.ds(0, slot_rows)],
        ybuf.at[pl.ds(pl.multiple_of(slot * slot_rows, slot_rows), slot_rows)], sem.at[slot]).wait()

    for k in range(TOP_K):
        gbuf[k] = jnp.broadcast_to(gate_ref[:, k:k + 1], (tm, LANES))

    hr = tm // FINAL_SPLIT
    for part in range(FINAL_SPLIT):
        rs = slice(part * hr, (part + 1) * hr)
        for col in range(slab):
            acc_hi = acc_lo = None
            for k in range(TOP_K):
                hi, lo = _load_slab_column(
                    ybuf, slot * slot_tokens + k * tm + part * hr, hr, slab, col)
                g = gbuf[k, rs, :]
                acc_hi = hi * g if acc_hi is None else acc_hi + hi * g
                acc_lo = lo * g if acc_lo is None else acc_lo + lo * g
            ffn_ref[rs, col * LANES:(col + 1) * LANES] = acc_hi
            ffn_ref[rs, half + col * LANES:half + (col + 1) * LANES] = acc_lo
        h2 = _layer_norm(alpha * x1_ref[rs, :] + ffn_ref[rs, :], g2_ref[...], b2_ref[...])
        pg = _sigmoid(_dot(h2.astype(BF16), wpg_ref[...]) + bpg_ref[...])
        pe = _dot(p_ref[rs, :].astype(BF16), wple_ref[...])
        o_ref[rs, :] = _layer_norm(alpha * h2 + pg * pe, g3_ref[...], b3_ref[...])


def _final(dest, gates_t, x1, p2, y, g2, b2, wpg, bpg, wple, g3, b3, cap, *, alpha):
    n, d = x1.shape
    slab = y.shape[0] // cap
    tm = MIX_TM
    n_steps = n // tm
    dest_cur = pl.BlockSpec((TOP_K, tm), lambda i: (0, i), memory_space=pltpu.SMEM)
    dest_next = pl.BlockSpec((TOP_K, tm), lambda i: (0, jnp.minimum(i + 1, n_steps - 1)),
                             memory_space=pltpu.SMEM)
    const = lambda i: (0, 0)
    row_blk = lambda width: pl.BlockSpec((tm, width), lambda i: (i, 0))
    return pl.pallas_call(
        functools.partial(_final_kernel, alpha=alpha, slab=slab),
        grid_spec=pltpu.PrefetchScalarGridSpec(
            num_scalar_prefetch=0,
            grid=(n_steps,),
            in_specs=[dest_cur, dest_next, row_blk(TOP_K), row_blk(d), row_blk(p2.shape[1]),
                      pl.BlockSpec(memory_space=pl.ANY),
                      pl.BlockSpec(g2.shape, const), pl.BlockSpec(b2.shape, const),
                      pl.BlockSpec(wpg.shape, const), pl.BlockSpec(bpg.shape, const),
                      pl.BlockSpec(wple.shape, const),
                      pl.BlockSpec(g3.shape, const), pl.BlockSpec(b3.shape, const)],
            out_specs=row_blk(d),
            scratch_shapes=[pltpu.VMEM((2 * TOP_K * tm * slab, LANES), U32),
                            pltpu.VMEM((tm, d), F32),
                            pltpu.VMEM((TOP_K, tm, LANES), F32),
                            pltpu.SemaphoreType.DMA((2,))],
        ),
        out_shape=jax.ShapeDtypeStruct((n, d), F32),
        compiler_params=pltpu.CompilerParams(
            dimension_semantics=("arbitrary",),
            vmem_limit_bytes=VMEM_LIMIT_BYTES),
        name="final",
    )(dest, dest, gates_t, x1, p2, y, g2, b2, wpg, bpg, wple, g3, b3)


def _plan(counts, n_items):
    nblk = (counts + ROW_BLK - 1) // ROW_BLK
    blk_end = jnp.cumsum(nblk)
    blk_start = blk_end - nblk
    items_per = (nblk + ITEM_BLKS - 1) // ITEM_BLKS
    item_end = jnp.cumsum(items_per)
    item_start = item_end - items_per
    total = item_end[-1]
    w = jnp.arange(n_items, dtype=jnp.int32)
    w_eff = jnp.minimum(w, total - 1)
    e = jnp.sum((item_end[None, :] <= w_eff[:, None]).astype(jnp.int32), axis=1)
    e = jnp.minimum(e, N_EXPERTS - 1)
    j = w_eff - item_start[e]
    st = blk_start[e] + j * ITEM_BLKS
    nb = jnp.where(w < total, jnp.clip(nblk[e] - j * ITEM_BLKS, 0, ITEM_BLKS), 0)
    return ((blk_start * ROW_BLK).astype(jnp.int32), (blk_end * ROW_BLK).astype(jnp.int32),
            e, st.astype(jnp.int32), nb.astype(jnp.int32), total.reshape(1).astype(jnp.int32))


def kernel(x, p, w_in, w_pool, pool_scale, conv_w, w_br_a, w_br_b, w_o, ln1_g, ln1_b,
           w_router, b_router, w_gu, b_gu, w_down, b_down, ln2_g, ln2_b,
           w_pg, b_pg, w_ple, ln3_g, ln3_b):
    depth = w_in.shape[0]
    bsz, seq, d = x.shape
    n = bsz * seq
    alpha = (2.0 * depth) ** 0.25
    nk = n * TOP_K
    n_row_blocks = (nk + N_EXPERTS * (ROW_BLK - 1) + ROW_BLK - 1) // ROW_BLK
    cap = n_row_blocks * ROW_BLK
    n_items = N_EXPERTS + n_row_blocks // ITEM_BLKS
    row = lambda v: v.reshape(1, -1)

    h = x.reshape(n, d)
    for i in range(depth):
        proj = _proj(h, w_in[i])
        x1, x1p, tope, gates, rank, cnt = _mixer(
            proj, h, w_pool[i].astype(BF16), row(pool_scale[i]), conv_w[i],
            w_br_a[i].astype(BF16), w_br_b[i].astype(BF16), w_o[i].astype(BF16),
            row(ln1_g[i]), row(ln1_b[i]), w_router[i].T.astype(BF16),
            b_router[i].reshape(N_EXPERTS, 1), seq=seq, alpha=alpha)
        pad_starts, pad_ends, item_e, item_st, item_nb, used_items = _plan(cnt[:, 0], n_items)
        dest = _dest(pad_starts, tope, rank)
        xs = _dispatch(pad_starts, pad_ends, dest, x1p, n, cap)
        used_blocks = pad_ends[N_EXPERTS - 1:] // ROW_BLK
        y = _ffn(item_e, item_st, item_nb, used_items, used_blocks, xs,
                 w_gu[i], b_gu[i], w_down[i], b_down[i], cap)
        h = _final(dest, gates.T, x1, p[i].reshape(n, -1), y,
                   row(ln2_g[i]), row(ln2_b[i]), w_pg[i].astype(BF16), row(b_pg[i]),
                   w_ple[i].astype(BF16), row(ln3_g[i]), row(ln3_b[i]), cap, alpha=alpha)
    return h.reshape(bsz, seq, d)
```

```python
import functools

import jax
import jax.numpy as jnp
import numpy as np
from jax import lax
from jax.experimental import pallas as pl
from jax.experimental.pallas import tpu as pltpu

F32 = jnp.float32
BF16 = jnp.bfloat16
U32 = jnp.uint32

POOL_WINDOWS = (2, 4, 8, 16)
N_POOL_GROUPS = 4
CONV_K = 3
N_EXPERTS = 32
TOP_K = 4
SWIGLU_LIMIT = 7.0
SWIGLU_ALPHA = 1.702
LN_EPS = 1e-5

LANES = 128
VMEM_LIMIT_BYTES = 56 * 1024 * 1024

HALO = 16
POOL_PAD = LANES
PROJ_BM = 1024
PROJ_BN = 1024
MIX_TM = 256
MIX_SPLIT = 1
FINAL_SPLIT = 2
DISP_TB = 512
ROW_BLK = 128
ITEM_BLKS = 10
FFN_TF = 512


def _layer_norm(v, g, b):
    mu = jnp.mean(v, axis=-1, keepdims=True)
    c = v - mu
    var = jnp.mean(c * c, axis=-1, keepdims=True)
    return c * lax.rsqrt(var + LN_EPS) * g + b


def _dot(a, b):
    return jnp.dot(a, b, preferred_element_type=F32)


def _sigmoid(v):
    return 0.5 * jnp.tanh(0.5 * v) + 0.5


def _pack_pairs(v):
    half = v.shape[1] // 2
    bits = pltpu.bitcast(v.astype(BF16).astype(F32), U32)
    return (bits[:, 0:half] & U32(0xFFFF0000)) | (bits[:, half:] >> 16)


def _store_slabs(ref, row0, packed):
    rows, width = packed.shape
    s = width // LANES
    for c in range(s):
        ref[pl.ds(row0 * s + c, rows, stride=s), :] = packed[:, c * LANES:(c + 1) * LANES]


def _load_slab_column(ref, row0, rows, s, c):
    p = ref[pl.ds(row0 * s + c, rows, stride=s), :]
    return pltpu.bitcast(p & U32(0xFFFF0000), F32), pltpu.bitcast(p << 16, F32)


def _proj_kernel(x_ref, w_ref, o_ref):
    o_ref[...] = _dot(x_ref[...].astype(BF16), w_ref[...].astype(BF16)).astype(o_ref.dtype)


def _proj(xb, wb):
    n, d = xb.shape
    n_in = wb.shape[1]
    return pl.pallas_call(
        _proj_kernel,
        grid=(n // PROJ_BM, n_in // PROJ_BN),
        in_specs=[pl.BlockSpec((PROJ_BM, d), lambda i, j: (i, 0)),
                  pl.BlockSpec((d, PROJ_BN), lambda i, j: (0, j))],
        out_specs=pl.BlockSpec((PROJ_BM, PROJ_BN), lambda i, j: (i, j)),
        out_shape=jax.ShapeDtypeStruct((n, n_in), BF16),
        compiler_params=pltpu.CompilerParams(
            dimension_semantics=("arbitrary", "arbitrary"),
            vmem_limit_bytes=VMEM_LIMIT_BYTES),
        name="proj",
    )(xb, wb)


def _mixer_kernel(proj_ref, halo_ref, x_ref, band_ref, wpool_ref, pscale_ref, convw_ref,
                  wbra_ref, wbrb_ref, wo_ref, g1_ref, b1_ref, wrt_ref, br_ref,
                  x1_ref, x1p_ref, tope_ref, gate_ref, rank_ref, cnt_ref,
                  ext_ref, cext_ref, carry_ref, *, seq, alpha, pw, cw, d):
    tm = x_ref.shape[0]
    i = pl.program_id(0)
    blocks_per_seq = seq // tm
    j = i % blocks_per_seq
    keep_halo = (j > 0).astype(F32)
    row = lax.broadcasted_iota(jnp.int32, (tm, 1), 0)
    pos1 = (j * tm + row + 1).astype(F32)

    @pl.when(i == 0)
    def _():
        carry_ref[...] = jnp.zeros_like(carry_ref)
        ext_ref[0:POOL_PAD - HALO, :] = jnp.zeros((POOL_PAD - HALO, pw), BF16)

    o1, o2, o3 = pw + cw, pw + 2 * cw, pw + 3 * cw
    ext_ref[POOL_PAD - HALO:POOL_PAD, :] = halo_ref[:, 0:pw] * keep_halo.astype(BF16)
    ext_ref[POOL_PAD:POOL_PAD + tm, :] = proj_ref[:, 0:pw]
    cext_ref[0:HALO, :] = (halo_ref[:, o1:o2].astype(F32) * halo_ref[:, o2:o3].astype(F32)) * keep_halo
    cext_ref[HALO:HALO + tm, :] = proj_ref[:, o1:o2].astype(F32) * proj_ref[:, o2:o3].astype(F32)

    gw = pw // N_POOL_GROUPS
    hr = tm // MIX_SPLIT
    logit_parts = []
    for part in range(MIX_SPLIT):
        r0 = part * hr
        rs = slice(r0, r0 + hr)

        a_parts = []
        for g, w in enumerate(POOL_WINDOWS):
            cols = slice(g * gw, (g + 1) * gw)
            s = _dot(band_ref[g, rs, :], ext_ref[:, cols])
            cnt = jnp.minimum(pos1[rs, :], float(w))
            pooled = s * (1.0 / cnt) - proj_ref[rs, cols].astype(F32)
            a_g = _dot(pooled.astype(BF16), wpool_ref[g]) * pscale_ref[:, cols]
            a_parts.append(a_g.astype(BF16))
        br_a = _dot(jnp.concatenate(a_parts, axis=1), wbra_ref[...])

        conv = cext_ref[HALO + r0:HALO + r0 + hr, :] * convw_ref[CONV_K - 1:CONV_K, :]
        for k in range(CONV_K - 1):
            sh = CONV_K - 1 - k
            conv = conv + cext_ref[HALO + r0 - sh:HALO + r0 - sh + hr, :] * convw_ref[k:k + 1, :]
        b = proj_ref[rs, pw:o1].astype(F32) * conv
        br_b = _dot(b.astype(BF16), wbrb_ref[...])

        g_a = proj_ref[rs, o3:o3 + d].astype(F32)
        g_b = proj_ref[rs, o3 + d:o3 + 2 * d].astype(F32)
        m = _sigmoid(g_a) * br_a + _sigmoid(g_b) * br_b
        mix = _dot(m.astype(BF16), wo_ref[...])
        x1 = _layer_norm(alpha * x_ref[rs, :] + mix, g1_ref[...], b1_ref[...])
        x1_ref[rs, :] = x1
        _store_slabs(x1p_ref, r0, _pack_pairs(x1))

        logit_parts.append(lax.dot_general(wrt_ref[...], x1.astype(BF16), (((1,), (1,)), ((), ())),
                                           preferred_element_type=F32))
    logits = jnp.concatenate(logit_parts, axis=1) + br_ref[...]
    e_iota = lax.broadcasted_iota(jnp.int32, logits.shape, 0)
    vals, idxs = [], []
    l = logits
    for _ in range(TOP_K):
        mx = jnp.max(l, axis=0, keepdims=True)
        ix = jnp.min(jnp.where(l == mx, e_iota, N_EXPERTS), axis=0, keepdims=True)
        vals.append(mx)
        idxs.append(ix)
        l = jnp.where(e_iota == ix, -jnp.inf, l)
    exps = [jnp.exp(v - vals[0]) for v in vals]
    denom = exps[0] + exps[1] + exps[2] + exps[3]
    onehot = jnp.zeros(logits.shape, F32)
    for k in range(TOP_K):
        tope_ref[k:k + 1, :] = idxs[k]
        gate_ref[k:k + 1, :] = exps[k] / denom
        onehot = onehot + (e_iota == idxs[k]).astype(F32)

    r_i = lax.broadcasted_iota(jnp.int32, (tm, tm), 0)
    c_i = lax.broadcasted_iota(jnp.int32, (tm, tm), 1)
    before = (r_i < c_i).astype(BF16)
    seen = _dot(onehot.astype(BF16), before) + carry_ref[:, 0:1]
    for k in range(TOP_K):
        rk = jnp.sum(jnp.where(e_iota == idxs[k], seen, 0.0), axis=0, keepdims=True)
        rank_ref[k:k + 1, :] = rk.astype(jnp.int32)
    carry_ref[...] = carry_ref[...] + jnp.sum(onehot, axis=1, keepdims=True)
    cnt_ref[...] = carry_ref[...].astype(jnp.int32)


def _mixer(proj, x2, wpool, pscale, convw, wbra, wbrb, wo, g1, b1, wrt, br, *, seq, alpha):
    n, d = x2.shape
    n_in = proj.shape[1]
    pw = wbra.shape[0]
    cw = wbrb.shape[0]
    tm = MIX_TM
    slab = d // 2 // LANES
    hb = tm // HALO
    const = lambda i: (0, 0)
    kern = functools.partial(_mixer_kernel, seq=seq, alpha=alpha, pw=pw, cw=cw, d=d)
    t_idx = np.arange(tm)[:, None] + POOL_PAD
    s_idx = np.arange(POOL_PAD + tm)[None, :]
    band = jnp.asarray(np.stack([(s_idx > t_idx - w) & (s_idx <= t_idx) for w in POOL_WINDOWS]), BF16)
    return pl.pallas_call(
        kern,
        grid=(n // tm,),
        in_specs=[
            pl.BlockSpec((tm, n_in), lambda i: (i, 0)),
            pl.BlockSpec((HALO, n_in // 2), lambda i: (jnp.maximum(i * hb - 1, 0), 0)),
            pl.BlockSpec((tm, d), lambda i: (i, 0)),
            pl.BlockSpec(band.shape, lambda i: (0, 0, 0)),
            pl.BlockSpec(wpool.shape, lambda i: (0, 0, 0)),
            pl.BlockSpec(pscale.shape, const),
            pl.BlockSpec(convw.shape, const),
            pl.BlockSpec(wbra.shape, const),
            pl.BlockSpec(wbrb.shape, const),
            pl.BlockSpec(wo.shape, const),
            pl.BlockSpec(g1.shape, const),
            pl.BlockSpec(b1.shape, const),
            pl.BlockSpec(wrt.shape, const),
            pl.BlockSpec(br.shape, const),
        ],
        out_specs=[
            pl.BlockSpec((tm, d), lambda i: (i, 0)),
            pl.BlockSpec((tm * slab, LANES), lambda i: (i, 0)),
            pl.BlockSpec((TOP_K, tm), lambda i: (0, i)),
            pl.BlockSpec((TOP_K, tm), lambda i: (0, i)),
            pl.BlockSpec((TOP_K, tm), lambda i: (0, i)),
            pl.BlockSpec((N_EXPERTS, LANES), const),
        ],
        out_shape=[
            jax.ShapeDtypeStruct((n, d), F32),
            jax.ShapeDtypeStruct((n * slab, LANES), U32),
            jax.ShapeDtypeStruct((TOP_K, n), jnp.int32),
            jax.ShapeDtypeStruct((TOP_K, n), F32),
            jax.ShapeDtypeStruct((TOP_K, n), jnp.int32),
            jax.ShapeDtypeStruct((N_EXPERTS, LANES), jnp.int32),
        ],
        scratch_shapes=[pltpu.VMEM((POOL_PAD + tm, pw), BF16),
                        pltpu.VMEM((HALO + tm, cw), F32),
                        pltpu.VMEM((N_EXPERTS, LANES), F32)],
        compiler_params=pltpu.CompilerParams(
            dimension_semantics=("arbitrary",),
            vmem_limit_bytes=VMEM_LIMIT_BYTES),
        name="mixer",
    )(proj, proj, x2, band, wpool, pscale, convw, wbra, wbrb, wo, g1, b1, wrt, br)


def _dest_kernel(pst_ref, tope_ref, rank_ref, dest_ref):
    tope = tope_ref[...]
    dest = rank_ref[...]
    for e in range(N_EXPERTS):
        dest = dest + jnp.where(tope == e, pst_ref[e], 0)
    dest_ref[...] = dest


def _dest(pad_starts, tope, rank):
    whole = pl.BlockSpec(tope.shape, lambda i, *_: (0, 0))
    return pl.pallas_call(
        _dest_kernel,
        grid_spec=pltpu.PrefetchScalarGridSpec(
            num_scalar_prefetch=1, grid=(1,), in_specs=[whole, whole], out_specs=whole),
        out_shape=jax.ShapeDtypeStruct(tope.shape, jnp.int32),
        name="dest",
    )(pad_starts, tope, rank)


def _dispatch_kernel(pst_ref, pend_ref, dest_ref, x1p_hbm, xs_hbm, xbuf, zero_ref, ld_sem, sem, zsem,
                     *, slab, n_steps):
    tb = dest_ref.shape[1]
    i = pl.program_id(0)
    blk = ROW_BLK * slab
    chunk = tb * slab

    def load(c):
        row0 = c * chunk if isinstance(c, int) else pl.multiple_of(c * chunk, chunk)
        rows = pl.ds(row0, chunk)
        return pltpu.make_async_copy(x1p_hbm.at[rows], xbuf.at[rows], ld_sem.at[c])

    def zero_copy(b):
        return pltpu.make_async_copy(
            zero_ref, xs_hbm.at[pl.ds(pl.multiple_of(b * blk, blk), blk)], zsem)

    @pl.when(i == 0)
    def _():
        zero_ref[...] = jnp.zeros_like(zero_ref)

        def start(e, c):
            @pl.when(pend_ref[e] > pst_ref[e])
            def _():
                zero_copy(pend_ref[e] // ROW_BLK - 1).start()
            return c

        def wait(e, c):
            @pl.when(pend_ref[e] > pst_ref[e])
            def _():
                zero_copy(pend_ref[e] // ROW_BLK - 1).wait()
            return c

        def start_tail(b, c):
            zero_copy(b).start()
            return c

        def wait_tail(b, c):
            zero_copy(b).wait()
            return c

        used = pend_ref[N_EXPERTS - 1] // ROW_BLK
        total = xs_hbm.shape[0] // blk
        lax.fori_loop(0, N_EXPERTS, start, 0)
        lax.fori_loop(used, total, start_tail, 0)
        for c in range(n_steps):
            load(c).start()
        lax.fori_loop(0, N_EXPERTS, wait, 0)
        lax.fori_loop(used, total, wait_tail, 0)

    load(i).wait()

    def start(t, c):
        src = xbuf.at[pl.ds(pl.multiple_of((i * tb + t) * slab, slab), slab)]
        for k in range(TOP_K):
            dst = dest_ref[k, t]
            pltpu.make_async_copy(
                src, xs_hbm.at[pl.ds(pl.multiple_of(dst * slab, slab), slab)], sem,
            ).start(priority=k % 2)
        return c

    lax.fori_loop(0, tb, start, 0, unroll=4)

    def wait_one_step():
        for k in range(TOP_K):
            pltpu.make_async_copy(
                xbuf.at[pl.ds(0, chunk)], xs_hbm.at[pl.ds(0, chunk)], sem).wait()

    @pl.when(i > 0)
    def _():
        wait_one_step()

    @pl.when(i == n_steps - 1)
    def _():
        wait_one_step()


def _dispatch(pad_starts, pad_ends, dest, x1p, n, cap):
    slab = x1p.shape[0] // n
    tb = DISP_TB
    n_steps = n // tb
    smem_blk = pl.BlockSpec((TOP_K, tb), lambda i, *_: (0, i), memory_space=pltpu.SMEM)
    return pl.pallas_call(
        functools.partial(_dispatch_kernel, slab=slab, n_steps=n_steps),
        grid_spec=pltpu.PrefetchScalarGridSpec(
            num_scalar_prefetch=2,
            grid=(n_steps,),
            in_specs=[smem_blk, pl.BlockSpec(memory_space=pl.ANY)],
            out_specs=pl.BlockSpec(memory_space=pl.ANY),
            scratch_shapes=[pltpu.VMEM(x1p.shape, U32),
                            pltpu.VMEM((ROW_BLK * slab, LANES), U32),
                            pltpu.SemaphoreType.DMA((n_steps,)),
                            pltpu.SemaphoreType.DMA(()),
                            pltpu.SemaphoreType.DMA(())],
        ),
        out_shape=jax.ShapeDtypeStruct((cap * slab, LANES), U32),
        compiler_params=pltpu.CompilerParams(
            dimension_semantics=("arbitrary",),
            vmem_limit_bytes=VMEM_LIMIT_BYTES,
            has_side_effects=True),
        name="dispatch",
    )(pad_starts, pad_ends, dest, x1p)


def _ffn_kernel(ie_ref, ist_ref, inb_ref, used_ref,
                xs_hbm, wgu_hbm, wd_hbm, bgu_ref, bd_ref,
                y_hbm,
                slabs, xbuf, yacc, wg_buf, wu_buf, wd_buf, sem_in, sem_out, sem_w, *, slab, nf):
    w = pl.program_id(0)
    n_items = pl.num_programs(0)
    nb = inb_ref[w]
    blk = ROW_BLK * slab
    half = slab * LANES
    slot_rows = slabs.shape[0] // 2
    tf = wg_buf.shape[2]
    dff = nf * tf
    prev_item = jnp.maximum(w - 1, 0)
    next_item = jnp.minimum(w + 1, n_items - 1)
    n_quads = lax.shift_right_logical(nb, 2)
    rem0 = n_quads * 4

    def rows(r):
        return pl.ds(pl.multiple_of(r * ROW_BLK, ROW_BLK), ROW_BLK)

    def slot_row0(item):
        return (item % 2) * (slot_rows // slab)

    def hbm_blk(ref, item, r):
        return ref.at[pl.ds(pl.multiple_of((ist_ref[item] + r) * blk, blk), blk)]

    def vmem_blk(item, r):
        return slabs.at[pl.ds(pl.multiple_of((item % 2) * slot_rows + r * blk, blk), blk)]

    def for_blocks(item, fn):
        def body(r, c):
            fn(r)
            return c
        lax.fori_loop(0, inb_ref[item], body, 0)

    def load(item):
        return lambda r: pltpu.make_async_copy(hbm_blk(xs_hbm, item, r), vmem_blk(item, r), sem_in)

    def store(item):
        return lambda r: pltpu.make_async_copy(
            vmem_blk(item, r), hbm_blk(y_hbm, item, r), sem_out.at[item % 2])

    def chunk_start(c):
        return c * tf if isinstance(c, int) else pl.multiple_of(c * tf, tf)

    def weight_copies(item, c, wslot):
        e = ie_ref[item]
        c0 = chunk_start(c)
        return (
            pltpu.make_async_copy(wgu_hbm.at[e, :, pl.ds(c0, tf)], wg_buf.at[wslot], sem_w.at[wslot, 0]),
            pltpu.make_async_copy(wgu_hbm.at[e, :, pl.ds(dff + c0, tf)], wu_buf.at[wslot], sem_w.at[wslot, 1]),
            pltpu.make_async_copy(wd_hbm.at[e, pl.ds(c0, tf), :], wd_buf.at[wslot], sem_w.at[wslot, 2]),
        )

    def unpack(row0, n_rows):
        rs = pl.ds(pl.multiple_of(row0, ROW_BLK), n_rows)
        for col in range(slab):
            hi, lo = _load_slab_column(slabs, slot_row0(w) + row0, n_rows, slab, col)
            xbuf[rs, col * LANES:(col + 1) * LANES] = hi.astype(BF16)
            xbuf[rs, half + col * LANES:half + (col + 1) * LANES] = lo.astype(BF16)

    def pack_and_store(blk0, n_blocks, acc):
        _store_slabs(slabs, slot_row0(w) + blk0 * ROW_BLK, _pack_pairs(acc))
        for r in range(n_blocks):
            store(w)(blk0 + r).start()

    def chunk(c, wslot, mode):
        expert = ie_ref[w]
        c0 = chunk_start(c)
        bias_g = bgu_ref[expert, :, pl.ds(c0, tf)]
        bias_u = bgu_ref[expert, :, pl.ds(dff + c0, tf)]
        bias_d = bd_ref[expert]

        def sub(blk0, n_blocks, sub_mode):
            row0 = blk0 * ROW_BLK
            n_rows = n_blocks * ROW_BLK
            rs = pl.ds(pl.multiple_of(row0, ROW_BLK), n_rows)
            if sub_mode == "first":
                unpack(row0, n_rows)
            xt = xbuf[rs, :]
            gate = jnp.minimum(_dot(xt, wg_buf[wslot]) + bias_g, SWIGLU_LIMIT)
            up = jnp.clip(_dot(xt, wu_buf[wslot]) + bias_u, -SWIGLU_LIMIT, SWIGLU_LIMIT)
            act = (up + 1.0) * (gate * _sigmoid(SWIGLU_ALPHA * gate))
            contrib = _dot(act.astype(BF16), wd_buf[wslot])
            if sub_mode == "first":
                yacc[rs, :] = contrib + bias_d
            elif sub_mode == "last":
                pack_and_store(blk0, n_blocks, yacc[rs, :] + contrib)
            else:
                yacc[rs, :] += contrib

        if mode == "first":
            def prepare(r, carry):
                unpack(r * ROW_BLK, ROW_BLK)
                yacc[rows(r), :] = jnp.broadcast_to(bias_d, (ROW_BLK, yacc.shape[1]))
                return carry
            lax.fori_loop(rem0, nb, prepare, 0)

        def quad(q, carry):
            sub(q * 4, 4, mode)
            return carry
        lax.fori_loop(0, n_quads, quad, 0)

        @pl.when((nb & 2) != 0)
        def _():
            sub(rem0, 2, "mid")

        @pl.when((nb & 1) != 0)
        def _():
            sub(nb - 1, 1, "mid")

        if mode == "last":
            def finish(r, carry):
                pack_and_store(r, 1, yacc[rows(r), :])
                return carry
            lax.fori_loop(rem0, nb, finish, 0)

    @pl.when(w == 0)
    def _():
        slabs[0:blk, :] = jnp.zeros((blk, LANES), U32)

        def tail_copy(b):
            return pltpu.make_async_copy(
                slabs.at[pl.ds(0, blk)], y_hbm.at[pl.ds(pl.multiple_of(b * blk, blk), blk)],
                sem_out.at[0])

        def start(b, c):
            tail_copy(b).start()
            return c

        def wait(b, c):
            tail_copy(b).wait()
            return c

        total = y_hbm.shape[0] // blk
        lax.fori_loop(used_ref[0], total, start, 0)
        lax.fori_loop(used_ref[0], total, wait, 0)
        for_blocks(w, lambda r: load(w)(r).start())
        for cp in weight_copies(w, 0, 0):
            cp.start()

    for cp in weight_copies(w, 0, 0):
        cp.wait()
    for cp in weight_copies(w, 1, 1):
        cp.start()
    for_blocks(w, lambda r: load(w)(r).wait())
    chunk(0, 0, "first")

    def middle(c, carry):
        wslot = c % 2
        for cp in weight_copies(w, c, wslot):
            cp.wait()
        for cp in weight_copies(w, c + 1, 1 - wslot):
            cp.start()
        chunk(c, wslot, "mid")
        return carry
    lax.fori_loop(1, nf - 1, middle, 0)

    last_slot = (nf - 1) % 2
    for cp in weight_copies(w, nf - 1, last_slot):
        cp.wait()

    @pl.when(w >= 1)
    def _():
        for_blocks(prev_item, lambda r: store(prev_item)(r).wait())

    @pl.when(w + 1 < n_items)
    def _():
        for_blocks(next_item, lambda r: load(next_item)(r).start())
        for cp in weight_copies(next_item, 0, 0):
            cp.start()

    chunk(nf - 1, last_slot, "last")

    @pl.when(w == n_items - 1)
    def _():
        for_blocks(w, lambda r: store(w)(r).wait())


def _ffn(item_e, item_st, item_nb, used_items, used_blocks, xs, w_gu, b_gu, w_down, b_down, cap):
    slab = xs.shape[0] // cap
    n_e, d, f2 = w_gu.shape
    dff = f2 // 2
    tf = FFN_TF
    nf = dff // tf
    assert nf >= 2 and nf % 2 == 0, "weight slots alternate per d_ff chunk and restart at 0 for every item"
    ts = ITEM_BLKS * ROW_BLK
    hbm = pl.BlockSpec(memory_space=pl.ANY)
    resident = pl.BlockSpec(memory_space=pltpu.VMEM)
    return pl.pallas_call(
        functools.partial(_ffn_kernel, slab=slab, nf=nf),
        grid_spec=pltpu.PrefetchScalarGridSpec(
            num_scalar_prefetch=4,
            grid=(used_items[0],),
            in_specs=[hbm, hbm, hbm, resident, resident],
            out_specs=hbm,
            scratch_shapes=[pltpu.VMEM((2 * ts * slab, LANES), U32),
                            pltpu.VMEM((ts, d), BF16),
                            pltpu.VMEM((ts, d), F32),
                            pltpu.VMEM((2, d, tf), F32),
                            pltpu.VMEM((2, d, tf), F32),
                            pltpu.VMEM((2, tf, d), F32),
                            pltpu.SemaphoreType.DMA(()),
                            pltpu.SemaphoreType.DMA((2,)),
                            pltpu.SemaphoreType.DMA((2, 3))],
        ),
        out_shape=jax.ShapeDtypeStruct((cap * slab, LANES), U32),
        compiler_params=pltpu.CompilerParams(
            dimension_semantics=("arbitrary",),
            vmem_limit_bytes=VMEM_LIMIT_BYTES,
            has_side_effects=True),
        name="ffn",
    )(item_e, item_st, item_nb, used_blocks, xs, w_gu, w_down, b_gu.reshape(n_e, 1, f2),
      b_down.reshape(n_e, 1, d))


def _final_kernel(dcur_ref, dnxt_ref, gate_ref, x1_ref, p_ref, y_hbm,
                  g2_ref, b2_ref, wpg_ref, bpg_ref, wple_ref, g3_ref, b3_ref,
                  o_ref, ybuf, ffn_ref, gbuf, sem, *, alpha, slab):
    tm = x1_ref.shape[0]
    half = slab * LANES
    i = pl.program_id(0)
    n_steps = pl.num_programs(0)
    slot = i % 2
    slot_tokens = TOP_K * tm
    slot_rows = slot_tokens * slab

    def gather(d_ref, into):
        def start(t, c):
            for k in range(TOP_K):
                src = d_ref[k, t]
                dst = (into * slot_tokens + k * tm + t) * slab
                pltpu.make_async_copy(
                    y_hbm.at[pl.ds(pl.multiple_of(src * slab, slab), slab)],
                    ybuf.at[pl.ds(pl.multiple_of(dst, slab), slab)], sem.at[into],
                ).start(priority=k % 2)
            return c
        lax.fori_loop(0, tm, start, 0, unroll=4)

    @pl.when(i == 0)
    def _():
        gather(dcur_ref, slot)

    @pl.when(i + 1 < n_steps)
    def _():
        gather(dnxt_ref, 1 - slot)

    pltpu.make_async_copy(
        y_hbm.at[pl.ds(0, slot_rows)],
        ybuf.at[pl.ds(pl.multiple_of(slot * slot_rows, slot_rows), slot_rows)], sem.at[slot]).wait()

    for k in range(TOP_K):
        gbuf[k] = jnp.broadcast_to(gate_ref[:, k:k + 1], (tm, LANES))

    hr = tm // FINAL_SPLIT
    for part in range(FINAL_SPLIT):
        rs = slice(part * hr, (part + 1) * hr)
        for col in range(slab):
            acc_hi = acc_lo = None
            for k in range(TOP_K):
                hi, lo = _load_slab_column(
                    ybuf, slot * slot_tokens + k * tm + part * hr, hr, slab, col)
                g = gbuf[k, rs, :]
                acc_hi = hi * g if acc_hi is None else acc_hi + hi * g
                acc_lo = lo * g if acc_lo is None else acc_lo + lo * g
            ffn_ref[rs, col * LANES:(col + 1) * LANES] = acc_hi
            ffn_ref[rs, half + col * LANES:half + (col + 1) * LANES] = acc_lo
        h2 = _layer_norm(alpha * x1_ref[rs, :] + ffn_ref[rs, :], g2_ref[...], b2_ref[...])
        pg = _sigmoid(_dot(h2.astype(BF16), wpg_ref[...]) + bpg_ref[...])
        pe = _dot(p_ref[rs, :].astype(BF16), wple_ref[...])
        o_ref[rs, :] = _layer_norm(alpha * h2 + pg * pe, g3_ref[...], b3_ref[...])


def _final(dest, gates_t, x1, p2, y, g2, b2, wpg, bpg, wple, g3, b3, cap, *, alpha):
    n, d = x1.shape
    slab = y.shape[0] // cap
    tm = MIX_TM
    n_steps = n // tm
    dest_cur = pl.BlockSpec((TOP_K, tm), lambda i: (0, i), memory_space=pltpu.SMEM)
    dest_next = pl.BlockSpec((TOP_K, tm), lambda i: (0, jnp.minimum(i + 1, n_steps - 1)),
                             memory_space=pltpu.SMEM)
    const = lambda i: (0, 0)
    row_blk = lambda width: pl.BlockSpec((tm, width), lambda i: (i, 0))
    return pl.pallas_call(
        functools.partial(_final_kernel, alpha=alpha, slab=slab),
        grid_spec=pltpu.PrefetchScalarGridSpec(
            num_scalar_prefetch=0,
            grid=(n_steps,),
            in_specs=[dest_cur, dest_next, row_blk(TOP_K), row_blk(d), row_blk(p2.shape[1]),
                      pl.BlockSpec(memory_space=pl.ANY),
                      pl.BlockSpec(g2.shape, const), pl.BlockSpec(b2.shape, const),
                      pl.BlockSpec(wpg.shape, const), pl.BlockSpec(bpg.shape, const),
                      pl.BlockSpec(wple.shape, const),
                      pl.BlockSpec(g3.shape, const), pl.BlockSpec(b3.shape, const)],
            out_specs=row_blk(d),
            scratch_shapes=[pltpu.VMEM((2 * TOP_K * tm * slab, LANES), U32),
                            pltpu.VMEM((tm, d), F32),
                            pltpu.VMEM((TOP_K, tm, LANES), F32),
                            pltpu.SemaphoreType.DMA((2,))],
        ),
        out_shape=jax.ShapeDtypeStruct((n, d), F32),
        compiler_params=pltpu.CompilerParams(
            dimension_semantics=("arbitrary",),
            vmem_limit_bytes=VMEM_LIMIT_BYTES),
        name="final",
    )(dest, dest, gates_t, x1, p2, y, g2, b2, wpg, bpg, wple, g3, b3)


def _plan(counts, n_items):
    nblk = (counts + ROW_BLK - 1) // ROW_BLK
    blk_end = jnp.cumsum(nblk)
    blk_start = blk_end - nblk
    items_per = (nblk + ITEM_BLKS - 1) // ITEM_BLKS
    item_end = jnp.cumsum(items_per)
    item_start = item_end - items_per
    total = item_end[-1]
    w = jnp.arange(n_items, dtype=jnp.int32)
    w_eff = jnp.minimum(w, total - 1)
    e = jnp.sum((item_end[None, :] <= w_eff[:, None]).astype(jnp.int32), axis=1)
    e = jnp.minimum(e, N_EXPERTS - 1)
    j = w_eff - item_start[e]
    st = blk_start[e] + j * ITEM_BLKS
    nb = jnp.where(w < total, jnp.clip(nblk[e] - j * ITEM_BLKS, 0, ITEM_BLKS), 0)
    return ((blk_start * ROW_BLK).astype(jnp.int32), (blk_end * ROW_BLK).astype(jnp.int32),
            e, st.astype(jnp.int32), nb.astype(jnp.int32), total.reshape(1).astype(jnp.int32))


def kernel(x, p, w_in, w_pool, pool_scale, conv_w, w_br_a, w_br_b, w_o, ln1_g, ln1_b,
           w_router, b_router, w_gu, b_gu, w_down, b_down, ln2_g, ln2_b,
           w_pg, b_pg, w_ple, ln3_g, ln3_b):
    depth = w_in.shape[0]
    bsz, seq, d = x.shape
    n = bsz * seq
    alpha = (2.0 * depth) ** 0.25
    nk = n * TOP_K
    n_row_blocks = (nk + N_EXPERTS * (ROW_BLK - 1) + ROW_BLK - 1) // ROW_BLK
    cap = n_row_blocks * ROW_BLK
    n_items = N_EXPERTS + n_row_blocks // ITEM_BLKS
    row = lambda v: v.reshape(1, -1)

    h = x.reshape(n, d)
    for i in range(depth):
        proj = _proj(h, w_in[i])
        x1, x1p, tope, gates, rank, cnt = _mixer(
            proj, h, w_pool[i].astype(BF16), row(pool_scale[i]), conv_w[i],
            w_br_a[i].astype(BF16), w_br_b[i].astype(BF16), w_o[i].astype(BF16),
            row(ln1_g[i]), row(ln1_b[i]), w_router[i].T.astype(BF16),
            b_router[i].reshape(N_EXPERTS, 1), seq=seq, alpha=alpha)
        pad_starts, pad_ends, item_e, item_st, item_nb, used_items = _plan(cnt[:, 0], n_items)
        dest = _dest(pad_starts, tope, rank)
        xs = _dispatch(pad_starts, pad_ends, dest, x1p, n, cap)
        used_blocks = pad_ends[N_EXPERTS - 1:] // ROW_BLK
        y = _ffn(item_e, item_st, item_nb, used_items, used_blocks, xs,
                 w_gu[i], b_gu[i], w_down[i], b_down[i], cap)
        h = _final(dest, gates.T, x1, p[i].reshape(n, -1), y,
                   row(ln2_g[i]), row(ln2_b[i]), w_pg[i].astype(BF16), row(b_pg[i]),
                   w_ple[i].astype(BF16), row(ln3_g[i]), row(ln3_b[i]), cap, alpha=alpha)
    return h.reshape(bsz, seq, d)
```

```python
import functools

import jax
import jax.numpy as jnp
import numpy as np
from jax import lax
from jax.experimental import pallas as pl
from jax.experimental.pallas import tpu as pltpu

F32 = jnp.float32
BF16 = jnp.bfloat16
U32 = jnp.uint32

POOL_WINDOWS = (2, 4, 8, 16)
N_POOL_GROUPS = 4
CONV_K = 3
N_EXPERTS = 32
TOP_K = 4
SWIGLU_LIMIT = 7.0
SWIGLU_ALPHA = 1.702
LN_EPS = 1e-5

LANES = 128
VMEM_LIMIT_BYTES = 56 * 1024 * 1024

HALO = 16
POOL_PAD = LANES
PROJ_BM = 1024
PROJ_BN = 1024
PROJ_SLOTS = 3
MIX_TM = 256
MIX_SPLIT = 1
FINAL_SPLIT = 2
DISP_TB = 512
ROW_BLK = 128
ITEM_BLKS = 10
FFN_TF = 512


def _layer_norm(v, g, b):
    mu = jnp.mean(v, axis=-1, keepdims=True)
    c = v - mu
    var = jnp.mean(c * c, axis=-1, keepdims=True)
    return c * lax.rsqrt(var + LN_EPS) * g + b


def _dot(a, b):
    return jnp.dot(a, b, preferred_element_type=F32)


def _sigmoid(v):
    return 0.5 * jnp.tanh(0.5 * v) + 0.5


def _pack_pairs(v):
    half = v.shape[1] // 2
    bits = pltpu.bitcast(v.astype(BF16).astype(F32), U32)
    return (bits[:, 0:half] & U32(0xFFFF0000)) | (bits[:, half:] >> 16)


def _store_slabs(ref, row0, packed):
    rows, width = packed.shape
    s = width // LANES
    for c in range(s):
        ref[pl.ds(row0 * s + c, rows, stride=s), :] = packed[:, c * LANES:(c + 1) * LANES]


def _load_slab_column(ref, row0, rows, s, c):
    p = ref[pl.ds(row0 * s + c, rows, stride=s), :]
    return pltpu.bitcast(p & U32(0xFFFF0000), F32), pltpu.bitcast(p << 16, F32)


def _proj_kernel(x_ref, w_hbm, o_ref, wbuf, sem, *, n_col_blocks, n_steps):
    s = pl.program_id(0) * n_col_blocks + pl.program_id(1)
    bn = wbuf.shape[2]

    def fetch(step):
        col0 = pl.multiple_of((step % n_col_blocks) * bn, bn)
        slot = step % PROJ_SLOTS
        return pltpu.make_async_copy(w_hbm.at[:, pl.ds(col0, bn)], wbuf.at[slot], sem.at[slot])

    @pl.when(s == 0)
    def _():
        for step in range(PROJ_SLOTS - 1):
            fetch(step).start()

    @pl.when(s + PROJ_SLOTS - 1 < n_steps)
    def _():
        fetch(s + PROJ_SLOTS - 1).start()

    fetch(s).wait()
    w = wbuf[s % PROJ_SLOTS]
    o_ref[...] = _dot(x_ref[...].astype(BF16), w.astype(BF16)).astype(o_ref.dtype)


def _proj(xb, wb):
    n, d = xb.shape
    n_in = wb.shape[1]
    n_row_blocks, n_col_blocks = n // PROJ_BM, n_in // PROJ_BN
    n_steps = n_row_blocks * n_col_blocks
    assert n_steps >= PROJ_SLOTS
    return pl.pallas_call(
        functools.partial(_proj_kernel, n_col_blocks=n_col_blocks, n_steps=n_steps),
        grid=(n_row_blocks, n_col_blocks),
        in_specs=[pl.BlockSpec((PROJ_BM, d), lambda i, j: (i, 0)),
                  pl.BlockSpec(memory_space=pl.ANY)],
        out_specs=pl.BlockSpec((PROJ_BM, PROJ_BN), lambda i, j: (i, j)),
        out_shape=jax.ShapeDtypeStruct((n, n_in), BF16),
        scratch_shapes=[pltpu.VMEM((PROJ_SLOTS, d, PROJ_BN), F32),
                        pltpu.SemaphoreType.DMA((PROJ_SLOTS,))],
        compiler_params=pltpu.CompilerParams(
            dimension_semantics=("arbitrary", "arbitrary"),
            vmem_limit_bytes=VMEM_LIMIT_BYTES),
        name="proj",
    )(xb, wb)


def _mixer_kernel(proj_ref, halo_ref, x_ref, band_ref, wpool_ref, pscale_ref, convw_ref,
                  wbra_ref, wbrb_ref, wo_ref, g1_ref, b1_ref, wrt_ref, br_ref,
                  x1_ref, x1p_ref, tope_ref, gate_ref, rank_ref, cnt_ref,
                  ext_ref, cext_ref, carry_ref, *, seq, alpha, pw, cw, d):
    tm = x_ref.shape[0]
    i = pl.program_id(0)
    blocks_per_seq = seq // tm
    j = i % blocks_per_seq
    keep_halo = (j > 0).astype(F32)
    row = lax.broadcasted_iota(jnp.int32, (tm, 1), 0)
    pos1 = (j * tm + row + 1).astype(F32)

    @pl.when(i == 0)
    def _():
        carry_ref[...] = jnp.zeros_like(carry_ref)
        ext_ref[0:POOL_PAD - HALO, :] = jnp.zeros((POOL_PAD - HALO, pw), BF16)

    o1, o2, o3 = pw + cw, pw + 2 * cw, pw + 3 * cw
    ext_ref[POOL_PAD - HALO:POOL_PAD, :] = halo_ref[:, 0:pw] * keep_halo.astype(BF16)
    ext_ref[POOL_PAD:POOL_PAD + tm, :] = proj_ref[:, 0:pw]
    cext_ref[0:HALO, :] = (halo_ref[:, o1:o2].astype(F32) * halo_ref[:, o2:o3].astype(F32)) * keep_halo
    cext_ref[HALO:HALO + tm, :] = proj_ref[:, o1:o2].astype(F32) * proj_ref[:, o2:o3].astype(F32)

    gw = pw // N_POOL_GROUPS
    hr = tm // MIX_SPLIT
    logit_parts = []
    for part in range(MIX_SPLIT):
        r0 = part * hr
        rs = slice(r0, r0 + hr)

        a_parts = []
        for g, w in enumerate(POOL_WINDOWS):
            cols = slice(g * gw, (g + 1) * gw)
            s = _dot(band_ref[g, rs, :], ext_ref[:, cols])
            cnt = jnp.minimum(pos1[rs, :], float(w))
            pooled = s * (1.0 / cnt) - proj_ref[rs, cols].astype(F32)
            a_g = _dot(pooled.astype(BF16), wpool_ref[g]) * pscale_ref[:, cols]
            a_parts.append(a_g.astype(BF16))
        br_a = _dot(jnp.concatenate(a_parts, axis=1), wbra_ref[...])

        conv = cext_ref[HALO + r0:HALO + r0 + hr, :] * convw_ref[CONV_K - 1:CONV_K, :]
        for k in range(CONV_K - 1):
            sh = CONV_K - 1 - k
            conv = conv + cext_ref[HALO + r0 - sh:HALO + r0 - sh + hr, :] * convw_ref[k:k + 1, :]
        b = proj_ref[rs, pw:o1].astype(F32) * conv
        br_b = _dot(b.astype(BF16), wbrb_ref[...])

        g_a = proj_ref[rs, o3:o3 + d].astype(F32)
        g_b = proj_ref[rs, o3 + d:o3 + 2 * d].astype(F32)
        m = _sigmoid(g_a) * br_a + _sigmoid(g_b) * br_b
        mix = _dot(m.astype(BF16), wo_ref[...])
        x1 = _layer_norm(alpha * x_ref[rs, :] + mix, g1_ref[...], b1_ref[...])
        x1_ref[rs, :] = x1
        _store_slabs(x1p_ref, r0, _pack_pairs(x1))

        logit_parts.append(lax.dot_general(wrt_ref[...], x1.astype(BF16), (((1,), (1,)), ((), ())),
                                           preferred_element_type=F32))
    logits = jnp.concatenate(logit_parts, axis=1) + br_ref[...]
    e_iota = lax.broadcasted_iota(jnp.int32, logits.shape, 0)
    vals, idxs = [], []
    l = logits
    for _ in range(TOP_K):
        mx = jnp.max(l, axis=0, keepdims=True)
        ix = jnp.min(jnp.where(l == mx, e_iota, N_EXPERTS), axis=0, keepdims=True)
        vals.append(mx)
        idxs.append(ix)
        l = jnp.where(e_iota == ix, -jnp.inf, l)
    exps = [jnp.exp(v - vals[0]) for v in vals]
    denom = exps[0] + exps[1] + exps[2] + exps[3]
    onehot = jnp.zeros(logits.shape, F32)
    for k in range(TOP_K):
        tope_ref[k:k + 1, :] = idxs[k]
        gate_ref[k:k + 1, :] = exps[k] / denom
        onehot = onehot + (e_iota == idxs[k]).astype(F32)

    r_i = lax.broadcasted_iota(jnp.int32, (tm, tm), 0)
    c_i = lax.broadcasted_iota(jnp.int32, (tm, tm), 1)
    before = (r_i < c_i).astype(BF16)
    seen = _dot(onehot.astype(BF16), before) + carry_ref[:, 0:1]
    for k in range(TOP_K):
        rk = jnp.sum(jnp.where(e_iota == idxs[k], seen, 0.0), axis=0, keepdims=True)
        rank_ref[k:k + 1, :] = rk.astype(jnp.int32)
    carry_ref[...] = carry_ref[...] + jnp.sum(onehot, axis=1, keepdims=True)
    cnt_ref[...] = carry_ref[...].astype(jnp.int32)


def _mixer(proj, x2, wpool, pscale, convw, wbra, wbrb, wo, g1, b1, wrt, br, *, seq, alpha):
    n, d = x2.shape
    n_in = proj.shape[1]
    pw = wbra.shape[0]
    cw = wbrb.shape[0]
    tm = MIX_TM
    slab = d // 2 // LANES
    hb = tm // HALO
    const = lambda i: (0, 0)
    kern = functools.partial(_mixer_kernel, seq=seq, alpha=alpha, pw=pw, cw=cw, d=d)
    t_idx = np.arange(tm)[:, None] + POOL_PAD
    s_idx = np.arange(POOL_PAD + tm)[None, :]
    band = jnp.asarray(np.stack([(s_idx > t_idx - w) & (s_idx <= t_idx) for w in POOL_WINDOWS]), BF16)
    return pl.pallas_call(
        kern,
        grid=(n // tm,),
        in_specs=[
            pl.BlockSpec((tm, n_in), lambda i: (i, 0)),
            pl.BlockSpec((HALO, n_in // 2), lambda i: (jnp.maximum(i * hb - 1, 0), 0)),
            pl.BlockSpec((tm, d), lambda i: (i, 0)),
            pl.BlockSpec(band.shape, lambda i: (0, 0, 0)),
            pl.BlockSpec(wpool.shape, lambda i: (0, 0, 0)),
            pl.BlockSpec(pscale.shape, const),
            pl.BlockSpec(convw.shape, const),
            pl.BlockSpec(wbra.shape, const),
            pl.BlockSpec(wbrb.shape, const),
            pl.BlockSpec(wo.shape, const),
            pl.BlockSpec(g1.shape, const),
            pl.BlockSpec(b1.shape, const),
            pl.BlockSpec(wrt.shape, const),
            pl.BlockSpec(br.shape, const),
        ],
        out_specs=[
            pl.BlockSpec((tm, d), lambda i: (i, 0)),
            pl.BlockSpec((tm * slab, LANES), lambda i: (i, 0)),
            pl.BlockSpec((TOP_K, tm), lambda i: (0, i)),
            pl.BlockSpec((TOP_K, tm), lambda i: (0, i)),
            pl.BlockSpec((TOP_K, tm), lambda i: (0, i)),
            pl.BlockSpec((N_EXPERTS, LANES), const),
        ],
        out_shape=[
            jax.ShapeDtypeStruct((n, d), F32),
            jax.ShapeDtypeStruct((n * slab, LANES), U32),
            jax.ShapeDtypeStruct((TOP_K, n), jnp.int32),
            jax.ShapeDtypeStruct((TOP_K, n), F32),
            jax.ShapeDtypeStruct((TOP_K, n), jnp.int32),
            jax.ShapeDtypeStruct((N_EXPERTS, LANES), jnp.int32),
        ],
        scratch_shapes=[pltpu.VMEM((POOL_PAD + tm, pw), BF16),
                        pltpu.VMEM((HALO + tm, cw), F32),
                        pltpu.VMEM((N_EXPERTS, LANES), F32)],
        compiler_params=pltpu.CompilerParams(
            dimension_semantics=("arbitrary",),
            vmem_limit_bytes=VMEM_LIMIT_BYTES),
        name="mixer",
    )(proj, proj, x2, band, wpool, pscale, convw, wbra, wbrb, wo, g1, b1, wrt, br)


def _dest_kernel(pst_ref, tope_ref, rank_ref, dest_ref):
    tope = tope_ref[...]
    dest = rank_ref[...]
    for e in range(N_EXPERTS):
        dest = dest + jnp.where(tope == e, pst_ref[e], 0)
    dest_ref[...] = dest


def _dest(pad_starts, tope, rank):
    whole = pl.BlockSpec(tope.shape, lambda i, *_: (0, 0))
    return pl.pallas_call(
        _dest_kernel,
        grid_spec=pltpu.PrefetchScalarGridSpec(
            num_scalar_prefetch=1, grid=(1,), in_specs=[whole, whole], out_specs=whole),
        out_shape=jax.ShapeDtypeStruct(tope.shape, jnp.int32),
        name="dest",
    )(pad_starts, tope, rank)


def _dispatch_kernel(pst_ref, pend_ref, dest_ref, x1p_hbm, xs_hbm, xbuf, zero_ref, ld_sem, sem, zsem,
                     *, slab, n_steps):
    tb = dest_ref.shape[1]
    i = pl.program_id(0)
    blk = ROW_BLK * slab
    chunk = tb * slab

    def load(c):
        row0 = c * chunk if isinstance(c, int) else pl.multiple_of(c * chunk, chunk)
        rows = pl.ds(row0, chunk)
        return pltpu.make_async_copy(x1p_hbm.at[rows], xbuf.at[rows], ld_sem.at[c])

    def zero_copy(b):
        return pltpu.make_async_copy(
            zero_ref, xs_hbm.at[pl.ds(pl.multiple_of(b * blk, blk), blk)], zsem)

    @pl.when(i == 0)
    def _():
        zero_ref[...] = jnp.zeros_like(zero_ref)

        def start(e, c):
            @pl.when(pend_ref[e] > pst_ref[e])
            def _():
                zero_copy(pend_ref[e] // ROW_BLK - 1).start()
            return c

        def wait(e, c):
            @pl.when(pend_ref[e] > pst_ref[e])
            def _():
                zero_copy(pend_ref[e] // ROW_BLK - 1).wait()
            return c

        def start_tail(b, c):
            zero_copy(b).start()
            return c

        def wait_tail(b, c):
            zero_copy(b).wait()
            return c

        used = pend_ref[N_EXPERTS - 1] // ROW_BLK
        total = xs_hbm.shape[0] // blk
        lax.fori_loop(0, N_EXPERTS, start, 0)
        lax.fori_loop(used, total, start_tail, 0)
        for c in range(n_steps):
            load(c).start()
        lax.fori_loop(0, N_EXPERTS, wait, 0)
        lax.fori_loop(used, total, wait_tail, 0)

    load(i).wait()

    def start(t, c):
        src = xbuf.at[pl.ds(pl.multiple_of((i * tb + t) * slab, slab), slab)]
        for k in range(TOP_K):
            dst = dest_ref[k, t]
            pltpu.make_async_copy(
                src, xs_hbm.at[pl.ds(pl.multiple_of(dst * slab, slab), slab)], sem,
            ).start(priority=k % 2)
        return c

    lax.fori_loop(0, tb, start, 0, unroll=4)

    def wait_one_step():
        for k in range(TOP_K):
            pltpu.make_async_copy(
                xbuf.at[pl.ds(0, chunk)], xs_hbm.at[pl.ds(0, chunk)], sem).wait()

    @pl.when(i > 0)
    def _():
        wait_one_step()

    @pl.when(i == n_steps - 1)
    def _():
        wait_one_step()


def _dispatch(pad_starts, pad_ends, dest, x1p, n, cap):
    slab = x1p.shape[0] // n
    tb = DISP_TB
    n_steps = n // tb
    smem_blk = pl.BlockSpec((TOP_K, tb), lambda i, *_: (0, i), memory_space=pltpu.SMEM)
    return pl.pallas_call(
        functools.partial(_dispatch_kernel, slab=slab, n_steps=n_steps),
        grid_spec=pltpu.PrefetchScalarGridSpec(
            num_scalar_prefetch=2,
            grid=(n_steps,),
            in_specs=[smem_blk, pl.BlockSpec(memory_space=pl.ANY)],
            out_specs=pl.BlockSpec(memory_space=pl.ANY),
            scratch_shapes=[pltpu.VMEM(x1p.shape, U32),
                            pltpu.VMEM((ROW_BLK * slab, LANES), U32),
                            pltpu.SemaphoreType.DMA((n_steps,)),
                            pltpu.SemaphoreType.DMA(()),
                            pltpu.SemaphoreType.DMA(())],
        ),
        out_shape=jax.ShapeDtypeStruct((cap * slab, LANES), U32),
        compiler_params=pltpu.CompilerParams(
            dimension_semantics=("arbitrary",),
            vmem_limit_bytes=VMEM_LIMIT_BYTES,
            has_side_effects=True),
        name="dispatch",
    )(pad_starts, pad_ends, dest, x1p)


def _ffn_kernel(ie_ref, ist_ref, inb_ref, used_ref,
                xs_hbm, wgu_hbm, wd_hbm, bgu_ref, bd_ref,
                y_hbm,
                slabs, xbuf, yacc, wg_buf, wu_buf, wd_buf, sem_in, sem_out, sem_w, *, slab, nf):
    w = pl.program_id(0)
    n_items = pl.num_programs(0)
    nb = inb_ref[w]
    blk = ROW_BLK * slab
    half = slab * LANES
    slot_rows = slabs.shape[0] // 2
    tf = wg_buf.shape[2]
    dff = nf * tf
    prev_item = jnp.maximum(w - 1, 0)
    next_item = jnp.minimum(w + 1, n_items - 1)
    n_quads = lax.shift_right_logical(nb, 2)
    rem0 = n_quads * 4

    def rows(r):
        return pl.ds(pl.multiple_of(r * ROW_BLK, ROW_BLK), ROW_BLK)

    def slot_row0(item):
        return (item % 2) * (slot_rows // slab)

    def hbm_blk(ref, item, r):
        return ref.at[pl.ds(pl.multiple_of((ist_ref[item] + r) * blk, blk), blk)]

    def vmem_blk(item, r):
        return slabs.at[pl.ds(pl.multiple_of((item % 2) * slot_rows + r * blk, blk), blk)]

    def for_blocks(item, fn):
        def body(r, c):
            fn(r)
            return c
        lax.fori_loop(0, inb_ref[item], body, 0)

    def load(item):
        return lambda r: pltpu.make_async_copy(hbm_blk(xs_hbm, item, r), vmem_blk(item, r), sem_in)

    def store(item):
        return lambda r: pltpu.make_async_copy(
            vmem_blk(item, r), hbm_blk(y_hbm, item, r), sem_out.at[item % 2])

    def chunk_start(c):
        return c * tf if isinstance(c, int) else pl.multiple_of(c * tf, tf)

    def weight_copies(item, c, wslot):
        e = ie_ref[item]
        c0 = chunk_start(c)
        return (
            pltpu.make_async_copy(wgu_hbm.at[e, :, pl.ds(c0, tf)], wg_buf.at[wslot], sem_w.at[wslot, 0]),
            pltpu.make_async_copy(wgu_hbm.at[e, :, pl.ds(dff + c0, tf)], wu_buf.at[wslot], sem_w.at[wslot, 1]),
            pltpu.make_async_copy(wd_hbm.at[e, pl.ds(c0, tf), :], wd_buf.at[wslot], sem_w.at[wslot, 2]),
        )

    def unpack(row0, n_rows):
        rs = pl.ds(pl.multiple_of(row0, ROW_BLK), n_rows)
        for col in range(slab):
            hi, lo = _load_slab_column(slabs, slot_row0(w) + row0, n_rows, slab, col)
            xbuf[rs, col * LANES:(col + 1) * LANES] = hi.astype(BF16)
            xbuf[rs, half + col * LANES:half + (col + 1) * LANES] = lo.astype(BF16)

    def pack_and_store(blk0, n_blocks, acc):
        _store_slabs(slabs, slot_row0(w) + blk0 * ROW_BLK, _pack_pairs(acc))
        for r in range(n_blocks):
            store(w)(blk0 + r).start()

    def chunk(c, wslot, mode):
        expert = ie_ref[w]
        c0 = chunk_start(c)
        bias_g = bgu_ref[expert, :, pl.ds(c0, tf)]
        bias_u = bgu_ref[expert, :, pl.ds(dff + c0, tf)]
        bias_d = bd_ref[expert]

        def sub(blk0, n_blocks, sub_mode):
            row0 = blk0 * ROW_BLK
            n_rows = n_blocks * ROW_BLK
            rs = pl.ds(pl.multiple_of(row0, ROW_BLK), n_rows)
            if sub_mode == "first":
                unpack(row0, n_rows)
            xt = xbuf[rs, :]
            gate = jnp.minimum(_dot(xt, wg_buf[wslot]) + bias_g, SWIGLU_LIMIT)
            up = jnp.clip(_dot(xt, wu_buf[wslot]) + bias_u, -SWIGLU_LIMIT, SWIGLU_LIMIT)
            act = (up + 1.0) * (gate * _sigmoid(SWIGLU_ALPHA * gate))
            contrib = _dot(act.astype(BF16), wd_buf[wslot])
            if sub_mode == "first":
                yacc[rs, :] = contrib + bias_d
            elif sub_mode == "last":
                pack_and_store(blk0, n_blocks, yacc[rs, :] + contrib)
            else:
                yacc[rs, :] += contrib

        if mode == "first":
            def prepare(r, carry):
                unpack(r * ROW_BLK, ROW_BLK)
                yacc[rows(r), :] = jnp.broadcast_to(bias_d, (ROW_BLK, yacc.shape[1]))
                return carry
            lax.fori_loop(rem0, nb, prepare, 0)

        def quad(q, carry):
            sub(q * 4, 4, mode)
            return carry
        lax.fori_loop(0, n_quads, quad, 0)

        @pl.when((nb & 2) != 0)
        def _():
            sub(rem0, 2, "mid")

        @pl.when((nb & 1) != 0)
        def _():
            sub(nb - 1, 1, "mid")

        if mode == "last":
            def finish(r, carry):
                pack_and_store(r, 1, yacc[rows(r), :])
                return carry
            lax.fori_loop(rem0, nb, finish, 0)

    @pl.when(w == 0)
    def _():
        slabs[0:blk, :] = jnp.zeros((blk, LANES), U32)

        def tail_copy(b):
            return pltpu.make_async_copy(
                slabs.at[pl.ds(0, blk)], y_hbm.at[pl.ds(pl.multiple_of(b * blk, blk), blk)],
                sem_out.at[0])

        def start(b, c):
            tail_copy(b).start()
            return c

        def wait(b, c):
            tail_copy(b).wait()
            return c

        total = y_hbm.shape[0] // blk
        lax.fori_loop(used_ref[0], total, start, 0)
        lax.fori_loop(used_ref[0], total, wait, 0)
        for_blocks(w, lambda r: load(w)(r).start())
        for cp in weight_copies(w, 0, 0):
            cp.start()

    for cp in weight_copies(w, 0, 0):
        cp.wait()
    for cp in weight_copies(w, 1, 1):
        cp.start()
    for_blocks(w, lambda r: load(w)(r).wait())
    chunk(0, 0, "first")

    def middle(c, carry):
        wslot = c % 2
        for cp in weight_copies(w, c, wslot):
            cp.wait()
        for cp in weight_copies(w, c + 1, 1 - wslot):
            cp.start()
        chunk(c, wslot, "mid")
        return carry
    lax.fori_loop(1, nf - 1, middle, 0)

    last_slot = (nf - 1) % 2
    for cp in weight_copies(w, nf - 1, last_slot):
        cp.wait()

    @pl.when(w >= 1)
    def _():
        for_blocks(prev_item, lambda r: store(prev_item)(r).wait())

    @pl.when(w + 1 < n_items)
    def _():
        for_blocks(next_item, lambda r: load(next_item)(r).start())
        for cp in weight_copies(next_item, 0, 0):
            cp.start()

    chunk(nf - 1, last_slot, "last")

    @pl.when(w == n_items - 1)
    def _():
        for_blocks(w, lambda r: store(w)(r).wait())


def _ffn(item_e, item_st, item_nb, used_items, used_blocks, xs, w_gu, b_gu, w_down, b_down, cap):
    slab = xs.shape[0] // cap
    n_e, d, f2 = w_gu.shape
    dff = f2 // 2
    tf = FFN_TF
    nf = dff // tf
    assert nf >= 2 and nf % 2 == 0, "weight slots alternate per d_ff chunk and restart at 0 for every item"
    ts = ITEM_BLKS * ROW_BLK
    hbm = pl.BlockSpec(memory_space=pl.ANY)
    resident = pl.BlockSpec(memory_space=pltpu.VMEM)
    return pl.pallas_call(
        functools.partial(_ffn_kernel, slab=slab, nf=nf),
        grid_spec=pltpu.PrefetchScalarGridSpec(
            num_scalar_prefetch=4,
            grid=(used_items[0],),
            in_specs=[hbm, hbm, hbm, resident, resident],
            out_specs=hbm,
            scratch_shapes=[pltpu.VMEM((2 * ts * slab, LANES), U32),
                            pltpu.VMEM((ts, d), BF16),
                            pltpu.VMEM((ts, d), F32),
                            pltpu.VMEM((2, d, tf), F32),
                            pltpu.VMEM((2, d, tf), F32),
                            pltpu.VMEM((2, tf, d), F32),
                            pltpu.SemaphoreType.DMA(()),
                            pltpu.SemaphoreType.DMA((2,)),
                            pltpu.SemaphoreType.DMA((2, 3))],
        ),
        out_shape=jax.ShapeDtypeStruct((cap * slab, LANES), U32),
        compiler_params=pltpu.CompilerParams(
            dimension_semantics=("arbitrary",),
            vmem_limit_bytes=VMEM_LIMIT_BYTES,
            has_side_effects=True),
        name="ffn",
    )(item_e, item_st, item_nb, used_blocks, xs, w_gu, w_down, b_gu.reshape(n_e, 1, f2),
      b_down.reshape(n_e, 1, d))


def _final_kernel(dcur_ref, dnxt_ref, gate_ref, x1_ref, p_ref, y_hbm,
                  g2_ref, b2_ref, wpg_ref, bpg_ref, wple_ref, g3_ref, b3_ref,
                  o_ref, ybuf, ffn_ref, gbuf, sem, *, alpha, slab):
    tm = x1_ref.shape[0]
    half = slab * LANES
    i = pl.program_id(0)
    n_steps = pl.num_programs(0)
    slot = i % 2
    slot_tokens = TOP_K * tm
    slot_rows = slot_tokens * slab

    def gather(d_ref, into):
        def start(t, c):
            for k in range(TOP_K):
                src = d_ref[k, t]
                dst = (into * slot_tokens + k * tm + t) * slab
                pltpu.make_async_copy(
                    y_hbm.at[pl.ds(pl.multiple_of(src * slab, slab), slab)],
                    ybuf.at[pl.ds(pl.multiple_of(dst, slab), slab)], sem.at[into],
                ).start(priority=k % 2)
            return c
        lax.fori_loop(0, tm, start, 0, unroll=4)

    @pl.when(i == 0)
    def _():
        gather(dcur_ref, slot)

    @pl.when(i + 1 < n_steps)
    def _():
        gather(dnxt_ref, 1 - slot)

    pltpu.make_async_copy(
        y_hbm.at[pl.ds(0, slot_rows)],
        ybuf.at[pl.ds(pl.multiple_of(slot * slot_rows, slot_rows), slot_rows)], sem.at[slot]).wait()

    for k in range(TOP_K):
        gbuf[k] = jnp.broadcast_to(gate_ref[:, k:k + 1], (tm, LANES))

    hr = tm // FINAL_SPLIT
    for part in range(FINAL_SPLIT):
        rs = slice(part * hr, (part + 1) * hr)
        for col in range(slab):
            acc_hi = acc_lo = None
            for k in range(TOP_K):
                hi, lo = _load_slab_column(
                    ybuf, slot * slot_tokens + k * tm + part * hr, hr, slab, col)
                g = gbuf[k, rs, :]
                acc_hi = hi * g if acc_hi is None else acc_hi + hi * g
                acc_lo = lo * g if acc_lo is None else acc_lo + lo * g
            ffn_ref[rs, col * LANES:(col + 1) * LANES] = acc_hi
            ffn_ref[rs, half + col * LANES:half + (col + 1) * LANES] = acc_lo
        h2 = _layer_norm(alpha * x1_ref[rs, :] + ffn_ref[rs, :], g2_ref[...], b2_ref[...])
        pg = _sigmoid(_dot(h2.astype(BF16), wpg_ref[...]) + bpg_ref[...])
        pe = _dot(p_ref[rs, :].astype(BF16), wple_ref[...])
        o_ref[rs, :] = _layer_norm(alpha * h2 + pg * pe, g3_ref[...], b3_ref[...])


def _final(dest, gates_t, x1, p2, y, g2, b2, wpg, bpg, wple, g3, b3, cap, *, alpha):
    n, d = x1.shape
    slab = y.shape[0] // cap
    tm = MIX_TM
    n_steps = n // tm
    dest_cur = pl.BlockSpec((TOP_K, tm), lambda i: (0, i), memory_space=pltpu.SMEM)
    dest_next = pl.BlockSpec((TOP_K, tm), lambda i: (0, jnp.minimum(i + 1, n_steps - 1)),
                             memory_space=pltpu.SMEM)
    const = lambda i: (0, 0)
    row_blk = lambda width: pl.BlockSpec((tm, width), lambda i: (i, 0))
    return pl.pallas_call(
        functools.partial(_final_kernel, alpha=alpha, slab=slab),
        grid_spec=pltpu.PrefetchScalarGridSpec(
            num_scalar_prefetch=0,
            grid=(n_steps,),
            in_specs=[dest_cur, dest_next, row_blk(TOP_K), row_blk(d), row_blk(p2.shape[1]),
                      pl.BlockSpec(memory_space=pl.ANY),
                      pl.BlockSpec(g2.shape, const), pl.BlockSpec(b2.shape, const),
                      pl.BlockSpec(wpg.shape, const), pl.BlockSpec(bpg.shape, const),
                      pl.BlockSpec(wple.shape, const),
                      pl.BlockSpec(g3.shape, const), pl.BlockSpec(b3.shape, const)],
            out_specs=row_blk(d),
            scratch_shapes=[pltpu.VMEM((2 * TOP_K * tm * slab, LANES), U32),
                            pltpu.VMEM((tm, d), F32),
                            pltpu.VMEM((TOP_K, tm, LANES), F32),
                            pltpu.SemaphoreType.DMA((2,))],
        ),
        out_shape=jax.ShapeDtypeStruct((n, d), F32),
        compiler_params=pltpu.CompilerParams(
            dimension_semantics=("arbitrary",),
            vmem_limit_bytes=VMEM_LIMIT_BYTES),
        name="final",
    )(dest, dest, gates_t, x1, p2, y, g2, b2, wpg, bpg, wple, g3, b3)


def _plan(counts, n_items):
    nblk = (counts + ROW_BLK - 1) // ROW_BLK
    blk_end = jnp.cumsum(nblk)
    blk_start = blk_end - nblk
    items_per = (nblk + ITEM_BLKS - 1) // ITEM_BLKS
    item_end = jnp.cumsum(items_per)
    item_start = item_end - items_per
    total = item_end[-1]
    w = jnp.arange(n_items, dtype=jnp.int32)
    w_eff = jnp.minimum(w, total - 1)
    e = jnp.sum((item_end[None, :] <= w_eff[:, None]).astype(jnp.int32), axis=1)
    e = jnp.minimum(e, N_EXPERTS - 1)
    j = w_eff - item_start[e]
    st = blk_start[e] + j * ITEM_BLKS
    nb = jnp.where(w < total, jnp.clip(nblk[e] - j * ITEM_BLKS, 0, ITEM_BLKS), 0)
    return ((blk_start * ROW_BLK).astype(jnp.int32), (blk_end * ROW_BLK).astype(jnp.int32),
            e, st.astype(jnp.int32), nb.astype(jnp.int32), total.reshape(1).astype(jnp.int32))


def kernel(x, p, w_in, w_pool, pool_scale, conv_w, w_br_a, w_br_b, w_o, ln1_g, ln1_b,
           w_router, b_router, w_gu, b_gu, w_down, b_down, ln2_g, ln2_b,
           w_pg, b_pg, w_ple, ln3_g, ln3_b):
    depth = w_in.shape[0]
    bsz, seq, d = x.shape
    n = bsz * seq
    alpha = (2.0 * depth) ** 0.25
    nk = n * TOP_K
    n_row_blocks = (nk + N_EXPERTS * (ROW_BLK - 1) + ROW_BLK - 1) // ROW_BLK
    cap = n_row_blocks * ROW_BLK
    n_items = N_EXPERTS + n_row_blocks // ITEM_BLKS
    row = lambda v: v.reshape(1, -1)

    h = x.reshape(n, d)
    for i in range(depth):
        proj = _proj(h, w_in[i])
        x1, x1p, tope, gates, rank, cnt = _mixer(
            proj, h, w_pool[i].astype(BF16), row(pool_scale[i]), conv_w[i],
            w_br_a[i].astype(BF16), w_br_b[i].astype(BF16), w_o[i].astype(BF16),
            row(ln1_g[i]), row(ln1_b[i]), w_router[i].T.astype(BF16),
            b_router[i].reshape(N_EXPERTS, 1), seq=seq, alpha=alpha)
        pad_starts, pad_ends, item_e, item_st, item_nb, used_items = _plan(cnt[:, 0], n_items)
        dest = _dest(pad_starts, tope, rank)
        xs = _dispatch(pad_starts, pad_ends, dest, x1p, n, cap)
        used_blocks = pad_ends[N_EXPERTS - 1:] // ROW_BLK
        y = _ffn(item_e, item_st, item_nb, used_items, used_blocks, xs,
                 w_gu[i], b_gu[i], w_down[i], b_down[i], cap)
        h = _final(dest, gates.T, x1, p[i].reshape(n, -1), y,
                   row(ln2_g[i]), row(ln2_b[i]), w_pg[i].astype(BF16), row(b_pg[i]),
                   w_ple[i].astype(BF16), row(ln3_g[i]), row(ln3_b[i]), cap, alpha=alpha)
    return h.reshape(bsz, seq, d)
```

```python
import functools

import jax
import jax.numpy as jnp
import numpy as np
from jax import lax
from jax.experimental import pallas as pl
from jax.experimental.pallas import tpu as pltpu

F32 = jnp.float32
BF16 = jnp.bfloat16
U32 = jnp.uint32

POOL_WINDOWS = (2, 4, 8, 16)
N_POOL_GROUPS = 4
CONV_K = 3
N_EXPERTS = 32
TOP_K = 4
SWIGLU_LIMIT = 7.0
SWIGLU_ALPHA = 1.702
LN_EPS = 1e-5

LANES = 128
VMEM_LIMIT_BYTES = 56 * 1024 * 1024

HALO = 16
POOL_PAD = LANES
PROJ_BM = 1024
PROJ_BN = 1024
PROJ_SLOTS = 3
MIX_TM = 256
MIX_SPLIT = 1
FINAL_SPLIT = 2
DISP_TB = 512
ROW_BLK = 128
ITEM_BLKS = 10
FFN_TF = 512


def _layer_norm(v, g, b):
    mu = jnp.mean(v, axis=-1, keepdims=True)
    c = v - mu
    var = jnp.mean(c * c, axis=-1, keepdims=True)
    return c * lax.rsqrt(var + LN_EPS) * g + b


def _dot(a, b):
    return jnp.dot(a, b, preferred_element_type=F32)


def _sigmoid(v):
    return 0.5 * jnp.tanh(0.5 * v) + 0.5


def _pack_pairs(v):
    half = v.shape[1] // 2
    bits = pltpu.bitcast(v.astype(BF16).astype(F32), U32)
    return (bits[:, 0:half] & U32(0xFFFF0000)) | (bits[:, half:] >> 16)


def _store_slabs(ref, row0, packed):
    rows, width = packed.shape
    s = width // LANES
    for c in range(s):
        ref[pl.ds(row0 * s + c, rows, stride=s), :] = packed[:, c * LANES:(c + 1) * LANES]


def _load_slab_column(ref, row0, rows, s, c):
    p = ref[pl.ds(row0 * s + c, rows, stride=s), :]
    return pltpu.bitcast(p & U32(0xFFFF0000), F32), pltpu.bitcast(p << 16, F32)


def _proj_kernel(x_hbm, w_hbm, o_ref, xbuf, wbuf, xsem, sem, *, n_col_blocks, n_steps):
    i = pl.program_id(0)
    j = pl.program_id(1)
    s = i * n_col_blocks + j
    bn = wbuf.shape[2]
    bm = xbuf.shape[1]

    def fetch_x(blk):
        slot = blk % 2
        return pltpu.make_async_copy(
            x_hbm.at[pl.ds(pl.multiple_of(blk * bm, bm), bm)], xbuf.at[slot], xsem.at[slot])

    @pl.when(s == 0)
    def _():
        fetch_x(0).start()

    @pl.when((j == 1) & (i + 1 < pl.num_programs(0)))
    def _():
        fetch_x(i + 1).start()

    @pl.when(j == 0)
    def _():
        fetch_x(i).wait()

    def fetch(step):
        col0 = pl.multiple_of((step % n_col_blocks) * bn, bn)
        slot = step % PROJ_SLOTS
        return pltpu.make_async_copy(w_hbm.at[:, pl.ds(col0, bn)], wbuf.at[slot], sem.at[slot])

    @pl.when(s == 0)
    def _():
        for step in range(PROJ_SLOTS - 1):
            fetch(step).start()

    @pl.when(s + PROJ_SLOTS - 1 < n_steps)
    def _():
        fetch(s + PROJ_SLOTS - 1).start()

    fetch(s).wait()
    w = wbuf[s % PROJ_SLOTS]
    o_ref[...] = _dot(xbuf[i % 2].astype(BF16), w.astype(BF16)).astype(o_ref.dtype)


def _proj(xb, wb):
    n, d = xb.shape
    n_in = wb.shape[1]
    n_row_blocks, n_col_blocks = n // PROJ_BM, n_in // PROJ_BN
    n_steps = n_row_blocks * n_col_blocks
    assert n_steps >= PROJ_SLOTS and n_col_blocks >= 2
    return pl.pallas_call(
        functools.partial(_proj_kernel, n_col_blocks=n_col_blocks, n_steps=n_steps),
        grid=(n_row_blocks, n_col_blocks),
        in_specs=[pl.BlockSpec(memory_space=pl.ANY),
                  pl.BlockSpec(memory_space=pl.ANY)],
        out_specs=pl.BlockSpec((PROJ_BM, PROJ_BN), lambda i, j: (i, j)),
        out_shape=jax.ShapeDtypeStruct((n, n_in), BF16),
        scratch_shapes=[pltpu.VMEM((2, PROJ_BM, d), F32),
                        pltpu.VMEM((PROJ_SLOTS, d, PROJ_BN), F32),
                        pltpu.SemaphoreType.DMA((2,)),
                        pltpu.SemaphoreType.DMA((PROJ_SLOTS,))],
        compiler_params=pltpu.CompilerParams(
            dimension_semantics=("arbitrary", "arbitrary"),
            vmem_limit_bytes=VMEM_LIMIT_BYTES),
        name="proj",
    )(xb, wb)


def _mixer_kernel(proj_ref, halo_ref, x_ref, band_ref, wpool_ref, pscale_ref, convw_ref,
                  wbra_ref, wbrb_ref, wo_ref, g1_ref, b1_ref, wrt_ref, br_ref,
                  x1_ref, x1p_ref, tope_ref, gate_ref, rank_ref, cnt_ref,
                  ext_ref, cext_ref, carry_ref, *, seq, alpha, pw, cw, d):
    tm = x_ref.shape[0]
    i = pl.program_id(0)
    blocks_per_seq = seq // tm
    j = i % blocks_per_seq
    keep_halo = (j > 0).astype(F32)
    row = lax.broadcasted_iota(jnp.int32, (tm, 1), 0)
    pos1 = (j * tm + row + 1).astype(F32)

    @pl.when(i == 0)
    def _():
        carry_ref[...] = jnp.zeros_like(carry_ref)
        ext_ref[0:POOL_PAD - HALO, :] = jnp.zeros((POOL_PAD - HALO, pw), BF16)

    o1, o2, o3 = pw + cw, pw + 2 * cw, pw + 3 * cw
    ext_ref[POOL_PAD - HALO:POOL_PAD, :] = halo_ref[:, 0:pw] * keep_halo.astype(BF16)
    ext_ref[POOL_PAD:POOL_PAD + tm, :] = proj_ref[:, 0:pw]
    cext_ref[0:HALO, :] = (halo_ref[:, o1:o2].astype(F32) * halo_ref[:, o2:o3].astype(F32)) * keep_halo
    cext_ref[HALO:HALO + tm, :] = proj_ref[:, o1:o2].astype(F32) * proj_ref[:, o2:o3].astype(F32)

    gw = pw // N_POOL_GROUPS
    hr = tm // MIX_SPLIT
    logit_parts = []
    for part in range(MIX_SPLIT):
        r0 = part * hr
        rs = slice(r0, r0 + hr)

        a_parts = []
        for g, w in enumerate(POOL_WINDOWS):
            cols = slice(g * gw, (g + 1) * gw)
            s = _dot(band_ref[g, rs, :], ext_ref[:, cols])
            cnt = jnp.minimum(pos1[rs, :], float(w))
            pooled = s * (1.0 / cnt) - proj_ref[rs, cols].astype(F32)
            a_g = _dot(pooled.astype(BF16), wpool_ref[g]) * pscale_ref[:, cols]
            a_parts.append(a_g.astype(BF16))
        br_a = _dot(jnp.concatenate(a_parts, axis=1), wbra_ref[...])

        conv = cext_ref[HALO + r0:HALO + r0 + hr, :] * convw_ref[CONV_K - 1:CONV_K, :]
        for k in range(CONV_K - 1):
            sh = CONV_K - 1 - k
            conv = conv + cext_ref[HALO + r0 - sh:HALO + r0 - sh + hr, :] * convw_ref[k:k + 1, :]
        b = proj_ref[rs, pw:o1].astype(F32) * conv
        br_b = _dot(b.astype(BF16), wbrb_ref[...])

        g_a = proj_ref[rs, o3:o3 + d].astype(F32)
        g_b = proj_ref[rs, o3 + d:o3 + 2 * d].astype(F32)
        m = _sigmoid(g_a) * br_a + _sigmoid(g_b) * br_b
        mix = _dot(m.astype(BF16), wo_ref[...])
        x1 = _layer_norm(alpha * x_ref[rs, :] + mix, g1_ref[...], b1_ref[...])
        x1_ref[rs, :] = x1
        _store_slabs(x1p_ref, r0, _pack_pairs(x1))

        logit_parts.append(lax.dot_general(wrt_ref[...], x1.astype(BF16), (((1,), (1,)), ((), ())),
                                           preferred_element_type=F32))
    logits = jnp.concatenate(logit_parts, axis=1) + br_ref[...]
    e_iota = lax.broadcasted_iota(jnp.int32, logits.shape, 0)
    vals, idxs = [], []
    l = logits
    for _ in range(TOP_K):
        mx = jnp.max(l, axis=0, keepdims=True)
        ix = jnp.min(jnp.where(l == mx, e_iota, N_EXPERTS), axis=0, keepdims=True)
        vals.append(mx)
        idxs.append(ix)
        l = jnp.where(e_iota == ix, -jnp.inf, l)
    exps = [jnp.exp(v - vals[0]) for v in vals]
    denom = exps[0] + exps[1] + exps[2] + exps[3]
    onehot = jnp.zeros(logits.shape, F32)
    for k in range(TOP_K):
        tope_ref[k:k + 1, :] = idxs[k]
        gate_ref[k:k + 1, :] = exps[k] / denom
        onehot = onehot + (e_iota == idxs[k]).astype(F32)

    r_i = lax.broadcasted_iota(jnp.int32, (tm, tm), 0)
    c_i = lax.broadcasted_iota(jnp.int32, (tm, tm), 1)
    before = (r_i < c_i).astype(BF16)
    seen = _dot(onehot.astype(BF16), before) + carry_ref[:, 0:1]
    for k in range(TOP_K):
        rk = jnp.sum(jnp.where(e_iota == idxs[k], seen, 0.0), axis=0, keepdims=True)
        rank_ref[k:k + 1, :] = rk.astype(jnp.int32)
    carry_ref[...] = carry_ref[...] + jnp.sum(onehot, axis=1, keepdims=True)
    cnt_ref[...] = carry_ref[...].astype(jnp.int32)


def _mixer(proj, x2, wpool, pscale, convw, wbra, wbrb, wo, g1, b1, wrt, br, *, seq, alpha):
    n, d = x2.shape
    n_in = proj.shape[1]
    pw = wbra.shape[0]
    cw = wbrb.shape[0]
    tm = MIX_TM
    slab = d // 2 // LANES
    hb = tm // HALO
    const = lambda i: (0, 0)
    kern = functools.partial(_mixer_kernel, seq=seq, alpha=alpha, pw=pw, cw=cw, d=d)
    t_idx = np.arange(tm)[:, None] + POOL_PAD
    s_idx = np.arange(POOL_PAD + tm)[None, :]
    band = jnp.asarray(np.stack([(s_idx > t_idx - w) & (s_idx <= t_idx) for w in POOL_WINDOWS]), BF16)
    return pl.pallas_call(
        kern,
        grid=(n // tm,),
        in_specs=[
            pl.BlockSpec((tm, n_in), lambda i: (i, 0)),
            pl.BlockSpec((HALO, n_in // 2), lambda i: (jnp.maximum(i * hb - 1, 0), 0)),
            pl.BlockSpec((tm, d), lambda i: (i, 0)),
            pl.BlockSpec(band.shape, lambda i: (0, 0, 0)),
            pl.BlockSpec(wpool.shape, lambda i: (0, 0, 0)),
            pl.BlockSpec(pscale.shape, const),
            pl.BlockSpec(convw.shape, const),
            pl.BlockSpec(wbra.shape, const),
            pl.BlockSpec(wbrb.shape, const),
            pl.BlockSpec(wo.shape, const),
            pl.BlockSpec(g1.shape, const),
            pl.BlockSpec(b1.shape, const),
            pl.BlockSpec(wrt.shape, const),
            pl.BlockSpec(br.shape, const),
        ],
        out_specs=[
            pl.BlockSpec((tm, d), lambda i: (i, 0)),
            pl.BlockSpec((tm * slab, LANES), lambda i: (i, 0)),
            pl.BlockSpec((TOP_K, tm), lambda i: (0, i)),
            pl.BlockSpec((TOP_K, tm), lambda i: (0, i)),
            pl.BlockSpec((TOP_K, tm), lambda i: (0, i)),
            pl.BlockSpec((N_EXPERTS, LANES), const),
        ],
        out_shape=[
            jax.ShapeDtypeStruct((n, d), F32),
            jax.ShapeDtypeStruct((n * slab, LANES), U32),
            jax.ShapeDtypeStruct((TOP_K, n), jnp.int32),
            jax.ShapeDtypeStruct((TOP_K, n), F32),
            jax.ShapeDtypeStruct((TOP_K, n), jnp.int32),
            jax.ShapeDtypeStruct((N_EXPERTS, LANES), jnp.int32),
        ],
        scratch_shapes=[pltpu.VMEM((POOL_PAD + tm, pw), BF16),
                        pltpu.VMEM((HALO + tm, cw), F32),
                        pltpu.VMEM((N_EXPERTS, LANES), F32)],
        compiler_params=pltpu.CompilerParams(
            dimension_semantics=("arbitrary",),
            vmem_limit_bytes=VMEM_LIMIT_BYTES),
        name="mixer",
    )(proj, proj, x2, band, wpool, pscale, convw, wbra, wbrb, wo, g1, b1, wrt, br)


def _dest_kernel(pst_ref, tope_ref, rank_ref, dest_ref):
    tope = tope_ref[...]
    dest = rank_ref[...]
    for e in range(N_EXPERTS):
        dest = dest + jnp.where(tope == e, pst_ref[e], 0)
    dest_ref[...] = dest


def _dest(pad_starts, tope, rank):
    whole = pl.BlockSpec(tope.shape, lambda i, *_: (0, 0))
    return pl.pallas_call(
        _dest_kernel,
        grid_spec=pltpu.PrefetchScalarGridSpec(
            num_scalar_prefetch=1, grid=(1,), in_specs=[whole, whole], out_specs=whole),
        out_shape=jax.ShapeDtypeStruct(tope.shape, jnp.int32),
        name="dest",
    )(pad_starts, tope, rank)


def _dispatch_kernel(pst_ref, pend_ref, dest_ref, x1p_hbm, xs_hbm, xbuf, zero_ref, ld_sem, sem, zsem,
                     *, slab, n_steps):
    tb = dest_ref.shape[1]
    i = pl.program_id(0)
    blk = ROW_BLK * slab
    chunk = tb * slab

    def load(c):
        row0 = c * chunk if isinstance(c, int) else pl.multiple_of(c * chunk, chunk)
        rows = pl.ds(row0, chunk)
        return pltpu.make_async_copy(x1p_hbm.at[rows], xbuf.at[rows], ld_sem.at[c])

    def zero_copy(b):
        return pltpu.make_async_copy(
            zero_ref, xs_hbm.at[pl.ds(pl.multiple_of(b * blk, blk), blk)], zsem)

    @pl.when(i == 0)
    def _():
        zero_ref[...] = jnp.zeros_like(zero_ref)

        def start(e, c):
            @pl.when(pend_ref[e] > pst_ref[e])
            def _():
                zero_copy(pend_ref[e] // ROW_BLK - 1).start()
            return c

        def wait(e, c):
            @pl.when(pend_ref[e] > pst_ref[e])
            def _():
                zero_copy(pend_ref[e] // ROW_BLK - 1).wait()
            return c

        def start_tail(b, c):
            zero_copy(b).start()
            return c

        def wait_tail(b, c):
            zero_copy(b).wait()
            return c

        used = pend_ref[N_EXPERTS - 1] // ROW_BLK
        total = xs_hbm.shape[0] // blk
        lax.fori_loop(0, N_EXPERTS, start, 0)
        lax.fori_loop(used, total, start_tail, 0)
        for c in range(n_steps):
            load(c).start()
        lax.fori_loop(0, N_EXPERTS, wait, 0)
        lax.fori_loop(used, total, wait_tail, 0)

    load(i).wait()

    def start(t, c):
        src = xbuf.at[pl.ds(pl.multiple_of((i * tb + t) * slab, slab), slab)]
        for k in range(TOP_K):
            dst = dest_ref[k, t]
            pltpu.make_async_copy(
                src, xs_hbm.at[pl.ds(pl.multiple_of(dst * slab, slab), slab)], sem,
            ).start(priority=k % 2)
        return c

    lax.fori_loop(0, tb, start, 0, unroll=4)

    def wait_one_step():
        for k in range(TOP_K):
            pltpu.make_async_copy(
                xbuf.at[pl.ds(0, chunk)], xs_hbm.at[pl.ds(0, chunk)], sem).wait()

    @pl.when(i > 0)
    def _():
        wait_one_step()

    @pl.when(i == n_steps - 1)
    def _():
        wait_one_step()


def _dispatch(pad_starts, pad_ends, dest, x1p, n, cap):
    slab = x1p.shape[0] // n
    tb = DISP_TB
    n_steps = n // tb
    smem_blk = pl.BlockSpec((TOP_K, tb), lambda i, *_: (0, i), memory_space=pltpu.SMEM)
    return pl.pallas_call(
        functools.partial(_dispatch_kernel, slab=slab, n_steps=n_steps),
        grid_spec=pltpu.PrefetchScalarGridSpec(
            num_scalar_prefetch=2,
            grid=(n_steps,),
            in_specs=[smem_blk, pl.BlockSpec(memory_space=pl.ANY)],
            out_specs=pl.BlockSpec(memory_space=pl.ANY),
            scratch_shapes=[pltpu.VMEM(x1p.shape, U32),
                            pltpu.VMEM((ROW_BLK * slab, LANES), U32),
                            pltpu.SemaphoreType.DMA((n_steps,)),
                            pltpu.SemaphoreType.DMA(()),
                            pltpu.SemaphoreType.DMA(())],
        ),
        out_shape=jax.ShapeDtypeStruct((cap * slab, LANES), U32),
        compiler_params=pltpu.CompilerParams(
            dimension_semantics=("arbitrary",),
            vmem_limit_bytes=VMEM_LIMIT_BYTES,
            has_side_effects=True),
        name="dispatch",
    )(pad_starts, pad_ends, dest, x1p)


def _ffn_kernel(ie_ref, ist_ref, inb_ref, used_ref,
                xs_hbm, wgu_hbm, wd_hbm, bgu_ref, bd_ref,
                y_hbm,
                slabs, xbuf, yacc, wg_buf, wu_buf, wd_buf, sem_in, sem_out, sem_w, *, slab, nf):
    w = pl.program_id(0)
    n_items = pl.num_programs(0)
    nb = inb_ref[w]
    blk = ROW_BLK * slab
    half = slab * LANES
    slot_rows = slabs.shape[0] // 2
    tf = wg_buf.shape[2]
    dff = nf * tf
    prev_item = jnp.maximum(w - 1, 0)
    next_item = jnp.minimum(w + 1, n_items - 1)
    n_quads = lax.shift_right_logical(nb, 2)
    rem0 = n_quads * 4

    def rows(r):
        return pl.ds(pl.multiple_of(r * ROW_BLK, ROW_BLK), ROW_BLK)

    def slot_row0(item):
        return (item % 2) * (slot_rows // slab)

    def hbm_blk(ref, item, r):
        return ref.at[pl.ds(pl.multiple_of((ist_ref[item] + r) * blk, blk), blk)]

    def vmem_blk(item, r):
        return slabs.at[pl.ds(pl.multiple_of((item % 2) * slot_rows + r * blk, blk), blk)]

    def for_blocks(item, fn):
        def body(r, c):
            fn(r)
            return c
        lax.fori_loop(0, inb_ref[item], body, 0)

    def load(item):
        return lambda r: pltpu.make_async_copy(hbm_blk(xs_hbm, item, r), vmem_blk(item, r), sem_in)

    def store(item):
        return lambda r: pltpu.make_async_copy(
            vmem_blk(item, r), hbm_blk(y_hbm, item, r), sem_out.at[item % 2])

    def chunk_start(c):
        return c * tf if isinstance(c, int) else pl.multiple_of(c * tf, tf)

    def weight_copies(item, c, wslot):
        e = ie_ref[item]
        c0 = chunk_start(c)
        return (
            pltpu.make_async_copy(wgu_hbm.at[e, :, pl.ds(c0, tf)], wg_buf.at[wslot], sem_w.at[wslot, 0]),
            pltpu.make_async_copy(wgu_hbm.at[e, :, pl.ds(dff + c0, tf)], wu_buf.at[wslot], sem_w.at[wslot, 1]),
            pltpu.make_async_copy(wd_hbm.at[e, pl.ds(c0, tf), :], wd_buf.at[wslot], sem_w.at[wslot, 2]),
        )

    def unpack(row0, n_rows):
        rs = pl.ds(pl.multiple_of(row0, ROW_BLK), n_rows)
        for col in range(slab):
            hi, lo = _load_slab_column(slabs, slot_row0(w) + row0, n_rows, slab, col)
            xbuf[rs, col * LANES:(col + 1) * LANES] = hi.astype(BF16)
            xbuf[rs, half + col * LANES:half + (col + 1) * LANES] = lo.astype(BF16)

    def pack_and_store(blk0, n_blocks, acc):
        _store_slabs(slabs, slot_row0(w) + blk0 * ROW_BLK, _pack_pairs(acc))
        for r in range(n_blocks):
            store(w)(blk0 + r).start()

    def chunk(c, wslot, mode):
        expert = ie_ref[w]
        c0 = chunk_start(c)
        bias_g = bgu_ref[expert, :, pl.ds(c0, tf)]
        bias_u = bgu_ref[expert, :, pl.ds(dff + c0, tf)]
        bias_d = bd_ref[expert]

        def sub(blk0, n_blocks, sub_mode):
            row0 = blk0 * ROW_BLK
            n_rows = n_blocks * ROW_BLK
            rs = pl.ds(pl.multiple_of(row0, ROW_BLK), n_rows)
            if sub_mode == "first":
                unpack(row0, n_rows)
            xt = xbuf[rs, :]
            gate = jnp.minimum(_dot(xt, wg_buf[wslot]) + bias_g, SWIGLU_LIMIT)
            up = jnp.clip(_dot(xt, wu_buf[wslot]) + bias_u, -SWIGLU_LIMIT, SWIGLU_LIMIT)
            act = (up + 1.0) * (gate * _sigmoid(SWIGLU_ALPHA * gate))
            contrib = _dot(act.astype(BF16), wd_buf[wslot])
            if sub_mode == "first":
                yacc[rs, :] = contrib + bias_d
            elif sub_mode == "last":
                pack_and_store(blk0, n_blocks, yacc[rs, :] + contrib)
            else:
                yacc[rs, :] += contrib

        if mode == "first":
            def prepare(r, carry):
                unpack(r * ROW_BLK, ROW_BLK)
                yacc[rows(r), :] = jnp.broadcast_to(bias_d, (ROW_BLK, yacc.shape[1]))
                return carry
            lax.fori_loop(rem0, nb, prepare, 0)

        def quad(q, carry):
            sub(q * 4, 4, mode)
            return carry
        lax.fori_loop(0, n_quads, quad, 0)

        @pl.when((nb & 2) != 0)
        def _():
            sub(rem0, 2, "mid")

        @pl.when((nb & 1) != 0)
        def _():
            sub(nb - 1, 1, "mid")

        if mode == "last":
            def finish(r, carry):
                pack_and_store(r, 1, yacc[rows(r), :])
                return carry
            lax.fori_loop(rem0, nb, finish, 0)

    @pl.when(w == 0)
    def _():
        slabs[0:blk, :] = jnp.zeros((blk, LANES), U32)

        def tail_copy(b):
            return pltpu.make_async_copy(
                slabs.at[pl.ds(0, blk)], y_hbm.at[pl.ds(pl.multiple_of(b * blk, blk), blk)],
                sem_out.at[0])

        def start(b, c):
            tail_copy(b).start()
            return c

        def wait(b, c):
            tail_copy(b).wait()
            return c

        total = y_hbm.shape[0] // blk
        lax.fori_loop(used_ref[0], total, start, 0)
        lax.fori_loop(used_ref[0], total, wait, 0)
        for_blocks(w, lambda r: load(w)(r).start())
        for cp in weight_copies(w, 0, 0):
            cp.start()

    for cp in weight_copies(w, 0, 0):
        cp.wait()
    for cp in weight_copies(w, 1, 1):
        cp.start()
    for_blocks(w, lambda r: load(w)(r).wait())
    chunk(0, 0, "first")

    def middle(c, carry):
        wslot = c % 2
        for cp in weight_copies(w, c, wslot):
            cp.wait()
        for cp in weight_copies(w, c + 1, 1 - wslot):
            cp.start()
        chunk(c, wslot, "mid")
        return carry
    lax.fori_loop(1, nf - 1, middle, 0)

    last_slot = (nf - 1) % 2
    for cp in weight_copies(w, nf - 1, last_slot):
        cp.wait()

    @pl.when(w >= 1)
    def _():
        for_blocks(prev_item, lambda r: store(prev_item)(r).wait())

    @pl.when(w + 1 < n_items)
    def _():
        for_blocks(next_item, lambda r: load(next_item)(r).start())
        for cp in weight_copies(next_item, 0, 0):
            cp.start()

    chunk(nf - 1, last_slot, "last")

    @pl.when(w == n_items - 1)
    def _():
        for_blocks(w, lambda r: store(w)(r).wait())


def _ffn(item_e, item_st, item_nb, used_items, used_blocks, xs, w_gu, b_gu, w_down, b_down, cap):
    slab = xs.shape[0] // cap
    n_e, d, f2 = w_gu.shape
    dff = f2 // 2
    tf = FFN_TF
    nf = dff // tf
    assert nf >= 2 and nf % 2 == 0, "weight slots alternate per d_ff chunk and restart at 0 for every item"
    ts = ITEM_BLKS * ROW_BLK
    hbm = pl.BlockSpec(memory_space=pl.ANY)
    resident = pl.BlockSpec(memory_space=pltpu.VMEM)
    return pl.pallas_call(
        functools.partial(_ffn_kernel, slab=slab, nf=nf),
        grid_spec=pltpu.PrefetchScalarGridSpec(
            num_scalar_prefetch=4,
            grid=(used_items[0],),
            in_specs=[hbm, hbm, hbm, resident, resident],
            out_specs=hbm,
            scratch_shapes=[pltpu.VMEM((2 * ts * slab, LANES), U32),
                            pltpu.VMEM((ts, d), BF16),
                            pltpu.VMEM((ts, d), F32),
                            pltpu.VMEM((2, d, tf), F32),
                            pltpu.VMEM((2, d, tf), F32),
                            pltpu.VMEM((2, tf, d), F32),
                            pltpu.SemaphoreType.DMA(()),
                            pltpu.SemaphoreType.DMA((2,)),
                            pltpu.SemaphoreType.DMA((2, 3))],
        ),
        out_shape=jax.ShapeDtypeStruct((cap * slab, LANES), U32),
        compiler_params=pltpu.CompilerParams(
            dimension_semantics=("arbitrary",),
            vmem_limit_bytes=VMEM_LIMIT_BYTES,
            has_side_effects=True),
        name="ffn",
    )(item_e, item_st, item_nb, used_blocks, xs, w_gu, w_down, b_gu.reshape(n_e, 1, f2),
      b_down.reshape(n_e, 1, d))


def _final_kernel(dcur_ref, dnxt_ref, gate_ref, x1_ref, p_ref, y_hbm,
                  g2_ref, b2_ref, wpg_ref, bpg_ref, wple_ref, g3_ref, b3_ref,
                  o_ref, ybuf, ffn_ref, gbuf, sem, *, alpha, slab):
    tm = x1_ref.shape[0]
    half = slab * LANES
    i = pl.program_id(0)
    n_steps = pl.num_programs(0)
    slot = i % 2
    slot_tokens = TOP_K * tm
    slot_rows = slot_tokens * slab

    def gather(d_ref, into):
        def start(t, c):
            for k in range(TOP_K):
                src = d_ref[k, t]
                dst = (into * slot_tokens + k * tm + t) * slab
                pltpu.make_async_copy(
                    y_hbm.at[pl.ds(pl.multiple_of(src * slab, slab), slab)],
                    ybuf.at[pl.ds(pl.multiple_of(dst, slab), slab)], sem.at[into],
                ).start(priority=k % 2)
            return c
        lax.fori_loop(0, tm, start, 0, unroll=4)

    @pl.when(i == 0)
    def _():
        gather(dcur_ref, slot)

    @pl.when(i + 1 < n_steps)
    def _():
        gather(dnxt_ref, 1 - slot)

    pltpu.make_async_copy(
        y_hbm.at[pl.ds(0, slot_rows)],
        ybuf.at[pl.ds(pl.multiple_of(slot * slot_rows, slot_rows), slot_rows)], sem.at[slot]).wait()

    for k in range(TOP_K):
        gbuf[k] = jnp.broadcast_to(gate_ref[:, k:k + 1], (tm, LANES))

    hr = tm // FINAL_SPLIT
    for part in range(FINAL_SPLIT):
        rs = slice(part * hr, (part + 1) * hr)
        for col in range(slab):
            acc_hi = acc_lo = None
            for k in range(TOP_K):
                hi, lo = _load_slab_column(
                    ybuf, slot * slot_tokens + k * tm + part * hr, hr, slab, col)
                g = gbuf[k, rs, :]
                acc_hi = hi * g if acc_hi is None else acc_hi + hi * g
                acc_lo = lo * g if acc_lo is None else acc_lo + lo * g
            ffn_ref[rs, col * LANES:(col + 1) * LANES] = acc_hi
            ffn_ref[rs, half + col * LANES:half + (col + 1) * LANES] = acc_lo
        h2 = _layer_norm(alpha * x1_ref[rs, :] + ffn_ref[rs, :], g2_ref[...], b2_ref[...])
        pg = _sigmoid(_dot(h2.astype(BF16), wpg_ref[...]) + bpg_ref[...])
        pe = _dot(p_ref[rs, :].astype(BF16), wple_ref[...])
        o_ref[rs, :] = _layer_norm(alpha * h2 + pg * pe, g3_ref[...], b3_ref[...])


def _final(dest, gates_t, x1, p2, y, g2, b2, wpg, bpg, wple, g3, b3, cap, *, alpha):
    n, d = x1.shape
    slab = y.shape[0] // cap
    tm = MIX_TM
    n_steps = n // tm
    dest_cur = pl.BlockSpec((TOP_K, tm), lambda i: (0, i), memory_space=pltpu.SMEM)
    dest_next = pl.BlockSpec((TOP_K, tm), lambda i: (0, jnp.minimum(i + 1, n_steps - 1)),
                             memory_space=pltpu.SMEM)
    const = lambda i: (0, 0)
    row_blk = lambda width: pl.BlockSpec((tm, width), lambda i: (i, 0))
    return pl.pallas_call(
        functools.partial(_final_kernel, alpha=alpha, slab=slab),
        grid_spec=pltpu.PrefetchScalarGridSpec(
            num_scalar_prefetch=0,
            grid=(n_steps,),
            in_specs=[dest_cur, dest_next, row_blk(TOP_K), row_blk(d), row_blk(p2.shape[1]),
                      pl.BlockSpec(memory_space=pl.ANY),
                      pl.BlockSpec(g2.shape, const), pl.BlockSpec(b2.shape, const),
                      pl.BlockSpec(wpg.shape, const), pl.BlockSpec(bpg.shape, const),
                      pl.BlockSpec(wple.shape, const),
                      pl.BlockSpec(g3.shape, const), pl.BlockSpec(b3.shape, const)],
            out_specs=row_blk(d),
            scratch_shapes=[pltpu.VMEM((2 * TOP_K * tm * slab, LANES), U32),
                            pltpu.VMEM((tm, d), F32),
                            pltpu.VMEM((TOP_K, tm, LANES), F32),
                            pltpu.SemaphoreType.DMA((2,))],
        ),
        out_shape=jax.ShapeDtypeStruct((n, d), F32),
        compiler_params=pltpu.CompilerParams(
            dimension_semantics=("arbitrary",),
            vmem_limit_bytes=VMEM_LIMIT_BYTES),
        name="final",
    )(dest, dest, gates_t, x1, p2, y, g2, b2, wpg, bpg, wple, g3, b3)


def _plan(counts, n_items):
    nblk = (counts + ROW_BLK - 1) // ROW_BLK
    blk_end = jnp.cumsum(nblk)
    blk_start = blk_end - nblk
    items_per = (nblk + ITEM_BLKS - 1) // ITEM_BLKS
    item_end = jnp.cumsum(items_per)
    item_start = item_end - items_per
    total = item_end[-1]
    w = jnp.arange(n_items, dtype=jnp.int32)
    w_eff = jnp.minimum(w, total - 1)
    e = jnp.sum((item_end[None, :] <= w_eff[:, None]).astype(jnp.int32), axis=1)
    e = jnp.minimum(e, N_EXPERTS - 1)
    j = w_eff - item_start[e]
    st = blk_start[e] + j * ITEM_BLKS
    nb = jnp.where(w < total, jnp.clip(nblk[e] - j * ITEM_BLKS, 0, ITEM_BLKS), 0)
    return ((blk_start * ROW_BLK).astype(jnp.int32), (blk_end * ROW_BLK).astype(jnp.int32),
            e, st.astype(jnp.int32), nb.astype(jnp.int32), total.reshape(1).astype(jnp.int32))


def kernel(x, p, w_in, w_pool, pool_scale, conv_w, w_br_a, w_br_b, w_o, ln1_g, ln1_b,
           w_router, b_router, w_gu, b_gu, w_down, b_down, ln2_g, ln2_b,
           w_pg, b_pg, w_ple, ln3_g, ln3_b):
    depth = w_in.shape[0]
    bsz, seq, d = x.shape
    n = bsz * seq
    alpha = (2.0 * depth) ** 0.25
    nk = n * TOP_K
    n_row_blocks = (nk + N_EXPERTS * (ROW_BLK - 1) + ROW_BLK - 1) // ROW_BLK
    cap = n_row_blocks * ROW_BLK
    n_items = N_EXPERTS + n_row_blocks // ITEM_BLKS
    row = lambda v: v.reshape(1, -1)

    h = x.reshape(n, d)
    for i in range(depth):
        proj = _proj(h, w_in[i])
        x1, x1p, tope, gates, rank, cnt = _mixer(
            proj, h, w_pool[i].astype(BF16), row(pool_scale[i]), conv_w[i],
            w_br_a[i].astype(BF16), w_br_b[i].astype(BF16), w_o[i].astype(BF16),
            row(ln1_g[i]), row(ln1_b[i]), w_router[i].T.astype(BF16),
            b_router[i].reshape(N_EXPERTS, 1), seq=seq, alpha=alpha)
        pad_starts, pad_ends, item_e, item_st, item_nb, used_items = _plan(cnt[:, 0], n_items)
        dest = _dest(pad_starts, tope, rank)
        xs = _dispatch(pad_starts, pad_ends, dest, x1p, n, cap)
        used_blocks = pad_ends[N_EXPERTS - 1:] // ROW_BLK
        y = _ffn(item_e, item_st, item_nb, used_items, used_blocks, xs,
                 w_gu[i], b_gu[i], w_down[i], b_down[i], cap)
        h = _final(dest, gates.T, x1, p[i].reshape(n, -1), y,
                   row(ln2_g[i]), row(ln2_b[i]), w_pg[i].astype(BF16), row(b_pg[i]),
                   w_ple[i].astype(BF16), row(ln3_g[i]), row(ln3_b[i]), cap, alpha=alpha)
    return h.reshape(bsz, seq, d)
```
